```python
import math
import jax, jax.numpy as jnp
from jax import lax
import numpy as np

D_MODEL = 1024
BATCH = 8
SEQ = 2048
DEPTH = 1
DEC_BATCH = 128
DEC_SEQ = 4
PAST_LEN = 8192
PAGE_SIZE = 128

HEAD_DIM = 64
EPS = 1e-6
ROPE_THETA = 10000.0
SSM_WIDTH = 512
SSM_GROUP = 16
SSM_GROUPS = SSM_WIDTH // SSM_GROUP
SSM_STATE = 64
DT_MIN = 1e-3
DT_MAX = 1e-1
SWA_Q_HEADS = 4
SWA_KV_HEADS = 2
SWA_REP = SWA_Q_HEADS // SWA_KV_HEADS
SWA_WIDTH = SWA_Q_HEADS * HEAD_DIM
SWA_KV_WIDTH = SWA_KV_HEADS * HEAD_DIM
WINDOW = 128
MEM_TOKENS = 256
MEM_HEADS = 4
MEM_WIDTH = MEM_HEADS * HEAD_DIM
MIX_WIDTH = SSM_WIDTH + SWA_WIDTH + MEM_WIDTH
IN_WIDTH = SSM_WIDTH + SWA_WIDTH + 2 * SWA_KV_WIDTH + MEM_WIDTH
SPLITS = (SSM_WIDTH, SSM_WIDTH + SWA_WIDTH, SSM_WIDTH + SWA_WIDTH + SWA_KV_WIDTH,
          SSM_WIDTH + SWA_WIDTH + 2 * SWA_KV_WIDTH)
PEER_HEADS = 8
PEER_KEYS = 128
PEER_EXPERTS = PEER_KEYS * PEER_KEYS
PEER_TOPK = 16
PEER_QDIM = 256
PEER_HALF = PEER_QDIM // 2
PEER_CHUNK = 256

kernel_name = 'hymba_s5_swa_sink_memx_peer_step'


def rms_norm(x, g):
    xf = x.astype(jnp.float32)
    y = xf * lax.rsqrt(jnp.mean(xf * xf, axis=-1, keepdims=True) + EPS)
    return (y * g.astype(jnp.float32)).astype(x.dtype)


def rope(x, pos):
    half = x.shape[-1] // 2
    inv = ROPE_THETA ** (-jnp.arange(half, dtype=jnp.float32) / half)
    ang = pos.astype(jnp.float32)[:, None] * inv[None, :]
    cos = jnp.cos(ang)[:, None, :]
    sin = jnp.sin(ang)[:, None, :]
    xf = x.astype(jnp.float32)
    x1, x2 = xf[..., :half], xf[..., half:]
    return jnp.concatenate([x1 * cos - x2 * sin, x2 * cos + x1 * sin], axis=-1).astype(x.dtype)


def _complex_affine_combine(left, right):
    ar1, ai1, br1, bi1 = left
    ar2, ai2, br2, bi2 = right
    return (ar2 * ar1 - ai2 * ai1, ar2 * ai1 + ai2 * ar1,
            ar2 * br1 - ai2 * bi1 + br2, ar2 * bi1 + ai2 * br1 + bi2)


def s5_mixer(u, s0_re, s0_im, log_dt, a_re, a_im, b_re, b_im, c_re, c_im, d_skip, w_glu, b_glu):
    f32 = jnp.float32
    B, T, _ = u.shape
    uf = u.astype(f32).reshape(B, T, SSM_GROUPS, SSM_GROUP)
    dt = jnp.exp(log_dt.astype(f32))
    ar, ai = a_re.astype(f32), a_im.astype(f32)
    mag = jnp.exp(ar * dt)
    lam_re = mag * jnp.cos(ai * dt)
    lam_im = mag * jnp.sin(ai * dt)
    den = ar * ar + ai * ai
    z_re = ((lam_re - 1.0) * ar + lam_im * ai) / den
    z_im = (lam_im * ar - (lam_re - 1.0) * ai) / den
    br, bi = b_re.astype(f32), b_im.astype(f32)
    bb_re = z_re[..., None] * br - z_im[..., None] * bi
    bb_im = z_re[..., None] * bi + z_im[..., None] * br
    bu_re = jnp.einsum('gnc,btgc->btgn', bb_re, uf)
    bu_im = jnp.einsum('gnc,btgc->btgn', bb_im, uf)
    s0r, s0i = s0_re.astype(f32), s0_im.astype(f32)
    bu_re = bu_re.at[:, 0].add(lam_re * s0r - lam_im * s0i)
    bu_im = bu_im.at[:, 0].add(lam_re * s0i + lam_im * s0r)
    lr = jnp.broadcast_to(lam_re, bu_re.shape)
    li = jnp.broadcast_to(lam_im, bu_im.shape)
    _, _, s_re, s_im = lax.associative_scan(_complex_affine_combine, (lr, li, bu_re, bu_im), axis=1)
    y = (jnp.einsum('gcn,btgn->btgc', c_re.astype(f32), s_re)
         - jnp.einsum('gcn,btgn->btgc', c_im.astype(f32), s_im)
         + d_skip.astype(f32) * uf)
    y = jax.nn.gelu(y.reshape(B, T, SSM_WIDTH), approximate=False)
    y = y * jax.nn.sigmoid(y @ w_glu.astype(f32) + b_glu.astype(f32))
    return y.astype(u.dtype), s_re[:, -1], s_im[:, -1]


def sink_attention(q, k, v, qpos, kpos, sinks):
    s = jnp.einsum('...qkrd,...skd->...krqs', q, k, preferred_element_type=jnp.float32) * HEAD_DIM ** -0.5
    diff = qpos[..., :, None] - kpos[..., None, :]
    valid = (diff >= 0) & (diff < WINDOW) & (kpos[..., None, :] >= 0)
    s = jnp.where(valid[..., None, None, :, :], s, -jnp.inf)
    sk = sinks.astype(jnp.float32).reshape(SWA_KV_HEADS, SWA_REP)[:, :, None, None]
    m = jnp.maximum(jnp.max(s, axis=-1, keepdims=True), sk)
    p = jnp.exp(s - m)
    p = p / (jnp.sum(p, axis=-1, keepdims=True) + jnp.exp(sk - m))
    return jnp.einsum('...krqs,...skd->...qkrd', p.astype(v.dtype), v)


def swa_prompt(q, k, v, sinks):
    B, T = q.shape[0], q.shape[1]
    nb = T // WINDOW
    qb = q.reshape(B, nb, WINDOW, SWA_KV_HEADS, SWA_REP, HEAD_DIM)
    kb = k.reshape(B, nb, WINDOW, SWA_KV_HEADS, HEAD_DIM)
    vb = v.reshape(B, nb, WINDOW, SWA_KV_HEADS, HEAD_DIM)
    padw = ((0, 0), (1, 0), (0, 0), (0, 0), (0, 0))
    kk = jnp.concatenate([jnp.pad(kb, padw)[:, :-1], kb], axis=2)
    vv = jnp.concatenate([jnp.pad(vb, padw)[:, :-1], vb], axis=2)
    pos = jnp.arange(T, dtype=jnp.int32).reshape(nb, WINDOW)
    kpos = jnp.concatenate([pos - WINDOW, pos], axis=1)
    o = sink_attention(qb, kk, vv, pos, kpos, sinks)
    return o.reshape(B, T, SWA_WIDTH)


def memory_kv(mem, mem_norm_g, w_mem_kv, mem_k_norm):
    B, M, _ = mem.shape
    kv = (rms_norm(mem, mem_norm_g) @ w_mem_kv).reshape(B, M, 2, MEM_HEADS, HEAD_DIM)
    return rms_norm(kv[:, :, 0], mem_k_norm), kv[:, :, 1]


def memory_attention(q, k, v):
    s = jnp.einsum('bthd,bmhd->bhtm', q, k, preferred_element_type=jnp.float32) * HEAD_DIM ** -0.5
    p = jax.nn.softmax(s, axis=-1).astype(v.dtype)
    o = jnp.einsum('bhtm,bmhd->bthd', p, v)
    return o.reshape(o.shape[0], o.shape[1], MEM_WIDTH)


def peer_ffn(xn, wq, k1, k2, u_tab, v_tab):
    shp = xn.shape
    x2 = xn.reshape(-1, D_MODEL)
    n = x2.shape[0]
    c = PEER_CHUNK
    pad = (-n) % c
    xp = jnp.pad(x2, ((0, pad), (0, 0))).reshape(-1, c, D_MODEL)

    def block(xc):
        q = (xc @ wq).reshape(c, PEER_HEADS, 2, PEER_HALF)
        s1 = jnp.einsum('chd,kd->chk', q[:, :, 0], k1, preferred_element_type=jnp.float32)
        s2 = jnp.einsum('chd,kd->chk', q[:, :, 1], k2, preferred_element_type=jnp.float32)
        v1, i1 = lax.top_k(s1, PEER_TOPK)
        v2, i2 = lax.top_k(s2, PEER_TOPK)
        cand = (v1[..., :, None] + v2[..., None, :]).reshape(c, PEER_HEADS, PEER_TOPK * PEER_TOPK)
        sc, ci = lax.top_k(cand, PEER_TOPK)
        ia = jnp.take_along_axis(i1, ci // PEER_TOPK, axis=-1)
        ib = jnp.take_along_axis(i2, ci % PEER_TOPK, axis=-1)
        expert = ia * PEER_KEYS + ib
        gate = jax.nn.softmax(sc, axis=-1)
        a = jnp.einsum('chkd,cd->chk', u_tab[expert], xc, preferred_element_type=jnp.float32)
        wgt = (gate * jax.nn.gelu(a, approximate=False)).astype(xc.dtype)
        return jnp.einsum('chk,chkd->cd', wgt, v_tab[expert])

    y = lax.map(block, xp).reshape(-1, D_MODEL)[:n]
    return y.reshape(shp)


def hybrid_layer(x, s0_re, s0_im, win_k, win_v, mem_k, mem_v, start,
                 norm1_g, w_in, log_dt, a_re, a_im, b_re, b_im, c_re, c_im, d_skip, w_glu, b_glu,
                 swa_q_norm, swa_k_norm, swa_sinks, mem_q_norm, w_out, norm2_g,
                 peer_wq, peer_k1, peer_k2, peer_u, peer_v):
    B, T, _ = x.shape
    pos = start + jnp.arange(T, dtype=jnp.int32)
    xn = rms_norm(x, norm1_g)
    u, q, k, v, qm = jnp.split(xn @ w_in, SPLITS, axis=-1)
    o_ssm, s_re, s_im = s5_mixer(u, s0_re, s0_im, log_dt, a_re, a_im, b_re, b_im,
                                 c_re, c_im, d_skip, w_glu, b_glu)
    q = rope(rms_norm(q.reshape(B, T, SWA_Q_HEADS, HEAD_DIM), swa_q_norm), pos)
    k = rope(rms_norm(k.reshape(B, T, SWA_KV_HEADS, HEAD_DIM), swa_k_norm), pos)
    v = v.reshape(B, T, SWA_KV_HEADS, HEAD_DIM)
    q = q.reshape(B, T, SWA_KV_HEADS, SWA_REP, HEAD_DIM)
    w = min(WINDOW, PAST_LEN)
    if win_k is None:
        o_swa = swa_prompt(q, k, v, swa_sinks)
        new_k, new_v = k[:, T - w:], v[:, T - w:]
    else:
        k_all = jnp.concatenate([win_k.astype(k.dtype), k], axis=1)
        v_all = jnp.concatenate([win_v.astype(v.dtype), v], axis=1)
        kpos = jnp.concatenate([start - w + jnp.arange(w, dtype=jnp.int32), pos])
        o_swa = sink_attention(q, k_all, v_all, pos, kpos, swa_sinks).reshape(B, T, SWA_WIDTH)
        new_k, new_v = k_all[:, -w:], v_all[:, -w:]
    qm = rms_norm(qm.reshape(B, T, MEM_HEADS, HEAD_DIM), mem_q_norm)
    o_mem = memory_attention(qm, mem_k.astype(qm.dtype), mem_v.astype(qm.dtype))
    mixed = jnp.concatenate([o_ssm, o_swa.astype(x.dtype), o_mem.astype(x.dtype)], axis=-1)
    x = x + mixed @ w_out
    x = x + peer_ffn(rms_norm(x, norm2_g), peer_wq, peer_k1, peer_k2, peer_u, peer_v)
    return x, s_re, s_im, new_k, new_v


def setup_inputs(seed: int = 0) -> dict:
    key = jax.random.key(seed)
    ks = iter(jax.random.split(key, 48))
    f32 = jnp.float32

    def nrm(shape, scale):
        return scale * jax.random.normal(next(ks), shape, f32)

    def gain(shape):
        return 1.0 + 0.05 * jax.random.normal(next(ks), shape, f32)

    L = DEPTH
    w = min(WINDOW, PAST_LEN)
    G, N, C = SSM_GROUPS, SSM_STATE, SSM_GROUP
    n_idx = jnp.arange(N, dtype=f32)
    return {
        'x_prompt': nrm((BATCH, SEQ, D_MODEL), 1.0),
        'x_sample': nrm((DEC_BATCH, DEC_SEQ, D_MODEL), 1.0),
        'state_ssm_re': nrm((L, DEC_BATCH, G, N), 0.1),
        'state_ssm_im': nrm((L, DEC_BATCH, G, N), 0.1),
        'cache_win_k': nrm((L, DEC_BATCH, w, SWA_KV_HEADS, HEAD_DIM), 1.0),
        'cache_win_v': nrm((L, DEC_BATCH, w, SWA_KV_HEADS, HEAD_DIM), 1.0),
        'cache_mem_k': nrm((L, DEC_BATCH, MEM_TOKENS, MEM_HEADS, HEAD_DIM), 1.0),
        'cache_mem_v': nrm((L, DEC_BATCH, MEM_TOKENS, MEM_HEADS, HEAD_DIM), 1.0),
        'mem_prompt': nrm((BATCH, MEM_TOKENS, D_MODEL), 1.0),
        'norm1_g': gain((L, D_MODEL)),
        'w_in': nrm((L, D_MODEL, IN_WIDTH), D_MODEL ** -0.5),
        'ssm_log_dt': jax.random.uniform(next(ks), (L, G, N), f32, math.log(DT_MIN), math.log(DT_MAX)),
        'ssm_a_re': -0.5 * jnp.exp(0.02 * jax.random.normal(next(ks), (L, G, N), f32)),
        'ssm_a_im': math.pi * n_idx + 0.01 * jax.random.normal(next(ks), (L, G, N), f32),
        'ssm_b_re': nrm((L, G, N, C), C ** -0.5),
        'ssm_b_im': nrm((L, G, N, C), C ** -0.5),
        'ssm_c_re': nrm((L, G, C, N), 0.5),
        'ssm_c_im': nrm((L, G, C, N), 0.5),
        'ssm_d': nrm((L, G, C), 1.0),
        'w_glu': nrm((L, SSM_WIDTH, SSM_WIDTH), SSM_WIDTH ** -0.5),
        'b_glu': nrm((L, SSM_WIDTH), 0.02),
        'swa_q_norm': gain((L, HEAD_DIM)),
        'swa_k_norm': gain((L, HEAD_DIM)),
        'swa_sinks': nrm((L, SWA_Q_HEADS), 0.5),
        'mem_norm_g': gain((L, D_MODEL)),
        'w_mem_kv': nrm((L, D_MODEL, 2 * MEM_WIDTH), D_MODEL ** -0.5),
        'mem_q_norm': gain((L, HEAD_DIM)),
        'mem_k_norm': gain((L, HEAD_DIM)),
        'w_out': nrm((L, MIX_WIDTH, D_MODEL), MIX_WIDTH ** -0.5),
        'norm2_g': gain((L, D_MODEL)),
        'peer_wq': nrm((L, D_MODEL, PEER_HEADS * PEER_QDIM), D_MODEL ** -0.5),
        'peer_k1': nrm((L, PEER_KEYS, PEER_HALF), PEER_HALF ** -0.5),
        'peer_k2': nrm((L, PEER_KEYS, PEER_HALF), PEER_HALF ** -0.5),
        'peer_u': nrm((L, PEER_EXPERTS, D_MODEL), D_MODEL ** -0.5),
        'peer_v': nrm((L, PEER_EXPERTS, D_MODEL), PEER_HEADS ** -0.5),
    }


def reference(x_prompt, x_sample, state_ssm_re, state_ssm_im, cache_win_k, cache_win_v,
              cache_mem_k, cache_mem_v, mem_prompt, norm1_g, w_in, ssm_log_dt, ssm_a_re, ssm_a_im,
              ssm_b_re, ssm_b_im, ssm_c_re, ssm_c_im, ssm_d, w_glu, b_glu, swa_q_norm, swa_k_norm,
              swa_sinks, mem_norm_g, w_mem_kv, mem_q_norm, mem_k_norm, w_out, norm2_g,
              peer_wq, peer_k1, peer_k2, peer_u, peer_v):
    y_p, y_s = x_prompt, x_sample
    p_sr, p_si, p_wk, p_wv, p_mk, p_mv = [], [], [], [], [], []
    s_sr, s_si, s_wk, s_wv = [], [], [], []
    for l in range(DEPTH):
        shared = (norm1_g[l], w_in[l], ssm_log_dt[l], ssm_a_re[l], ssm_a_im[l], ssm_b_re[l],
                  ssm_b_im[l], ssm_c_re[l], ssm_c_im[l], ssm_d[l], w_glu[l], b_glu[l],
                  swa_q_norm[l], swa_k_norm[l], swa_sinks[l], mem_q_norm[l], w_out[l], norm2_g[l],
                  peer_wq[l], peer_k1[l], peer_k2[l], peer_u[l], peer_v[l])
        mk, mv = memory_kv(mem_prompt, mem_norm_g[l], w_mem_kv[l], mem_k_norm[l])
        zeros = jnp.zeros((y_p.shape[0], SSM_GROUPS, SSM_STATE), jnp.float32)
        y_p, sr, si, wk, wv = hybrid_layer(y_p, zeros, zeros, None, None, mk, mv, 0, *shared)
        p_sr.append(sr); p_si.append(si); p_wk.append(wk); p_wv.append(wv)
        p_mk.append(mk); p_mv.append(mv)
        y_s, sr, si, wk, wv = hybrid_layer(y_s, state_ssm_re[l], state_ssm_im[l], cache_win_k[l],
                                           cache_win_v[l], cache_mem_k[l], cache_mem_v[l],
                                           PAST_LEN, *shared)
        s_sr.append(sr); s_si.append(si); s_wk.append(wk); s_wv.append(wv)
    return (y_p, y_s,
            jnp.stack(p_sr), jnp.stack(p_si), jnp.stack(p_wk), jnp.stack(p_wv),
            jnp.stack(p_mk), jnp.stack(p_mv),
            jnp.stack(s_sr), jnp.stack(s_si), jnp.stack(s_wk), jnp.stack(s_wv))
```

```python
import functools
import math

import jax
import jax.numpy as jnp
import numpy as np
from jax import lax
from jax.experimental import pallas as pl
from jax.experimental.pallas import tpu as pltpu

F32 = jnp.float32
BF16 = jnp.bfloat16

D_MODEL = 1024
HEAD_DIM = 64
EPS = 1e-6
ROPE_THETA = 10000.0
PAST_LEN = 8192
SSM_WIDTH = 512
SSM_GROUP = 16
SSM_GROUPS = 32
SSM_STATE = 64
SSM_HALF_GROUPS = SSM_GROUPS // 2
SSM_HALF_STATE = SSM_HALF_GROUPS * SSM_STATE
SSM_COLS = 2 * 2 * SSM_HALF_STATE
SWA_Q_HEADS = 4
SWA_KV_HEADS = 2
SWA_REP = 2
SWA_WIDTH = 256
SWA_KV_WIDTH = 128
WINDOW = 128
MEM_TOKENS = 256
MEM_HEADS = 4
MEM_WIDTH = 256
IN_WIDTH = 1280
PEER_HEADS = 8
PEER_KEYS = 128
PEER_EXPERTS = PEER_KEYS * PEER_KEYS
PEER_TOPK = 16
PEER_HALF = 128

LANES = 128
VMEM_LIMIT = 56 * 1024 * 1024

TOKEN_TILE = 512
PEER_EXPERT_BLOCK = 512
PEER_ROW_CHUNK = 32
S5_TIME_TILE = 64

_CAND_COUNT = [PEER_TOPK // (a + 1) for a in range(PEER_TOPK)]
_CAND_ROW0 = [0, 16, 24, 32, 36, 40, 42, 44, 48, 49, 50, 51, 52, 53, 54, 55]
_CAND_ROWS = 56


def _cparams(sem):
    return pltpu.CompilerParams(dimension_semantics=sem, vmem_limit_bytes=VMEM_LIMIT)


def _rms(x, g):
    return x * lax.rsqrt(jnp.mean(x * x, axis=-1, keepdims=True) + EPS) * g


def _gelu(x):
    return 0.5 * x * (1.0 + lax.erf(x * np.float32(math.sqrt(0.5))))


def _dot(a, b):
    return jnp.dot(a, b, preferred_element_type=F32)


def _dot_nt(a, b):
    return lax.dot_general(a, b, (((1,), (1,)), ((), ())), preferred_element_type=F32)


def _div(x, n):
    return x >> (n.bit_length() - 1) if n & (n - 1) == 0 else x // n


def _mod(x, n):
    return x & (n - 1) if n & (n - 1) == 0 else x % n


def _head_rms(x, ones_bd, g):
    sq = x * x
    hi = sq.astype(BF16)
    lo = (sq - hi.astype(F32)).astype(BF16)
    ms = (_dot(hi, ones_bd) + _dot(lo, ones_bd)) * np.float32(1.0 / HEAD_DIM)
    return x * lax.rsqrt(ms + EPS) * g


def _rope(x, cos, sin_signed):
    w = x.shape[-1]
    lane = lax.broadcasted_iota(jnp.int32, x.shape, 1)
    first_half = _mod(lane, HEAD_DIM) < (HEAD_DIM // 2)
    partner = jnp.where(first_half, pltpu.roll(x, w - HEAD_DIM // 2, 1),
                        pltpu.roll(x, HEAD_DIM // 2, 1))
    return x * cos + partner * sin_signed


def _in_proj_kernel(x_ref, g1_ref, win_ref, ones_ref, gq_ref, gk_ref, gm_ref, cos_ref, sin_ref,
                    u_ref, q_ref, k_ref, v_ref, qm_ref):
    x = x_ref[0]
    xn = _rms(x, g1_ref[...])
    proj = _dot(xn.astype(BF16), win_ref[...])
    u_ref[...] = proj[:, :SSM_WIDTH]
    q = proj[:, 512:768]
    k = proj[:, 768:896]
    v_ref[0] = proj[:, 896:1024]
    qm = proj[:, 1024:1280]
    ones = ones_ref[...]
    cos = cos_ref[...]
    sin = sin_ref[...]
    q_ref[0] = _rope(_head_rms(q, ones, gq_ref[...]), cos, sin)
    k_ref[0] = _rope(_head_rms(k, ones[:SWA_KV_WIDTH, :SWA_KV_WIDTH], gk_ref[...]),
                     cos[:, :SWA_KV_WIDTH], sin[:, :SWA_KV_WIDTH])
    qm_ref[0] = _head_rms(qm, ones, gm_ref[...])


def _in_proj(x, cos, sin, g1, win, ones, gq, gk, gm):
    B, T, _ = x.shape
    tt = min(TOKEN_TILE, T)
    grid = (B, T // tt)
    full = lambda shape: pl.BlockSpec(shape, lambda b, t: (0,) * len(shape))
    return pl.pallas_call(
        _in_proj_kernel,
        grid=grid,
        in_specs=[
            pl.BlockSpec((1, tt, D_MODEL), lambda b, t: (b, t, 0)),
            full((1, D_MODEL)), full((D_MODEL, IN_WIDTH)), full((SWA_WIDTH, SWA_WIDTH)),
            full((1, SWA_WIDTH)), full((1, SWA_KV_WIDTH)), full((1, MEM_WIDTH)),
            pl.BlockSpec((tt, SWA_WIDTH), lambda b, t: (t, 0)),
            pl.BlockSpec((tt, SWA_WIDTH), lambda b, t: (t, 0)),
        ],
        out_specs=[
            pl.BlockSpec((tt, SSM_WIDTH), lambda b, t: (t, b)),
            pl.BlockSpec((1, tt, SWA_WIDTH), lambda b, t: (b, t, 0)),
            pl.BlockSpec((1, tt, SWA_KV_WIDTH), lambda b, t: (b, t, 0)),
            pl.BlockSpec((1, tt, SWA_KV_WIDTH), lambda b, t: (b, t, 0)),
            pl.BlockSpec((1, tt, MEM_WIDTH), lambda b, t: (b, t, 0)),
        ],
        out_shape=[
            jax.ShapeDtypeStruct((T, B * SSM_WIDTH), F32),
            jax.ShapeDtypeStruct((B, T, SWA_WIDTH), F32),
            jax.ShapeDtypeStruct((B, T, SWA_KV_WIDTH), F32),
            jax.ShapeDtypeStruct((B, T, SWA_KV_WIDTH), F32),
            jax.ShapeDtypeStruct((B, T, MEM_WIDTH), F32),
        ],
        compiler_params=_cparams(("parallel", "parallel")),
        name="in_proj",
    )(x, g1, win, ones, gq, gk, gm, cos, sin)


def _s5_kernel(u_ref, s0_ref, lam_ref, bmat_ref, cmat_ref, d_ref, wglu_ref, bglu_ref,
               o_ref, sfin_ref, s_scr, carry_scr, *, bt, tt):
    @pl.when(pl.program_id(0) == 0)
    def _():
        carry_scr[...] = s0_ref[...]

    u = u_ref[...]
    ub = u.astype(BF16)
    hw = 2 * SSM_HALF_STATE
    for j in range(2):
        s_scr[:, j * hw:(j + 1) * hw] = _dot(ub[:, j * 256:(j + 1) * 256], bmat_ref[j])

    def step(t, carry):
        r0 = pl.multiple_of(t * bt, bt)
        for j in range(2):
            c_re = pl.ds(j * hw, SSM_HALF_STATE)
            c_im = pl.ds(j * hw + SSM_HALF_STATE, SSM_HALF_STATE)
            p_re = carry_scr[:, c_re]
            p_im = carry_scr[:, c_im]
            l_re = lam_ref[:, c_re]
            l_im = lam_ref[:, c_im]
            n_re = l_re * p_re - l_im * p_im + s_scr[pl.ds(r0, bt), c_re]
            n_im = l_re * p_im + l_im * p_re + s_scr[pl.ds(r0, bt), c_im]
            s_scr[pl.ds(r0, bt), c_re] = n_re
            s_scr[pl.ds(r0, bt), c_im] = n_im
            carry_scr[:, c_re] = n_re
            carry_scr[:, c_im] = n_im
        return carry

    lax.fori_loop(0, tt, step, 0)
    sfin_ref[...] = carry_scr[...]

    ys = [_dot(s_scr[:, j * hw:(j + 1) * hw].astype(BF16), cmat_ref[j]) for j in range(2)]
    y = jnp.concatenate(ys, axis=-1) + d_ref[...] * u
    y = _gelu(y)
    z = _dot(y.astype(BF16), wglu_ref[...]) + bglu_ref[...]
    o_ref[...] = y * jax.nn.sigmoid(z)


def _s5(u_tm, s0, lam, bmat, cmat, d, wglu, bglu, *, bt, tt):
    rows = u_tm.shape[0]
    nt = rows // (bt * tt)
    full = lambda shape: pl.BlockSpec(shape, lambda t: (0,) * len(shape))
    return pl.pallas_call(
        functools.partial(_s5_kernel, bt=bt, tt=tt),
        grid=(nt,),
        in_specs=[
            pl.BlockSpec((bt * tt, SSM_WIDTH), lambda t: (t, 0)),
            full((bt, SSM_COLS)), full((1, SSM_COLS)),
            full((2, 256, 2 * SSM_HALF_STATE)), full((2, 2 * SSM_HALF_STATE, 256)),
            full((1, SSM_WIDTH)), full((SSM_WIDTH, SSM_WIDTH)), full((1, SSM_WIDTH)),
        ],
        out_specs=[
            pl.BlockSpec((bt * tt, SSM_WIDTH), lambda t: (t, 0)),
            full((bt, SSM_COLS)),
        ],
        out_shape=[
            jax.ShapeDtypeStruct((rows, SSM_WIDTH), F32),
            jax.ShapeDtypeStruct((bt, SSM_COLS), F32),
        ],
        scratch_shapes=[pltpu.VMEM((bt * tt, SSM_COLS), F32), pltpu.VMEM((bt, SSM_COLS), F32)],
        compiler_params=_cparams(("arbitrary",)),
        name="s5_mixer",
    )(u_tm, s0, lam, bmat, cmat, d, wglu, bglu)


def _mem_kv_kernel(m_ref, g_ref, w_ref, ones_ref, gk_ref, k_ref, v_ref):
    xn = _rms(m_ref[...], g_ref[...])
    kv = _dot(xn.astype(BF16), w_ref[...])
    k_ref[...] = _head_rms(kv[:, :MEM_WIDTH], ones_ref[...], gk_ref[...])
    v_ref[...] = kv[:, MEM_WIDTH:]


def _mem_kv(mem_rows, g, w, ones, gk):
    rows = mem_rows.shape[0]
    tt = min(TOKEN_TILE, rows)
    full = lambda shape: pl.BlockSpec(shape, lambda t: (0,) * len(shape))
    return pl.pallas_call(
        _mem_kv_kernel,
        grid=(rows // tt,),
        in_specs=[pl.BlockSpec((tt, D_MODEL), lambda t: (t, 0)), full((1, D_MODEL)),
                  full((D_MODEL, 2 * MEM_WIDTH)), full((MEM_WIDTH, MEM_WIDTH)), full((1, MEM_WIDTH))],
        out_specs=[pl.BlockSpec((tt, MEM_WIDTH), lambda t: (t, 0))] * 2,
        out_shape=[jax.ShapeDtypeStruct((rows, MEM_WIDTH), F32)] * 2,
        compiler_params=_cparams(("parallel",)),
        name="mem_kv",
    )(mem_rows, g, w, ones, gk)


def _softmax_pv(s, v_b, sink=None):
    m = jnp.max(s, axis=-1, keepdims=True)
    if sink is not None:
        m = jnp.maximum(m, sink)
    p = jnp.exp(s - m)
    den = jnp.sum(p, axis=-1, keepdims=True)
    if sink is not None:
        den = den + jnp.exp(sink - m)
    return _dot((p / den).astype(BF16), v_b)


def _attn_prompt_kernel(sinks_ref, q_ref, kp_ref, kc_ref, vp_ref, vc_ref, qm_ref, mk_ref, mv_ref,
                        osw_ref, omem_ref):
    nb = pl.program_id(1)
    scale = np.float32(HEAD_DIM ** -0.5)
    q = q_ref[0].astype(BF16)
    kk = jnp.concatenate([kp_ref[0], kc_ref[0]], axis=0).astype(BF16)
    vv = jnp.concatenate([vp_ref[0], vc_ref[0]], axis=0).astype(BF16)
    qi = lax.broadcasted_iota(jnp.int32, (WINDOW, 2 * WINDOW), 0)
    ki = lax.broadcasted_iota(jnp.int32, (WINDOW, 2 * WINDOW), 1) - WINDOW
    diff = qi - ki
    valid = (diff >= 0) & (diff < WINDOW) & (nb * WINDOW + ki >= 0)
    outs = []
    for hq in range(SWA_Q_HEADS):
        kv = hq // SWA_REP
        s = _dot_nt(q[:, hq * HEAD_DIM:(hq + 1) * HEAD_DIM],
                    kk[:, kv * HEAD_DIM:(kv + 1) * HEAD_DIM]) * scale
        s = jnp.where(valid, s, -jnp.inf)
        outs.append(_softmax_pv(s, vv[:, kv * HEAD_DIM:(kv + 1) * HEAD_DIM], sinks_ref[hq]))
    osw_ref[0] = jnp.concatenate(outs, axis=-1)

    qm = qm_ref[0].astype(BF16)
    mk = mk_ref[0].astype(BF16)
    mv = mv_ref[0].astype(BF16)
    outs = []
    for h in range(MEM_HEADS):
        sl = slice(h * HEAD_DIM, (h + 1) * HEAD_DIM)
        s = _dot_nt(qm[:, sl], mk[:, sl]) * scale
        outs.append(_softmax_pv(s, mv[:, sl]))
    omem_ref[0] = jnp.concatenate(outs, axis=-1)


def _attn_prompt(sinks, q, k, v, qm, mk, mv):
    B, T, _ = q.shape
    nb = T // WINDOW
    blk = lambda w: pl.BlockSpec((1, WINDOW, w), lambda b, n: (b, n, 0))
    prev = lambda w: pl.BlockSpec((1, WINDOW, w), lambda b, n: (b, jnp.maximum(n - 1, 0), 0))
    memb = pl.BlockSpec((1, MEM_TOKENS, MEM_WIDTH), lambda b, n: (b, 0, 0))
    return pl.pallas_call(
        _attn_prompt_kernel,
        grid=(B, nb),
        in_specs=[pl.BlockSpec(memory_space=pltpu.SMEM),
                  blk(SWA_WIDTH), prev(SWA_KV_WIDTH), blk(SWA_KV_WIDTH), prev(SWA_KV_WIDTH),
                  blk(SWA_KV_WIDTH), blk(MEM_WIDTH), memb, memb],
        out_specs=[blk(SWA_WIDTH), blk(MEM_WIDTH)],
        out_shape=[jax.ShapeDtypeStruct((B, T, SWA_WIDTH), F32),
                   jax.ShapeDtypeStruct((B, T, MEM_WIDTH), F32)],
        compiler_params=_cparams(("parallel", "parallel")),
        name="attn_prompt",
    )(sinks, q, k, k, v, v, qm, mk, mv)


def _attn_sample_kernel(sinks_ref, q_ref, kn_ref, vn_ref, ck_ref, cv_ref, qm_ref, mk_ref, mv_ref,
                        osw_ref, omem_ref, *, bb, ts, start):
    scale = np.float32(HEAD_DIM ** -0.5)
    w = ck_ref.shape[1]
    nq = bb * ts
    q = q_ref[...].astype(BF16)
    kn = kn_ref[...].astype(BF16)
    vn = vn_ref[...].astype(BF16)
    ck = ck_ref[...].reshape(bb * w, SWA_KV_WIDTH).astype(BF16)
    cv = cv_ref[...].reshape(bb * w, SWA_KV_WIDTH).astype(BF16)

    rq = lax.broadcasted_iota(jnp.int32, (nq, bb * w), 0)
    cc = lax.broadcasted_iota(jnp.int32, (nq, bb * w), 1)
    qpos = start + _mod(rq, ts)
    kpos = start - w + _mod(cc, w)
    diff = qpos - kpos
    valid_c = (_div(rq, ts) == _div(cc, w)) & (diff >= 0) & (diff < WINDOW) & (kpos >= 0)
    rq = lax.broadcasted_iota(jnp.int32, (nq, nq), 0)
    cn = lax.broadcasted_iota(jnp.int32, (nq, nq), 1)
    diff = _mod(rq, ts) - _mod(cn, ts)
    valid_n = (_div(rq, ts) == _div(cn, ts)) & (diff >= 0) & (diff < WINDOW)

    outs = []
    for hq in range(SWA_Q_HEADS):
        kv = hq // SWA_REP
        qs = q[:, hq * HEAD_DIM:(hq + 1) * HEAD_DIM]
        ks = slice(kv * HEAD_DIM, (kv + 1) * HEAD_DIM)
        s_c = jnp.where(valid_c, _dot_nt(qs, ck[:, ks]) * scale, -jnp.inf)
        s_n = jnp.where(valid_n, _dot_nt(qs, kn[:, ks]) * scale, -jnp.inf)
        sink = sinks_ref[hq]
        m = jnp.maximum(jnp.maximum(jnp.max(s_c, axis=-1, keepdims=True),
                                    jnp.max(s_n, axis=-1, keepdims=True)), sink)
        p_c = jnp.exp(s_c - m)
        p_n = jnp.exp(s_n - m)
        den = (jnp.sum(p_c, axis=-1, keepdims=True) + jnp.sum(p_n, axis=-1, keepdims=True)
               + jnp.exp(sink - m))
        outs.append(_dot((p_c / den).astype(BF16), cv[:, ks]) + _dot((p_n / den).astype(BF16), vn[:, ks]))
    osw_ref[...] = jnp.concatenate(outs, axis=-1)

    qm = qm_ref[...].astype(BF16)
    mk = mk_ref[...].reshape(bb * MEM_TOKENS, MEM_WIDTH).astype(BF16)
    mv = mv_ref[...].reshape(bb * MEM_TOKENS, MEM_WIDTH).astype(BF16)
    rq = lax.broadcasted_iota(jnp.int32, (nq, bb * MEM_TOKENS), 0)
    cm = lax.broadcasted_iota(jnp.int32, (nq, bb * MEM_TOKENS), 1)
    valid_m = _div(rq, ts) == _div(cm, MEM_TOKENS)
    outs = []
    for h in range(MEM_HEADS):
        sl = slice(h * HEAD_DIM, (h + 1) * HEAD_DIM)
        s = jnp.where(valid_m, _dot_nt(qm[:, sl], mk[:, sl]) * scale, -jnp.inf)
        outs.append(_softmax_pv(s, mv[:, sl]))
    omem_ref[...] = jnp.concatenate(outs, axis=-1)


def _attn_sample(sinks, q, kn, vn, ck, cv, qm, cmk, cmv, *, ts, start):
    B, w, _ = ck.shape
    bb = 8
    rows = lambda wd: pl.BlockSpec((bb * ts, wd), lambda i: (i, 0))
    blk3 = lambda n, wd: pl.BlockSpec((bb, n, wd), lambda i: (i, 0, 0))
    return pl.pallas_call(
        functools.partial(_attn_sample_kernel, bb=bb, ts=ts, start=start),
        grid=(B // bb,),
        in_specs=[pl.BlockSpec(memory_space=pltpu.SMEM),
                  rows(SWA_WIDTH), rows(SWA_KV_WIDTH), rows(SWA_KV_WIDTH),
                  blk3(w, SWA_KV_WIDTH), blk3(w, SWA_KV_WIDTH), rows(MEM_WIDTH),
                  blk3(MEM_TOKENS, MEM_WIDTH), blk3(MEM_TOKENS, MEM_WIDTH)],
        out_specs=[rows(SWA_WIDTH), rows(MEM_WIDTH)],
        out_shape=[jax.ShapeDtypeStruct((B * ts, SWA_WIDTH), F32),
                   jax.ShapeDtypeStruct((B * ts, MEM_WIDTH), F32)],
        compiler_params=_cparams(("parallel",)),
        name="attn_sample",
    )(sinks, q, kn, vn, ck, cv, qm, cmk, cmv)


def _out_proj_kernel(x_ref, ossm_ref, osw_ref, omem_ref, wo_ref, g2_ref, h_ref, xn_ref):
    h = x_ref[0]
    h = h + _dot(ossm_ref[...].astype(BF16), wo_ref[0:512, :])
    h = h + _dot(osw_ref[0].astype(BF16), wo_ref[512:768, :])
    h = h + _dot(omem_ref[0].astype(BF16), wo_ref[768:1024, :])
    h_ref[0] = h
    xn_ref[0] = _rms(h, g2_ref[...]).astype(BF16)


def _out_proj(x, ossm, osw, omem, wo, g2):
    B, T, _ = x.shape
    tt = min(TOKEN_TILE, T)
    full = lambda shape: pl.BlockSpec(shape, lambda b, t: (0,) * len(shape))
    blk = lambda wd: pl.BlockSpec((1, tt, wd), lambda b, t: (b, t, 0))
    return pl.pallas_call(
        _out_proj_kernel,
        grid=(B, T // tt),
        in_specs=[blk(D_MODEL), pl.BlockSpec((tt, SSM_WIDTH), lambda b, t: (t, b)),
                  blk(SWA_WIDTH), blk(MEM_WIDTH), full((D_MODEL, D_MODEL)), full((1, D_MODEL))],
        out_specs=[blk(D_MODEL), blk(D_MODEL)],
        out_shape=[jax.ShapeDtypeStruct((B, T, D_MODEL), F32),
                   jax.ShapeDtypeStruct((B, T, D_MODEL), BF16)],
        compiler_params=_cparams(("parallel", "parallel")),
        name="out_proj",
    )(x, ossm, osw, omem, wo, g2)


def _top16(s, iota):
    work = s
    rank = jnp.full(s.shape, float(PEER_TOPK), F32)
    vals = []
    for a in range(PEER_TOPK):
        m = jnp.max(work, axis=0, keepdims=True)
        idx = jnp.min(jnp.where(work == m, iota, float(PEER_KEYS)), axis=0, keepdims=True)
        sel = iota == idx
        rank = jnp.where(sel, float(a), rank)
        work = jnp.where(sel, -jnp.inf, work)
        vals.append(m)
    return vals, rank


def _peer_route(s1, s2, cand_scr):
    L = s1.shape[1]
    iota = lax.broadcasted_iota(jnp.int32, (PEER_KEYS, L), 0).astype(F32)
    v1, r1 = _top16(s1, iota)
    v2, r2 = _top16(s2, iota)

    cand_scr[...] = jnp.full((_CAND_ROWS, L), -jnp.inf, F32)
    for a in range(PEER_TOPK):
        for b in range(_CAND_COUNT[a]):
            cand_scr[pl.ds(_CAND_ROW0[a] + b, 1), :] = v1[a] + v2[b]
    work = cand_scr[...]
    iota_c = lax.broadcasted_iota(jnp.int32, (_CAND_ROWS, L), 0).astype(F32)
    taken = jnp.zeros((_CAND_ROWS, L), F32)
    m0 = None
    z = None
    for k in range(PEER_TOPK):
        m = jnp.max(work, axis=0, keepdims=True)
        idx = jnp.min(jnp.where(work == m, iota_c, float(_CAND_ROWS)), axis=0, keepdims=True)
        sel = iota_c == idx
        taken = jnp.where(sel, 1.0, taken)
        work = jnp.where(sel, -jnp.inf, work)
        if k == 0:
            m0 = m
            z = jnp.ones_like(m)
        else:
            z = z + jnp.exp(m - m0)

    n1 = jnp.zeros((PEER_KEYS, L), F32)
    for a in range(PEER_TOPK):
        lo, hi = float(_CAND_ROW0[a]), float(_CAND_ROW0[a] + _CAND_COUNT[a])
        cnt = jnp.sum(jnp.where((iota_c >= lo) & (iota_c < hi), taken, 0.0), axis=0, keepdims=True)
        n1 = jnp.where(r1 == float(a), cnt, n1)
    c1 = jnp.exp(s1 - v1[0]) / z
    e2 = jnp.exp(s2 - v2[0])
    return n1, c1, r2, e2


def _peer_kernel(xn_ref, h_ref, wqt_ref, k1_ref, k2_ref, u_ref, vt_ref, y_ref,
                 n1_scr, c1_scr, r2_scr, e2_scr, acc_scr, a_scr, w_scr, cand_scr):
    e = pl.program_id(1)
    ne = pl.num_programs(1)
    tt = xn_ref.shape[0]
    eb = u_ref.shape[0]

    @pl.when(e == 0)
    def _route():
        xn = xn_ref[...]
        for h in range(PEER_HEADS):
            for side, (k_ref, dst) in enumerate(((k1_ref, n1_scr), (k2_ref, r2_scr))):
                r0 = h * 2 * PEER_HALF + side * PEER_HALF
                qt = _dot_nt(wqt_ref[r0:r0 + PEER_HALF, :], xn)
                dst[h] = _dot(k_ref[...], qt.astype(BF16))

        def body(i, carry):
            h = i // (tt // LANES)
            c0 = pl.multiple_of((i % (tt // LANES)) * LANES, LANES)
            lanes = pl.ds(c0, LANES)
            n1, c1, r2, e2 = _peer_route(n1_scr[h, :, lanes], r2_scr[h, :, lanes], cand_scr)
            n1_scr[h, :, lanes] = n1
            c1_scr[h, :, lanes] = c1
            r2_scr[h, :, lanes] = r2
            e2_scr[h, :, lanes] = e2
            return carry

        lax.fori_loop(0, PEER_HEADS * (tt // LANES), body, 0)
        acc_scr[...] = jnp.zeros_like(acc_scr)

    a_scr[...] = _dot_nt(u_ref[...], xn_ref[...])

    n_chunks = PEER_KEYS // PEER_ROW_CHUNK

    def gate_body(i, carry):
        ii = i // n_chunks
        r0 = pl.multiple_of((i % n_chunks) * PEER_ROW_CHUNK, PEER_ROW_CHUNK)
        key1 = e * (eb // PEER_KEYS) + ii
        rows2 = pl.ds(r0, PEER_ROW_CHUNK)
        g = jnp.zeros((PEER_ROW_CHUNK, tt), F32)
        for h in range(PEER_HEADS):
            n_row = n1_scr[h, pl.ds(key1, 1), :]
            c_row = c1_scr[h, pl.ds(key1, 1), :]
            g = g + jnp.where(r2_scr[h, rows2, :] < n_row, e2_scr[h, rows2, :], 0.0) * c_row
        rows_e = pl.ds(pl.multiple_of(ii * PEER_KEYS + r0, PEER_ROW_CHUNK), PEER_ROW_CHUNK)
        w_scr[rows_e, :] = (g * _gelu(a_scr[rows_e, :])).astype(BF16)
        return carry

    lax.fori_loop(0, (eb // PEER_KEYS) * n_chunks, gate_body, 0)
    acc_scr[...] += _dot(vt_ref[...], w_scr[...])

    @pl.when(e == ne - 1)
    def _fin():
        y_ref[...] = h_ref[...] + acc_scr[...].T


def _peer(xn, h, wqt, k1, k2, u_tab, vt_tab):
    n = xn.shape[0]
    tt = min(TOKEN_TILE, n)
    eb = PEER_EXPERT_BLOCK
    full = lambda shape: pl.BlockSpec(shape, lambda i, e: (0,) * len(shape))
    tok = pl.BlockSpec((tt, D_MODEL), lambda i, e: (i, 0))
    head_scr = pltpu.VMEM((PEER_HEADS, PEER_KEYS, tt), F32)
    return pl.pallas_call(
        _peer_kernel,
        grid=(n // tt, PEER_EXPERTS // eb),
        in_specs=[tok, tok, full((2 * PEER_HEADS * PEER_HALF, D_MODEL)),
                  full((PEER_KEYS, PEER_HALF)), full((PEER_KEYS, PEER_HALF)),
                  pl.BlockSpec((eb, D_MODEL), lambda i, e: (e, 0)),
                  pl.BlockSpec((D_MODEL, eb), lambda i, e: (0, e))],
        out_specs=tok,
        out_shape=jax.ShapeDtypeStruct((n, D_MODEL), F32),
        scratch_shapes=[head_scr, head_scr, head_scr, head_scr,
                        pltpu.VMEM((D_MODEL, tt), F32), pltpu.VMEM((eb, tt), F32),
                        pltpu.VMEM((eb, tt), BF16), pltpu.VMEM((_CAND_ROWS, LANES), F32)],
        compiler_params=_cparams(("parallel", "arbitrary")),
        name="peer",
    )(xn, h, wqt, k1, k2, u_tab, vt_tab)


def _rope_tables(pos):
    half = HEAD_DIM // 2
    inv = ROPE_THETA ** (-jnp.arange(half, dtype=F32) / half)
    ang = pos.astype(F32)[:, None] * inv[None, :]
    cos = jnp.cos(ang)
    sin = jnp.sin(ang)
    cos = jnp.tile(jnp.concatenate([cos, cos], axis=-1), (1, SWA_Q_HEADS))
    sin = jnp.tile(jnp.concatenate([-sin, sin], axis=-1), (1, SWA_Q_HEADS))
    return cos, sin


def _ssm_params(log_dt, a_re, a_im, b_re, b_im, c_re, c_im):
    dt = jnp.exp(log_dt)
    mag = jnp.exp(a_re * dt)
    lam_re = mag * jnp.cos(a_im * dt)
    lam_im = mag * jnp.sin(a_im * dt)
    den = a_re * a_re + a_im * a_im
    z_re = ((lam_re - 1.0) * a_re + lam_im * a_im) / den
    z_im = (lam_im * a_re - (lam_re - 1.0) * a_im) / den
    bb_re = z_re[..., None] * b_re - z_im[..., None] * b_im
    bb_im = z_re[..., None] * b_im + z_im[..., None] * b_re
    hg = SSM_HALF_GROUPS
    eye = jnp.eye(hg, dtype=F32)
    bb = jnp.stack([bb_re, bb_im]).reshape(2, 2, hg, SSM_STATE, SSM_GROUP)
    bmat = jnp.einsum('rjgnc,gh->jgcrhn', bb, eye).reshape(2, hg * SSM_GROUP, 2 * SSM_HALF_STATE)
    cc = jnp.stack([c_re, -c_im]).reshape(2, 2, hg, SSM_GROUP, SSM_STATE)
    cmat = jnp.einsum('rjgcn,gh->jrgnhc', cc, eye).reshape(2, 2 * SSM_HALF_STATE, hg * SSM_GROUP)
    lam = jnp.stack([lam_re.reshape(2, SSM_HALF_STATE), lam_im.reshape(2, SSM_HALF_STATE)], axis=1)
    return lam.reshape(1, SSM_COLS), bmat.astype(BF16), cmat.astype(BF16)


def _state_to_cols(s_re, s_im):
    b = s_re.shape[0]
    st = jnp.stack([s_re.reshape(b, 2, SSM_HALF_STATE), s_im.reshape(b, 2, SSM_HALF_STATE)], axis=2)
    return st.reshape(b, SSM_COLS)


def _cols_to_state(cols):
    b = cols.shape[0]
    st = cols.reshape(b, 2, 2, SSM_HALF_STATE)
    return (st[:, :, 0].reshape(b, SSM_GROUPS, SSM_STATE), st[:, :, 1].reshape(b, SSM_GROUPS, SSM_STATE))


def kernel(x_prompt, x_sample, state_ssm_re, state_ssm_im, cache_win_k, cache_win_v, cache_mem_k, cache_mem_v, mem_prompt, norm1_g, w_in, ssm_log_dt, ssm_a_re, ssm_a_im, ssm_b_re, ssm_b_im, ssm_c_re, ssm_c_im, ssm_d, w_glu, b_glu, swa_q_norm, swa_k_norm, swa_sinks, mem_norm_g, w_mem_kv, mem_q_norm, mem_k_norm, w_out, norm2_g, peer_wq, peer_k1, peer_k2, peer_u, peer_v):
    depth = norm1_g.shape[0]
    assert depth == 1
    l = 0
    B, T, _ = x_prompt.shape
    SB, ST, _ = x_sample.shape
    w = cache_win_k.shape[2]

    row = lambda a: a.reshape(1, -1)
    g1 = row(norm1_g[l])
    g2 = row(norm2_g[l])
    win = w_in[l].astype(BF16)
    wo = w_out[l].astype(BF16)
    wglu = w_glu[l].astype(BF16)
    bglu = row(b_glu[l])
    gq = row(jnp.tile(swa_q_norm[l], SWA_Q_HEADS))
    gk = row(jnp.tile(swa_k_norm[l], SWA_KV_HEADS))
    gm = row(jnp.tile(mem_q_norm[l], MEM_HEADS))
    gmk = row(jnp.tile(mem_k_norm[l], MEM_HEADS))
    gmem = row(mem_norm_g[l])
    wkv = w_mem_kv[l].astype(BF16)
    sinks = swa_sinks[l]
    head_id = np.arange(SWA_WIDTH) // HEAD_DIM
    ones = jnp.asarray(head_id[:, None] == head_id[None, :], dtype=BF16)
    lam, bmat, cmat = _ssm_params(ssm_log_dt[l], ssm_a_re[l], ssm_a_im[l], ssm_b_re[l], ssm_b_im[l],
                                  ssm_c_re[l], ssm_c_im[l])
    dskip = row(ssm_d[l])
    wqt = peer_wq[l].T.astype(BF16)
    k1 = peer_k1[l].astype(BF16)
    k2 = peer_k2[l].astype(BF16)
    u_tab = peer_u[l].astype(BF16)
    vt_tab = peer_v[l].T.astype(BF16)

    cos_p, sin_p = _rope_tables(jnp.arange(T, dtype=jnp.int32))
    u_p, q_p, k_p, v_p, qm_p = _in_proj(x_prompt, cos_p, sin_p, g1, win, ones, gq, gk, gm)
    zeros = jnp.zeros((B, SSM_COLS), F32)
    ossm_p, sfin_p = _s5(u_p.reshape(T * B, SSM_WIDTH), zeros, lam, bmat, cmat, dskip, wglu, bglu,
                         bt=B, tt=S5_TIME_TILE)
    mk, mv = _mem_kv(mem_prompt.reshape(B * MEM_TOKENS, D_MODEL), gmem, wkv, ones, gmk)
    mk = mk.reshape(B, MEM_TOKENS, MEM_WIDTH)
    mv = mv.reshape(B, MEM_TOKENS, MEM_WIDTH)
    osw_p, omem_p = _attn_prompt(sinks, q_p, k_p, v_p, qm_p, mk, mv)
    h_p, xn_p = _out_proj(x_prompt, ossm_p.reshape(T, B * SSM_WIDTH), osw_p, omem_p, wo, g2)
    y_p = _peer(xn_p.reshape(B * T, D_MODEL), h_p.reshape(B * T, D_MODEL), wqt, k1, k2, u_tab, vt_tab)
    y_p = y_p.reshape(B, T, D_MODEL)
    p_sr, p_si = _cols_to_state(sfin_p)
    p_wk = k_p[:, T - w:].reshape(B, w, SWA_KV_HEADS, HEAD_DIM)
    p_wv = v_p[:, T - w:].reshape(B, w, SWA_KV_HEADS, HEAD_DIM)
    p_mk = mk.reshape(B, MEM_TOKENS, MEM_HEADS, HEAD_DIM)
    p_mv = mv.reshape(B, MEM_TOKENS, MEM_HEADS, HEAD_DIM)

    n_s = SB * ST
    pos_s = PAST_LEN + jnp.tile(jnp.arange(ST, dtype=jnp.int32), SB)
    cos_s, sin_s = _rope_tables(pos_s)
    xs = x_sample.reshape(1, n_s, D_MODEL)
    u_s, q_s, k_s, v_s, qm_s = _in_proj(xs, cos_s, sin_s, g1, win, ones, gq, gk, gm)
    u_tm = u_s.reshape(SB, ST, SSM_WIDTH).transpose(1, 0, 2).reshape(n_s, SSM_WIDTH)
    ossm_tm, sfin_s = _s5(u_tm, _state_to_cols(state_ssm_re[l], state_ssm_im[l]), lam, bmat, cmat,
                          dskip, wglu, bglu, bt=SB, tt=ST)
    ossm_s = ossm_tm.reshape(ST, SB, SSM_WIDTH).transpose(1, 0, 2).reshape(n_s, SSM_WIDTH)
    ck = cache_win_k[l].reshape(SB, w, SWA_KV_WIDTH)
    cv = cache_win_v[l].reshape(SB, w, SWA_KV_WIDTH)
    q_s2 = q_s.reshape(n_s, SWA_WIDTH)
    k_s2 = k_s.reshape(n_s, SWA_KV_WIDTH)
    v_s2 = v_s.reshape(n_s, SWA_KV_WIDTH)
    osw_s, omem_s = _attn_sample(sinks, q_s2, k_s2, v_s2, ck, cv, qm_s.reshape(n_s, MEM_WIDTH),
                                 cache_mem_k[l].reshape(SB, MEM_TOKENS, MEM_WIDTH),
                                 cache_mem_v[l].reshape(SB, MEM_TOKENS, MEM_WIDTH),
                                 ts=ST, start=PAST_LEN)
    h_s, xn_s = _out_proj(xs, ossm_s, osw_s.reshape(1, n_s, SWA_WIDTH), omem_s.reshape(1, n_s, MEM_WIDTH),
                          wo, g2)
    y_s = _peer(xn_s.reshape(n_s, D_MODEL), h_s.reshape(n_s, D_MODEL), wqt, k1, k2, u_tab, vt_tab)
    y_s = y_s.reshape(SB, ST, D_MODEL)
    s_sr, s_si = _cols_to_state(sfin_s)
    s_wk = jnp.concatenate([ck, k_s2.reshape(SB, ST, SWA_KV_WIDTH)], axis=1)[:, -w:]
    s_wv = jnp.concatenate([cv, v_s2.reshape(SB, ST, SWA_KV_WIDTH)], axis=1)[:, -w:]
    s_wk = s_wk.reshape(SB, w, SWA_KV_HEADS, HEAD_DIM)
    s_wv = s_wv.reshape(SB, w, SWA_KV_HEADS, HEAD_DIM)

    st = lambda a: a[None]
    return (y_p, y_s, st(p_sr), st(p_si), st(p_wk), st(p_wv), st(p_mk), st(p_mv),
            st(s_sr), st(s_si), st(s_wk), st(s_wv))
```

```python
import functools
import math

import jax
import jax.numpy as jnp
import numpy as np
from jax import lax
from jax.experimental import pallas as pl
from jax.experimental.pallas import tpu as pltpu

F32 = jnp.float32
BF16 = jnp.bfloat16

D_MODEL = 1024
HEAD_DIM = 64
EPS = 1e-6
ROPE_THETA = 10000.0
PAST_LEN = 8192
SSM_WIDTH = 512
SSM_GROUP = 16
SSM_GROUPS = 32
SSM_STATE = 64
SSM_HALF_GROUPS = SSM_GROUPS // 2
SSM_HALF_STATE = SSM_HALF_GROUPS * SSM_STATE
SSM_COLS = 2 * 2 * SSM_HALF_STATE
SWA_Q_HEADS = 4
SWA_KV_HEADS = 2
SWA_REP = 2
SWA_WIDTH = 256
SWA_KV_WIDTH = 128
WINDOW = 128
MEM_TOKENS = 256
MEM_HEADS = 4
MEM_WIDTH = 256
IN_WIDTH = 1280
PEER_HEADS = 8
PEER_KEYS = 128
PEER_EXPERTS = PEER_KEYS * PEER_KEYS
PEER_TOPK = 16
PEER_HALF = 128

LANES = 128
VMEM_LIMIT = 60 * 1024 * 1024

TOKEN_TILE = 512
PEER_EXPERT_BLOCK = 1024
PEER_GATE_KEYS = 4
S5_TIME_TILE = 64

_CAND_COUNT = [PEER_TOPK // (a + 1) for a in range(PEER_TOPK)]
_CAND_ROW0 = [0, 16, 24, 32, 36, 40, 42, 44, 48, 49, 50, 51, 52, 53, 54, 55]
_CAND_ROWS = 56


def _cparams(sem):
    return pltpu.CompilerParams(dimension_semantics=sem, vmem_limit_bytes=VMEM_LIMIT)


def _rms(x, g):
    return x * lax.rsqrt(jnp.mean(x * x, axis=-1, keepdims=True) + EPS) * g


def _gelu(x):
    return 0.5 * x * (1.0 + lax.erf(x * np.float32(math.sqrt(0.5))))


def _dot(a, b):
    return jnp.dot(a, b, preferred_element_type=F32)


def _dot_nt(a, b):
    return lax.dot_general(a, b, (((1,), (1,)), ((), ())), preferred_element_type=F32)


def _div(x, n):
    return x >> (n.bit_length() - 1) if n & (n - 1) == 0 else x // n


def _mod(x, n):
    return x & (n - 1) if n & (n - 1) == 0 else x % n


def _head_rms(x, ones_bd, g):
    sq = x * x
    hi = sq.astype(BF16)
    lo = (sq - hi.astype(F32)).astype(BF16)
    ms = (_dot(hi, ones_bd) + _dot(lo, ones_bd)) * np.float32(1.0 / HEAD_DIM)
    return x * lax.rsqrt(ms + EPS) * g


def _rope(x, cos, sin_signed):
    w = x.shape[-1]
    lane = lax.broadcasted_iota(jnp.int32, x.shape, 1)
    first_half = _mod(lane, HEAD_DIM) < (HEAD_DIM // 2)
    partner = jnp.where(first_half, pltpu.roll(x, w - HEAD_DIM // 2, 1),
                        pltpu.roll(x, HEAD_DIM // 2, 1))
    return x * cos + partner * sin_signed


def _in_proj_kernel(x_ref, g1_ref, win_ref, ones_ref, gq_ref, gk_ref, gm_ref, cos_ref, sin_ref,
                    u_ref, q_ref, k_ref, v_ref, qm_ref):
    x = x_ref[0]
    xn = _rms(x, g1_ref[...])
    proj = _dot(xn.astype(BF16), win_ref[...])
    u_ref[...] = proj[:, :SSM_WIDTH]
    q = proj[:, 512:768]
    k = proj[:, 768:896]
    v_ref[0] = proj[:, 896:1024]
    qm = proj[:, 1024:1280]
    ones = ones_ref[...]
    cos = cos_ref[...]
    sin = sin_ref[...]
    q_ref[0] = _rope(_head_rms(q, ones, gq_ref[...]), cos, sin)
    k_ref[0] = _rope(_head_rms(k, ones[:SWA_KV_WIDTH, :SWA_KV_WIDTH], gk_ref[...]),
                     cos[:, :SWA_KV_WIDTH], sin[:, :SWA_KV_WIDTH])
    qm_ref[0] = _head_rms(qm, ones, gm_ref[...])


def _in_proj(x, cos, sin, g1, win, ones, gq, gk, gm):
    B, T, _ = x.shape
    tt = min(TOKEN_TILE, T)
    grid = (B, T // tt)
    full = lambda shape: pl.BlockSpec(shape, lambda b, t: (0,) * len(shape))
    return pl.pallas_call(
        _in_proj_kernel,
        grid=grid,
        in_specs=[
            pl.BlockSpec((1, tt, D_MODEL), lambda b, t: (b, t, 0)),
            full((1, D_MODEL)), full((D_MODEL, IN_WIDTH)), full((SWA_WIDTH, SWA_WIDTH)),
            full((1, SWA_WIDTH)), full((1, SWA_KV_WIDTH)), full((1, MEM_WIDTH)),
            pl.BlockSpec((tt, SWA_WIDTH), lambda b, t: (t, 0)),
            pl.BlockSpec((tt, SWA_WIDTH), lambda b, t: (t, 0)),
        ],
        out_specs=[
            pl.BlockSpec((tt, SSM_WIDTH), lambda b, t: (t, b)),
            pl.BlockSpec((1, tt, SWA_WIDTH), lambda b, t: (b, t, 0)),
            pl.BlockSpec((1, tt, SWA_KV_WIDTH), lambda b, t: (b, t, 0)),
            pl.BlockSpec((1, tt, SWA_KV_WIDTH), lambda b, t: (b, t, 0)),
            pl.BlockSpec((1, tt, MEM_WIDTH), lambda b, t: (b, t, 0)),
        ],
        out_shape=[
            jax.ShapeDtypeStruct((T, B * SSM_WIDTH), F32),
            jax.ShapeDtypeStruct((B, T, SWA_WIDTH), F32),
            jax.ShapeDtypeStruct((B, T, SWA_KV_WIDTH), F32),
            jax.ShapeDtypeStruct((B, T, SWA_KV_WIDTH), F32),
            jax.ShapeDtypeStruct((B, T, MEM_WIDTH), F32),
        ],
        compiler_params=_cparams(("parallel", "parallel")),
        name="in_proj",
    )(x, g1, win, ones, gq, gk, gm, cos, sin)


def _s5_kernel(u_ref, s0_ref, lam_ref, bmat_ref, cmat_ref, d_ref, wglu_ref, bglu_ref,
               o_ref, sfin_ref, s_scr, carry_scr, *, bt, tt):
    @pl.when(pl.program_id(0) == 0)
    def _():
        carry_scr[...] = s0_ref[...]

    u = u_ref[...]
    ub = u.astype(BF16)
    hw = 2 * SSM_HALF_STATE
    for j in range(2):
        s_scr[:, j * hw:(j + 1) * hw] = _dot(ub[:, j * 256:(j + 1) * 256], bmat_ref[j])

    def step(t, carry):
        r0 = pl.multiple_of(t * bt, bt)
        for j in range(2):
            c_re = pl.ds(j * hw, SSM_HALF_STATE)
            c_im = pl.ds(j * hw + SSM_HALF_STATE, SSM_HALF_STATE)
            p_re = carry_scr[:, c_re]
            p_im = carry_scr[:, c_im]
            l_re = lam_ref[:, c_re]
            l_im = lam_ref[:, c_im]
            n_re = l_re * p_re - l_im * p_im + s_scr[pl.ds(r0, bt), c_re]
            n_im = l_re * p_im + l_im * p_re + s_scr[pl.ds(r0, bt), c_im]
            s_scr[pl.ds(r0, bt), c_re] = n_re
            s_scr[pl.ds(r0, bt), c_im] = n_im
            carry_scr[:, c_re] = n_re
            carry_scr[:, c_im] = n_im
        return carry

    lax.fori_loop(0, tt, step, 0)
    sfin_ref[...] = carry_scr[...]

    ys = [_dot(s_scr[:, j * hw:(j + 1) * hw].astype(BF16), cmat_ref[j]) for j in range(2)]
    y = jnp.concatenate(ys, axis=-1) + d_ref[...] * u
    y = _gelu(y)
    z = _dot(y.astype(BF16), wglu_ref[...]) + bglu_ref[...]
    o_ref[...] = y * jax.nn.sigmoid(z)


def _s5(u_tm, s0, lam, bmat, cmat, d, wglu, bglu, *, bt, tt):
    rows = u_tm.shape[0]
    nt = rows // (bt * tt)
    full = lambda shape: pl.BlockSpec(shape, lambda t: (0,) * len(shape))
    return pl.pallas_call(
        functools.partial(_s5_kernel, bt=bt, tt=tt),
        grid=(nt,),
        in_specs=[
            pl.BlockSpec((bt * tt, SSM_WIDTH), lambda t: (t, 0)),
            full((bt, SSM_COLS)), full((1, SSM_COLS)),
            full((2, 256, 2 * SSM_HALF_STATE)), full((2, 2 * SSM_HALF_STATE, 256)),
            full((1, SSM_WIDTH)), full((SSM_WIDTH, SSM_WIDTH)), full((1, SSM_WIDTH)),
        ],
        out_specs=[
            pl.BlockSpec((bt * tt, SSM_WIDTH), lambda t: (t, 0)),
            full((bt, SSM_COLS)),
        ],
        out_shape=[
            jax.ShapeDtypeStruct((rows, SSM_WIDTH), F32),
            jax.ShapeDtypeStruct((bt, SSM_COLS), F32),
        ],
        scratch_shapes=[pltpu.VMEM((bt * tt, SSM_COLS), F32), pltpu.VMEM((bt, SSM_COLS), F32)],
        compiler_params=_cparams(("arbitrary",)),
        name="s5_mixer",
    )(u_tm, s0, lam, bmat, cmat, d, wglu, bglu)


def _mem_kv_kernel(m_ref, g_ref, w_ref, ones_ref, gk_ref, k_ref, v_ref):
    xn = _rms(m_ref[...], g_ref[...])
    kv = _dot(xn.astype(BF16), w_ref[...])
    k_ref[...] = _head_rms(kv[:, :MEM_WIDTH], ones_ref[...], gk_ref[...])
    v_ref[...] = kv[:, MEM_WIDTH:]


def _mem_kv(mem_rows, g, w, ones, gk):
    rows = mem_rows.shape[0]
    tt = min(TOKEN_TILE, rows)
    full = lambda shape: pl.BlockSpec(shape, lambda t: (0,) * len(shape))
    return pl.pallas_call(
        _mem_kv_kernel,
        grid=(rows // tt,),
        in_specs=[pl.BlockSpec((tt, D_MODEL), lambda t: (t, 0)), full((1, D_MODEL)),
                  full((D_MODEL, 2 * MEM_WIDTH)), full((MEM_WIDTH, MEM_WIDTH)), full((1, MEM_WIDTH))],
        out_specs=[pl.BlockSpec((tt, MEM_WIDTH), lambda t: (t, 0))] * 2,
        out_shape=[jax.ShapeDtypeStruct((rows, MEM_WIDTH), F32)] * 2,
        compiler_params=_cparams(("parallel",)),
        name="mem_kv",
    )(mem_rows, g, w, ones, gk)


def _softmax_pv(s, v_b, sink=None):
    m = jnp.max(s, axis=-1, keepdims=True)
    if sink is not None:
        m = jnp.maximum(m, sink)
    p = jnp.exp(s - m)
    den = jnp.sum(p, axis=-1, keepdims=True)
    if sink is not None:
        den = den + jnp.exp(sink - m)
    return _dot((p / den).astype(BF16), v_b)


def _attn_prompt_kernel(sinks_ref, q_ref, kp_ref, kc_ref, vp_ref, vc_ref, qm_ref, mk_ref, mv_ref,
                        osw_ref, omem_ref):
    nb = pl.program_id(1)
    scale = np.float32(HEAD_DIM ** -0.5)
    q = q_ref[0].astype(BF16)
    kk = jnp.concatenate([kp_ref[0], kc_ref[0]], axis=0).astype(BF16)
    vv = jnp.concatenate([vp_ref[0], vc_ref[0]], axis=0).astype(BF16)
    qi = lax.broadcasted_iota(jnp.int32, (WINDOW, 2 * WINDOW), 0)
    ki = lax.broadcasted_iota(jnp.int32, (WINDOW, 2 * WINDOW), 1) - WINDOW
    diff = qi - ki
    valid = (diff >= 0) & (diff < WINDOW) & (nb * WINDOW + ki >= 0)
    outs = []
    for hq in range(SWA_Q_HEADS):
        kv = hq // SWA_REP
        s = _dot_nt(q[:, hq * HEAD_DIM:(hq + 1) * HEAD_DIM],
                    kk[:, kv * HEAD_DIM:(kv + 1) * HEAD_DIM]) * scale
        s = jnp.where(valid, s, -jnp.inf)
        outs.append(_softmax_pv(s, vv[:, kv * HEAD_DIM:(kv + 1) * HEAD_DIM], sinks_ref[hq]))
    osw_ref[0] = jnp.concatenate(outs, axis=-1)

    qm = qm_ref[0].astype(BF16)
    mk = mk_ref[0].astype(BF16)
    mv = mv_ref[0].astype(BF16)
    outs = []
    for h in range(MEM_HEADS):
        sl = slice(h * HEAD_DIM, (h + 1) * HEAD_DIM)
        s = _dot_nt(qm[:, sl], mk[:, sl]) * scale
        outs.append(_softmax_pv(s, mv[:, sl]))
    omem_ref[0] = jnp.concatenate(outs, axis=-1)


def _attn_prompt(sinks, q, k, v, qm, mk, mv):
    B, T, _ = q.shape
    nb = T // WINDOW
    blk = lambda w: pl.BlockSpec((1, WINDOW, w), lambda b, n: (b, n, 0))
    prev = lambda w: pl.BlockSpec((1, WINDOW, w), lambda b, n: (b, jnp.maximum(n - 1, 0), 0))
    memb = pl.BlockSpec((1, MEM_TOKENS, MEM_WIDTH), lambda b, n: (b, 0, 0))
    return pl.pallas_call(
        _attn_prompt_kernel,
        grid=(B, nb),
        in_specs=[pl.BlockSpec(memory_space=pltpu.SMEM),
                  blk(SWA_WIDTH), prev(SWA_KV_WIDTH), blk(SWA_KV_WIDTH), prev(SWA_KV_WIDTH),
                  blk(SWA_KV_WIDTH), blk(MEM_WIDTH), memb, memb],
        out_specs=[blk(SWA_WIDTH), blk(MEM_WIDTH)],
        out_shape=[jax.ShapeDtypeStruct((B, T, SWA_WIDTH), F32),
                   jax.ShapeDtypeStruct((B, T, MEM_WIDTH), F32)],
        compiler_params=_cparams(("parallel", "parallel")),
        name="attn_prompt",
    )(sinks, q, k, k, v, v, qm, mk, mv)


def _attn_sample_kernel(sinks_ref, q_ref, kn_ref, vn_ref, ck_ref, cv_ref, qm_ref, mk_ref, mv_ref,
                        osw_ref, omem_ref, *, bb, ts, start):
    scale = np.float32(HEAD_DIM ** -0.5)
    w = ck_ref.shape[1]
    nq = bb * ts
    q = q_ref[...].astype(BF16)
    kn = kn_ref[...].astype(BF16)
    vn = vn_ref[...].astype(BF16)
    ck = ck_ref[...].reshape(bb * w, SWA_KV_WIDTH).astype(BF16)
    cv = cv_ref[...].reshape(bb * w, SWA_KV_WIDTH).astype(BF16)

    rq = lax.broadcasted_iota(jnp.int32, (nq, bb * w), 0)
    cc = lax.broadcasted_iota(jnp.int32, (nq, bb * w), 1)
    qpos = start + _mod(rq, ts)
    kpos = start - w + _mod(cc, w)
    diff = qpos - kpos
    valid_c = (_div(rq, ts) == _div(cc, w)) & (diff >= 0) & (diff < WINDOW) & (kpos >= 0)
    rq = lax.broadcasted_iota(jnp.int32, (nq, nq), 0)
    cn = lax.broadcasted_iota(jnp.int32, (nq, nq), 1)
    diff = _mod(rq, ts) - _mod(cn, ts)
    valid_n = (_div(rq, ts) == _div(cn, ts)) & (diff >= 0) & (diff < WINDOW)

    outs = []
    for hq in range(SWA_Q_HEADS):
        kv = hq // SWA_REP
        qs = q[:, hq * HEAD_DIM:(hq + 1) * HEAD_DIM]
        ks = slice(kv * HEAD_DIM, (kv + 1) * HEAD_DIM)
        s_c = jnp.where(valid_c, _dot_nt(qs, ck[:, ks]) * scale, -jnp.inf)
        s_n = jnp.where(valid_n, _dot_nt(qs, kn[:, ks]) * scale, -jnp.inf)
        sink = sinks_ref[hq]
        m = jnp.maximum(jnp.maximum(jnp.max(s_c, axis=-1, keepdims=True),
                                    jnp.max(s_n, axis=-1, keepdims=True)), sink)
        p_c = jnp.exp(s_c - m)
        p_n = jnp.exp(s_n - m)
        den = (jnp.sum(p_c, axis=-1, keepdims=True) + jnp.sum(p_n, axis=-1, keepdims=True)
               + jnp.exp(sink - m))
        outs.append(_dot((p_c / den).astype(BF16), cv[:, ks]) + _dot((p_n / den).astype(BF16), vn[:, ks]))
    osw_ref[...] = jnp.concatenate(outs, axis=-1)

    qm = qm_ref[...].astype(BF16)
    mk = mk_ref[...].reshape(bb * MEM_TOKENS, MEM_WIDTH).astype(BF16)
    mv = mv_ref[...].reshape(bb * MEM_TOKENS, MEM_WIDTH).astype(BF16)
    rq = lax.broadcasted_iota(jnp.int32, (nq, bb * MEM_TOKENS), 0)
    cm = lax.broadcasted_iota(jnp.int32, (nq, bb * MEM_TOKENS), 1)
    valid_m = _div(rq, ts) == _div(cm, MEM_TOKENS)
    outs = []
    for h in range(MEM_HEADS):
        sl = slice(h * HEAD_DIM, (h + 1) * HEAD_DIM)
        s = jnp.where(valid_m, _dot_nt(qm[:, sl], mk[:, sl]) * scale, -jnp.inf)
        outs.append(_softmax_pv(s, mv[:, sl]))
    omem_ref[...] = jnp.concatenate(outs, axis=-1)


def _attn_sample(sinks, q, kn, vn, ck, cv, qm, cmk, cmv, *, ts, start):
    B, w, _ = ck.shape
    bb = 8
    rows = lambda wd: pl.BlockSpec((bb * ts, wd), lambda i: (i, 0))
    blk3 = lambda n, wd: pl.BlockSpec((bb, n, wd), lambda i: (i, 0, 0))
    return pl.pallas_call(
        functools.partial(_attn_sample_kernel, bb=bb, ts=ts, start=start),
        grid=(B // bb,),
        in_specs=[pl.BlockSpec(memory_space=pltpu.SMEM),
                  rows(SWA_WIDTH), rows(SWA_KV_WIDTH), rows(SWA_KV_WIDTH),
                  blk3(w, SWA_KV_WIDTH), blk3(w, SWA_KV_WIDTH), rows(MEM_WIDTH),
                  blk3(MEM_TOKENS, MEM_WIDTH), blk3(MEM_TOKENS, MEM_WIDTH)],
        out_specs=[rows(SWA_WIDTH), rows(MEM_WIDTH)],
        out_shape=[jax.ShapeDtypeStruct((B * ts, SWA_WIDTH), F32),
                   jax.ShapeDtypeStruct((B * ts, MEM_WIDTH), F32)],
        compiler_params=_cparams(("parallel",)),
        name="attn_sample",
    )(sinks, q, kn, vn, ck, cv, qm, cmk, cmv)


def _out_proj_kernel(x_ref, ossm_ref, osw_ref, omem_ref, wo_ref, g2_ref, h_ref, xn_ref):
    h = x_ref[0]
    h = h + _dot(ossm_ref[...].astype(BF16), wo_ref[0:512, :])
    h = h + _dot(osw_ref[0].astype(BF16), wo_ref[512:768, :])
    h = h + _dot(omem_ref[0].astype(BF16), wo_ref[768:1024, :])
    h_ref[0] = h
    xn_ref[0] = _rms(h, g2_ref[...]).astype(BF16)


def _out_proj(x, ossm, osw, omem, wo, g2):
    B, T, _ = x.shape
    tt = min(TOKEN_TILE, T)
    full = lambda shape: pl.BlockSpec(shape, lambda b, t: (0,) * len(shape))
    blk = lambda wd: pl.BlockSpec((1, tt, wd), lambda b, t: (b, t, 0))
    return pl.pallas_call(
        _out_proj_kernel,
        grid=(B, T // tt),
        in_specs=[blk(D_MODEL), pl.BlockSpec((tt, SSM_WIDTH), lambda b, t: (t, b)),
                  blk(SWA_WIDTH), blk(MEM_WIDTH), full((D_MODEL, D_MODEL)), full((1, D_MODEL))],
        out_specs=[blk(D_MODEL), blk(D_MODEL)],
        out_shape=[jax.ShapeDtypeStruct((B, T, D_MODEL), F32),
                   jax.ShapeDtypeStruct((B, T, D_MODEL), BF16)],
        compiler_params=_cparams(("parallel", "parallel")),
        name="out_proj",
    )(x, ossm, osw, omem, wo, g2)


def _top16(s, iota):
    work = s
    rank = jnp.full(s.shape, float(PEER_TOPK), F32)
    vals = []
    for a in range(PEER_TOPK):
        m = jnp.max(work, axis=0, keepdims=True)
        idx = jnp.min(jnp.where(work == m, iota, float(PEER_KEYS)), axis=0, keepdims=True)
        sel = iota == idx
        rank = jnp.where(sel, float(a), rank)
        work = jnp.where(sel, -jnp.inf, work)
        vals.append(m)
    return vals, rank


def _sort_pairs(n):
    pairs = []
    t = max(1, (n - 1).bit_length())
    p = 1 << (t - 1)
    while p > 0:
        q, r, d = 1 << (t - 1), 0, p
        while d > 0:
            pairs += [(i, i + d) for i in range(n - d) if (i & p) == r]
            d, q, r = q - p, q >> 1, p
        p >>= 1
    return pairs


_SORT16_PAIRS = _sort_pairs(PEER_TOPK)


def _cmp_exchange(xs, i, j):
    xs[i], xs[j] = jnp.maximum(xs[i], xs[j]), jnp.minimum(xs[i], xs[j])


def _top16_sorted(s):
    n = PEER_TOPK
    xs = [s[8 * k:8 * k + 8, :] for k in range(n)]
    for i, j in _SORT16_PAIRS:
        _cmp_exchange(xs, i, j)
    for shift in (4, 2, 1):
        other = [pltpu.roll(x, shift, 0) for x in xs]
        xs = [jnp.maximum(xs[k], other[n - 1 - k]) for k in range(n)]
        for d in (8, 4, 2, 1):
            for k in range(n):
                if k & d == 0:
                    _cmp_exchange(xs, k, k + d)
    return xs


def _joint_counts(v1, v2, cand_scr):
    L = v1[0].shape[1]
    cand_scr[...] = jnp.full((_CAND_ROWS, L), -jnp.inf, F32)
    for a in range(PEER_TOPK):
        for b in range(_CAND_COUNT[a]):
            cand_scr[pl.ds(_CAND_ROW0[a] + b, 1), :] = v1[a] + v2[b]
    work = cand_scr[...]
    iota_c = lax.broadcasted_iota(jnp.int32, (_CAND_ROWS, L), 0).astype(F32)
    taken = jnp.zeros((_CAND_ROWS, L), F32)
    m0 = None
    z = None
    for k in range(PEER_TOPK):
        m = jnp.max(work, axis=0, keepdims=True)
        idx = jnp.min(jnp.where(work == m, iota_c, float(_CAND_ROWS)), axis=0, keepdims=True)
        sel = iota_c == idx
        taken = jnp.where(sel, 1.0, taken)
        work = jnp.where(sel, -jnp.inf, work)
        if k == 0:
            m0 = m
            z = jnp.ones_like(m)
        else:
            z = z + jnp.exp(m - m0)

    cnt = []
    for a in range(PEER_TOPK):
        lo, hi = float(_CAND_ROW0[a]), float(_CAND_ROW0[a] + _CAND_COUNT[a])
        cnt.append(jnp.sum(jnp.where((iota_c >= lo) & (iota_c < hi), taken, 0.0), axis=0, keepdims=True))
    return cnt, z


def _route_tile(s1, s2, cand_scr, n1_out, r2_out):
    n = PEER_TOPK
    t1 = _top16_sorted(s1)
    t2 = _top16_sorted(s2)
    v1 = [t[0:1, :] for t in t1]
    v2 = [t[0:1, :] for t in t2]
    cnt, z = _joint_counts(v1, v2, cand_scr)

    bad = jnp.zeros((8, s1.shape[1]), F32)
    for s, t in ((s1, t1), (s2, t2)):
        gap = t[0] - t[1]
        for a in range(1, n - 1):
            gap = jnp.minimum(gap, t[a] - t[a + 1])
        at_least = jnp.zeros((8, s.shape[1]), F32)
        for k in range(n):
            at_least = at_least + jnp.where(s[8 * k:8 * k + 8, :] >= t[n - 1], 1.0, 0.0)
        at_least = jnp.sum(at_least, axis=0, keepdims=True)
        bad = jnp.where((gap <= 0.0) | (at_least != float(n)), 1.0, bad)
    has_ties = jnp.max(bad) > 0.0

    @pl.when(jnp.logical_not(has_ties))
    def _by_value():
        cnt_b = [jnp.broadcast_to(c, (8, c.shape[1])) for c in cnt]
        ranks = []
        for k in range(n):
            x1 = s1[8 * k:8 * k + 8, :]
            x2 = s2[8 * k:8 * k + 8, :]
            n1 = cnt_b[0]
            r2 = jnp.zeros_like(x2)
            for a in range(n):
                n1 = jnp.where(t1[a] > x1, cnt_b[a + 1] if a + 1 < n else 0.0, n1)
                r2 = jnp.where(t2[a] > x2, float(a + 1), r2)
            n1_out[8 * k:8 * k + 8, :] = n1
            ranks.append(r2)
        r2_out[...] = jnp.concatenate(ranks, axis=0)

    @pl.when(has_ties)
    def _by_index():
        iota = lax.broadcasted_iota(jnp.int32, s1.shape, 0).astype(F32)
        _, r1 = _top16(s1, iota)
        _, r2 = _top16(s2, iota)
        n1 = jnp.zeros(s1.shape, F32)
        for a in range(n):
            n1 = jnp.where(r1 == float(a), cnt[a], n1)
        n1_out[...] = n1
        r2_out[...] = r2

    c1 = jnp.exp(s1 - v1[0]) / z
    e2 = jnp.exp(s2 - v2[0])
    return c1, e2


def _peer_kernel(xn_ref, h_ref, wqt_ref, k1_ref, k2_ref, u0_ref, ua_ref, ub_ref, vta_ref, vtb_ref, y_ref,
                 n1_scr, c1_scr, s2_scr, r2_scr, e2_scr, acc_scr, a0_scr, a1_scr,
                 w0_scr, w1_scr, w2_scr, w3_scr, q_scr, xt_scr, e2f_scr, cand_scr, *, n_steps):
    g = pl.program_id(1)
    ng = n_steps
    tt = xn_ref.shape[0]
    eb = ua_ref.shape[0]
    keys_per_block = eb // PEER_KEYS
    assert keys_per_block % 8 == 0

    @pl.when(g == 0)
    def _route():
        xt_scr[...] = xn_ref[...].T
        xt = xt_scr[...]
        q_scr[...] = _dot(wqt_ref[...], xt).astype(BF16)
        for h in range(PEER_HEADS):
            for side, dst in enumerate((n1_scr, s2_scr)):
                k_ref = (k1_ref, k2_ref)[side]
                r0 = h * 2 * PEER_HALF + side * PEER_HALF
                dst[h] = _dot(k_ref[...], q_scr[r0:r0 + PEER_HALF, :])

        def body(i, carry):
            h = i // (tt // LANES)
            lanes = pl.ds(pl.multiple_of((i % (tt // LANES)) * LANES, LANES), LANES)
            c1, e2 = _route_tile(n1_scr[h, :, lanes], s2_scr[h, :, lanes], cand_scr,
                                 n1_scr.at[h, :, lanes], s2_scr.at[h, :, lanes])
            c1_scr[h, :, lanes] = c1
            e2f_scr[h, :, lanes] = e2
            return carry

        lax.fori_loop(0, PEER_HEADS * (tt // LANES), body, 0)
        for h in range(PEER_HEADS):
            r2_scr[h] = s2_scr[h].astype(BF16)
            e2_scr[h] = e2f_scr[h].astype(BF16)
        acc_scr[...] = jnp.zeros_like(acc_scr)
        w2_scr[...] = jnp.zeros_like(w2_scr)
        w3_scr[...] = jnp.zeros_like(w3_scr)
        a0_scr[...] = _dot(u0_ref[...], xt)

    def gate(a_scr, w_scr, block):
        zero = jnp.zeros((), BF16)
        group = PEER_GATE_KEYS
        for kt in range(keys_per_block // 8):
            tile0 = pl.multiple_of(block * keys_per_block + kt * 8, 8)
            for k0 in range(0, 8, group):
                for c in range(tt // LANES):
                    lanes = slice(c * LANES, (c + 1) * LANES)
                    gts = [jnp.zeros((PEER_KEYS // 16, 16, LANES), BF16) for _ in range(group)]
                    for h in range(PEER_HEADS):
                        r2 = r2_scr[h, :, lanes].reshape(PEER_KEYS // 16, 16, LANES)
                        e2 = e2_scr[h, :, lanes].reshape(PEER_KEYS // 16, 16, LANES)
                        n_tile = n1_scr[h, pl.ds(tile0, 8), lanes]
                        c_tile = c1_scr[h, pl.ds(tile0, 8), lanes]
                        for ii in range(group):
                            row = slice(k0 + ii, k0 + ii + 1)
                            n_b = jnp.broadcast_to(n_tile[row, :], (16, LANES)).astype(BF16)[None]
                            c_b = jnp.broadcast_to(c_tile[row, :], (16, LANES)).astype(BF16)[None]
                            gts[ii] = gts[ii] + jnp.where(r2 < n_b, e2, zero) * c_b
                    for ii in range(group):
                        r0 = (kt * 8 + k0 + ii) * PEER_KEYS
                        rows = slice(r0, r0 + PEER_KEYS)
                        act = _gelu(a_scr[rows, lanes]).astype(BF16)
                        w_scr[rows, lanes] = gts[ii].reshape(PEER_KEYS, LANES) * act

    def step(w_prev, w_cur):
        a1_scr[...] = _dot(ua_ref[...], xt_scr[...])
        acc_scr[...] += (_dot(vta_ref[...], w_prev[0][...])
                         + _dot(vtb_ref[...], w_prev[1][...]))
        gate(a0_scr, w_cur[0], 2 * g)
        a0_scr[...] = _dot(ub_ref[...], xt_scr[...])
        gate(a1_scr, w_cur[1], 2 * g + 1)

    w_sets = ((w0_scr, w1_scr), (w2_scr, w3_scr))

    @pl.when((g < ng) & (g % 2 == 0))
    def _even_step():
        step(w_sets[1], w_sets[0])

    @pl.when((g < ng) & (g % 2 == 1))
    def _odd_step():
        step(w_sets[0], w_sets[1])

    @pl.when(g == ng)
    def _fin():
        w_last = w_sets[(n_steps - 1) % 2]
        acc = acc_scr[...] + _dot(vta_ref[...], w_last[0][...]) + _dot(vtb_ref[...], w_last[1][...])
        y_ref[...] = h_ref[...] + acc.T


def _peer(xn, h, wqt, k1, k2, u_tab, vt_tab):
    n = xn.shape[0]
    tt = min(TOKEN_TILE, n)
    eb = PEER_EXPERT_BLOCK
    ne = PEER_EXPERTS // eb
    ng = ne // 2
    full = lambda shape: pl.BlockSpec(shape, lambda i, g: (0,) * len(shape), pipeline_mode=pl.Buffered(1))
    tok = pl.BlockSpec((tt, D_MODEL), lambda i, g: (i, 0))
    u_blk = lambda f: pl.BlockSpec((eb, D_MODEL), lambda i, g: (jnp.clip(f(g), 0, ne - 1), 0))
    vt_blk = lambda f: pl.BlockSpec((D_MODEL, eb), lambda i, g: (0, jnp.clip(f(g), 0, ne - 1)))
    head_f32 = pltpu.VMEM((PEER_HEADS, PEER_KEYS, tt), F32)
    head_bf16 = pltpu.VMEM((PEER_HEADS, PEER_KEYS, tt), BF16)
    w_buf = pltpu.VMEM((eb, tt), BF16)
    return pl.pallas_call(
        functools.partial(_peer_kernel, n_steps=ng),
        grid=(n // tt, ng + 1),
        in_specs=[tok, tok, full((2 * PEER_HEADS * PEER_HALF, D_MODEL)),
                  full((PEER_KEYS, PEER_HALF)), full((PEER_KEYS, PEER_HALF)),
                  full((eb, D_MODEL)), u_blk(lambda g: 2 * g + 1), u_blk(lambda g: 2 * g + 2),
                  vt_blk(lambda g: 2 * g - 2), vt_blk(lambda g: 2 * g - 1)],
        out_specs=tok,
        out_shape=jax.ShapeDtypeStruct((n, D_MODEL), F32),
        scratch_shapes=[head_f32, head_f32, head_f32, head_bf16, head_bf16,
                        pltpu.VMEM((D_MODEL, tt), F32),
                        pltpu.VMEM((eb, tt), F32), pltpu.VMEM((eb, tt), F32),
                        w_buf, w_buf, w_buf, w_buf,
                        pltpu.VMEM((2 * PEER_HEADS * PEER_HALF, tt), BF16),
                        pltpu.VMEM((D_MODEL, tt), BF16), head_f32,
                        pltpu.VMEM((_CAND_ROWS, LANES), F32)],
        compiler_params=_cparams(("parallel", "arbitrary")),
        name="peer",
    )(xn, h, wqt, k1, k2, u_tab, u_tab, u_tab, vt_tab, vt_tab)


def _rope_tables(pos):
    half = HEAD_DIM // 2
    inv = ROPE_THETA ** (-jnp.arange(half, dtype=F32) / half)
    ang = pos.astype(F32)[:, None] * inv[None, :]
    cos = jnp.cos(ang)
    sin = jnp.sin(ang)
    cos = jnp.tile(jnp.concatenate([cos, cos], axis=-1), (1, SWA_Q_HEADS))
    sin = jnp.tile(jnp.concatenate([-sin, sin], axis=-1), (1, SWA_Q_HEADS))
    return cos, sin


def _ssm_params(log_dt, a_re, a_im, b_re, b_im, c_re, c_im):
    dt = jnp.exp(log_dt)
    mag = jnp.exp(a_re * dt)
    lam_re = mag * jnp.cos(a_im * dt)
    lam_im = mag * jnp.sin(a_im * dt)
    den = a_re * a_re + a_im * a_im
    z_re = ((lam_re - 1.0) * a_re + lam_im * a_im) / den
    z_im = (lam_im * a_re - (lam_re - 1.0) * a_im) / den
    bb_re = z_re[..., None] * b_re - z_im[..., None] * b_im
    bb_im = z_re[..., None] * b_im + z_im[..., None] * b_re
    hg = SSM_HALF_GROUPS
    eye = jnp.eye(hg, dtype=F32)
    bb = jnp.stack([bb_re, bb_im]).reshape(2, 2, hg, SSM_STATE, SSM_GROUP)
    bmat = jnp.einsum('rjgnc,gh->jgcrhn', bb, eye).reshape(2, hg * SSM_GROUP, 2 * SSM_HALF_STATE)
    cc = jnp.stack([c_re, -c_im]).reshape(2, 2, hg, SSM_GROUP, SSM_STATE)
    cmat = jnp.einsum('rjgcn,gh->jrgnhc', cc, eye).reshape(2, 2 * SSM_HALF_STATE, hg * SSM_GROUP)
    lam = jnp.stack([lam_re.reshape(2, SSM_HALF_STATE), lam_im.reshape(2, SSM_HALF_STATE)], axis=1)
    return lam.reshape(1, SSM_COLS), bmat.astype(BF16), cmat.astype(BF16)


def _state_to_cols(s_re, s_im):
    b = s_re.shape[0]
    st = jnp.stack([s_re.reshape(b, 2, SSM_HALF_STATE), s_im.reshape(b, 2, SSM_HALF_STATE)], axis=2)
    return st.reshape(b, SSM_COLS)


def _cols_to_state(cols):
    b = cols.shape[0]
    st = cols.reshape(b, 2, 2, SSM_HALF_STATE)
    return (st[:, :, 0].reshape(b, SSM_GROUPS, SSM_STATE), st[:, :, 1].reshape(b, SSM_GROUPS, SSM_STATE))


def kernel(x_prompt, x_sample, state_ssm_re, state_ssm_im, cache_win_k, cache_win_v, cache_mem_k, cache_mem_v, mem_prompt, norm1_g, w_in, ssm_log_dt, ssm_a_re, ssm_a_im, ssm_b_re, ssm_b_im, ssm_c_re, ssm_c_im, ssm_d, w_glu, b_glu, swa_q_norm, swa_k_norm, swa_sinks, mem_norm_g, w_mem_kv, mem_q_norm, mem_k_norm, w_out, norm2_g, peer_wq, peer_k1, peer_k2, peer_u, peer_v):
    depth = norm1_g.shape[0]
    assert depth == 1
    l = 0
    B, T, _ = x_prompt.shape
    SB, ST, _ = x_sample.shape
    w = cache_win_k.shape[2]

    row = lambda a: a.reshape(1, -1)
    g1 = row(norm1_g[l])
    g2 = row(norm2_g[l])
    win = w_in[l].astype(BF16)
    wo = w_out[l].astype(BF16)
    wglu = w_glu[l].astype(BF16)
    bglu = row(b_glu[l])
    gq = row(jnp.tile(swa_q_norm[l], SWA_Q_HEADS))
    gk = row(jnp.tile(swa_k_norm[l], SWA_KV_HEADS))
    gm = row(jnp.tile(mem_q_norm[l], MEM_HEADS))
    gmk = row(jnp.tile(mem_k_norm[l], MEM_HEADS))
    gmem = row(mem_norm_g[l])
    wkv = w_mem_kv[l].astype(BF16)
    sinks = swa_sinks[l]
    head_id = np.arange(SWA_WIDTH) // HEAD_DIM
    ones = jnp.asarray(head_id[:, None] == head_id[None, :], dtype=BF16)
    lam, bmat, cmat = _ssm_params(ssm_log_dt[l], ssm_a_re[l], ssm_a_im[l], ssm_b_re[l], ssm_b_im[l],
                                  ssm_c_re[l], ssm_c_im[l])
    dskip = row(ssm_d[l])
    wqt = peer_wq[l].T.astype(BF16)
    k1 = peer_k1[l].astype(BF16)
    k2 = peer_k2[l].astype(BF16)
    u_tab = peer_u[l].astype(BF16)
    vt_tab = peer_v[l].T.astype(BF16)

    cos_p, sin_p = _rope_tables(jnp.arange(T, dtype=jnp.int32))
    u_p, q_p, k_p, v_p, qm_p = _in_proj(x_prompt, cos_p, sin_p, g1, win, ones, gq, gk, gm)
    zeros = jnp.zeros((B, SSM_COLS), F32)
    ossm_p, sfin_p = _s5(u_p.reshape(T * B, SSM_WIDTH), zeros, lam, bmat, cmat, dskip, wglu, bglu,
                         bt=B, tt=S5_TIME_TILE)
    mk, mv = _mem_kv(mem_prompt.reshape(B * MEM_TOKENS, D_MODEL), gmem, wkv, ones, gmk)
    mk = mk.reshape(B, MEM_TOKENS, MEM_WIDTH)
    mv = mv.reshape(B, MEM_TOKENS, MEM_WIDTH)
    osw_p, omem_p = _attn_prompt(sinks, q_p, k_p, v_p, qm_p, mk, mv)
    h_p, xn_p = _out_proj(x_prompt, ossm_p.reshape(T, B * SSM_WIDTH), osw_p, omem_p, wo, g2)
    y_p = _peer(xn_p.reshape(B * T, D_MODEL), h_p.reshape(B * T, D_MODEL), wqt, k1, k2, u_tab, vt_tab)
    y_p = y_p.reshape(B, T, D_MODEL)
    p_sr, p_si = _cols_to_state(sfin_p)
    p_wk = k_p[:, T - w:].reshape(B, w, SWA_KV_HEADS, HEAD_DIM)
    p_wv = v_p[:, T - w:].reshape(B, w, SWA_KV_HEADS, HEAD_DIM)
    p_mk = mk.reshape(B, MEM_TOKENS, MEM_HEADS, HEAD_DIM)
    p_mv = mv.reshape(B, MEM_TOKENS, MEM_HEADS, HEAD_DIM)

    n_s = SB * ST
    pos_s = PAST_LEN + jnp.tile(jnp.arange(ST, dtype=jnp.int32), SB)
    cos_s, sin_s = _rope_tables(pos_s)
    xs = x_sample.reshape(1, n_s, D_MODEL)
    u_s, q_s, k_s, v_s, qm_s = _in_proj(xs, cos_s, sin_s, g1, win, ones, gq, gk, gm)
    u_tm = u_s.reshape(SB, ST, SSM_WIDTH).transpose(1, 0, 2).reshape(n_s, SSM_WIDTH)
    ossm_tm, sfin_s = _s5(u_tm, _state_to_cols(state_ssm_re[l], state_ssm_im[l]), lam, bmat, cmat,
                          dskip, wglu, bglu, bt=SB, tt=ST)
    ossm_s = ossm_tm.reshape(ST, SB, SSM_WIDTH).transpose(1, 0, 2).reshape(n_s, SSM_WIDTH)
    ck = cache_win_k[l].reshape(SB, w, SWA_KV_WIDTH)
    cv = cache_win_v[l].reshape(SB, w, SWA_KV_WIDTH)
    q_s2 = q_s.reshape(n_s, SWA_WIDTH)
    k_s2 = k_s.reshape(n_s, SWA_KV_WIDTH)
    v_s2 = v_s.reshape(n_s, SWA_KV_WIDTH)
    osw_s, omem_s = _attn_sample(sinks, q_s2, k_s2, v_s2, ck, cv, qm_s.reshape(n_s, MEM_WIDTH),
                                 cache_mem_k[l].reshape(SB, MEM_TOKENS, MEM_WIDTH),
                                 cache_mem_v[l].reshape(SB, MEM_TOKENS, MEM_WIDTH),
                                 ts=ST, start=PAST_LEN)
    h_s, xn_s = _out_proj(xs, ossm_s, osw_s.reshape(1, n_s, SWA_WIDTH), omem_s.reshape(1, n_s, MEM_WIDTH),
                          wo, g2)
    y_s = _peer(xn_s.reshape(n_s, D_MODEL), h_s.reshape(n_s, D_MODEL), wqt, k1, k2, u_tab, vt_tab)
    y_s = y_s.reshape(SB, ST, D_MODEL)
    s_sr, s_si = _cols_to_state(sfin_s)
    s_wk = jnp.concatenate([ck, k_s2.reshape(SB, ST, SWA_KV_WIDTH)], axis=1)[:, -w:]
    s_wv = jnp.concatenate([cv, v_s2.reshape(SB, ST, SWA_KV_WIDTH)], axis=1)[:, -w:]
    s_wk = s_wk.reshape(SB, w, SWA_KV_HEADS, HEAD_DIM)
    s_wv = s_wv.reshape(SB, w, SWA_KV_HEADS, HEAD_DIM)

    st = lambda a: a[None]
    return (y_p, y_s, st(p_sr), st(p_si), st(p_wk), st(p_wv), st(p_mk), st(p_mv),
            st(s_sr), st(s_si), st(s_wk), st(s_wv))
```

```python
import functools
import math

import jax
import jax.numpy as jnp
import numpy as np
from jax import lax
from jax.experimental import pallas as pl
from jax.experimental.pallas import tpu as pltpu

F32 = jnp.float32
BF16 = jnp.bfloat16

D_MODEL = 1024
HEAD_DIM = 64
EPS = 1e-6
ROPE_THETA = 10000.0
PAST_LEN = 8192
SSM_WIDTH = 512
SSM_GROUP = 16
SSM_GROUPS = 32
SSM_STATE = 64
SSM_HALF_GROUPS = SSM_GROUPS // 2
SSM_HALF_STATE = SSM_HALF_GROUPS * SSM_STATE
SSM_COLS = 2 * 2 * SSM_HALF_STATE
SWA_Q_HEADS = 4
SWA_KV_HEADS = 2
SWA_REP = 2
SWA_WIDTH = 256
SWA_KV_WIDTH = 128
WINDOW = 128
MEM_TOKENS = 256
MEM_HEADS = 4
MEM_WIDTH = 256
IN_WIDTH = 1280
PEER_HEADS = 8
PEER_KEYS = 128
PEER_EXPERTS = PEER_KEYS * PEER_KEYS
PEER_TOPK = 16
PEER_HALF = 128

LANES = 128
VMEM_LIMIT = 60 * 1024 * 1024

TOKEN_TILE = 512
PEER_EXPERT_BLOCK = 1024
PEER_GATE_KEYS = 4
S5_TIME_TILE = 64

_CAND_COUNT = [PEER_TOPK // (a + 1) for a in range(PEER_TOPK)]
_CAND_ROW0 = [0, 16, 24, 32, 36, 40, 42, 44, 48, 49, 50, 51, 52, 53, 54, 55]
_CAND_ROWS = 56


def _cparams(sem):
    return pltpu.CompilerParams(dimension_semantics=sem, vmem_limit_bytes=VMEM_LIMIT)


def _rms(x, g):
    return x * lax.rsqrt(jnp.mean(x * x, axis=-1, keepdims=True) + EPS) * g


def _gelu(x):
    return 0.5 * x * (1.0 + lax.erf(x * math.sqrt(0.5)))


def _dot(a, b):
    return jnp.dot(a, b, preferred_element_type=F32)


def _dot_nt(a, b):
    return lax.dot_general(a, b, (((1,), (1,)), ((), ())), preferred_element_type=F32)


def _div(x, n):
    return x >> (n.bit_length() - 1) if n & (n - 1) == 0 else x // n


def _mod(x, n):
    return x & (n - 1) if n & (n - 1) == 0 else x % n


def _head_rms(x, ones_bd, g):
    sq = x * x
    hi = sq.astype(BF16)
    lo = (sq - hi.astype(F32)).astype(BF16)
    ms = (_dot(hi, ones_bd) + _dot(lo, ones_bd)) * np.float32(1.0 / HEAD_DIM)
    return x * lax.rsqrt(ms + EPS) * g


def _rope(x, cos, sin_signed):
    w = x.shape[-1]
    lane = lax.broadcasted_iota(jnp.int32, x.shape, 1)
    first_half = _mod(lane, HEAD_DIM) < (HEAD_DIM // 2)
    partner = jnp.where(first_half, pltpu.roll(x, w - HEAD_DIM // 2, 1),
                        pltpu.roll(x, HEAD_DIM // 2, 1))
    return x * cos + partner * sin_signed


def _in_proj_kernel(x_ref, g1_ref, win_ref, ones_ref, gq_ref, gk_ref, gm_ref, cos_ref, sin_ref,
                    u_ref, q_ref, k_ref, v_ref, qm_ref):
    x = x_ref[0]
    xn = _rms(x, g1_ref[...])
    proj = _dot(xn.astype(BF16), win_ref[...])
    u_ref[...] = proj[:, :SSM_WIDTH]
    q = proj[:, 512:768]
    k = proj[:, 768:896]
    v_ref[0] = proj[:, 896:1024]
    qm = proj[:, 1024:1280]
    ones = ones_ref[...]
    cos = cos_ref[...]
    sin = sin_ref[...]
    q_ref[0] = _rope(_head_rms(q, ones, gq_ref[...]), cos, sin)
    k_ref[0] = _rope(_head_rms(k, ones[:SWA_KV_WIDTH, :SWA_KV_WIDTH], gk_ref[...]),
                     cos[:, :SWA_KV_WIDTH], sin[:, :SWA_KV_WIDTH])
    qm_ref[0] = _head_rms(qm, ones, gm_ref[...])


def _in_proj(x, cos, sin, g1, win, ones, gq, gk, gm):
    B, T, _ = x.shape
    tt = min(TOKEN_TILE, T)
    grid = (B, T // tt)
    full = lambda shape: pl.BlockSpec(shape, lambda b, t: (0,) * len(shape))
    return pl.pallas_call(
        _in_proj_kernel,
        grid=grid,
        in_specs=[
            pl.BlockSpec((1, tt, D_MODEL), lambda b, t: (b, t, 0)),
            full((1, D_MODEL)), full((D_MODEL, IN_WIDTH)), full((SWA_WIDTH, SWA_WIDTH)),
            full((1, SWA_WIDTH)), full((1, SWA_KV_WIDTH)), full((1, MEM_WIDTH)),
            pl.BlockSpec((tt, SWA_WIDTH), lambda b, t: (t, 0)),
            pl.BlockSpec((tt, SWA_WIDTH), lambda b, t: (t, 0)),
        ],
        out_specs=[
            pl.BlockSpec((tt, SSM_WIDTH), lambda b, t: (t, b)),
            pl.BlockSpec((1, tt, SWA_WIDTH), lambda b, t: (b, t, 0)),
            pl.BlockSpec((1, tt, SWA_KV_WIDTH), lambda b, t: (b, t, 0)),
            pl.BlockSpec((1, tt, SWA_KV_WIDTH), lambda b, t: (b, t, 0)),
            pl.BlockSpec((1, tt, MEM_WIDTH), lambda b, t: (b, t, 0)),
        ],
        out_shape=[
            jax.ShapeDtypeStruct((T, B * SSM_WIDTH), F32),
            jax.ShapeDtypeStruct((B, T, SWA_WIDTH), F32),
            jax.ShapeDtypeStruct((B, T, SWA_KV_WIDTH), F32),
            jax.ShapeDtypeStruct((B, T, SWA_KV_WIDTH), F32),
            jax.ShapeDtypeStruct((B, T, MEM_WIDTH), F32),
        ],
        compiler_params=_cparams(("parallel", "parallel")),
        name="in_proj",
    )(x, g1, win, ones, gq, gk, gm, cos, sin)


def _s5_kernel(u_ref, s0_ref, lam_ref, bmat_ref, cmat_ref, d_ref, wglu_ref, bglu_ref,
               o_ref, sfin_ref, s_scr, carry_scr, *, bt, tt):
    @pl.when(pl.program_id(0) == 0)
    def _():
        carry_scr[...] = s0_ref[...]

    u = u_ref[...]
    ub = u.astype(BF16)
    hw = 2 * SSM_HALF_STATE
    for j in range(2):
        s_scr[:, j * hw:(j + 1) * hw] = _dot(ub[:, j * 256:(j + 1) * 256], bmat_ref[j])

    def step(t, carry):
        r0 = pl.multiple_of(t * bt, bt)
        for j in range(2):
            c_re = pl.ds(j * hw, SSM_HALF_STATE)
            c_im = pl.ds(j * hw + SSM_HALF_STATE, SSM_HALF_STATE)
            p_re = carry_scr[:, c_re]
            p_im = carry_scr[:, c_im]
            l_re = lam_ref[:, c_re]
            l_im = lam_ref[:, c_im]
            n_re = l_re * p_re - l_im * p_im + s_scr[pl.ds(r0, bt), c_re]
            n_im = l_re * p_im + l_im * p_re + s_scr[pl.ds(r0, bt), c_im]
            s_scr[pl.ds(r0, bt), c_re] = n_re
            s_scr[pl.ds(r0, bt), c_im] = n_im
            carry_scr[:, c_re] = n_re
            carry_scr[:, c_im] = n_im
        return carry

    lax.fori_loop(0, tt, step, 0)
    sfin_ref[...] = carry_scr[...]

    ys = [_dot(s_scr[:, j * hw:(j + 1) * hw].astype(BF16), cmat_ref[j]) for j in range(2)]
    y = jnp.concatenate(ys, axis=-1) + d_ref[...] * u
    y = _gelu(y)
    z = _dot(y.astype(BF16), wglu_ref[...]) + bglu_ref[...]
    o_ref[...] = y * jax.nn.sigmoid(z)


def _s5(u_tm, s0, lam, bmat, cmat, d, wglu, bglu, *, bt, tt):
    rows = u_tm.shape[0]
    nt = rows // (bt * tt)
    full = lambda shape: pl.BlockSpec(shape, lambda t: (0,) * len(shape))
    return pl.pallas_call(
        functools.partial(_s5_kernel, bt=bt, tt=tt),
        grid=(nt,),
        in_specs=[
            pl.BlockSpec((bt * tt, SSM_WIDTH), lambda t: (t, 0)),
            full((bt, SSM_COLS)), full((1, SSM_COLS)),
            full((2, 256, 2 * SSM_HALF_STATE)), full((2, 2 * SSM_HALF_STATE, 256)),
            full((1, SSM_WIDTH)), full((SSM_WIDTH, SSM_WIDTH)), full((1, SSM_WIDTH)),
        ],
        out_specs=[
            pl.BlockSpec((bt * tt, SSM_WIDTH), lambda t: (t, 0)),
            full((bt, SSM_COLS)),
        ],
        out_shape=[
            jax.ShapeDtypeStruct((rows, SSM_WIDTH), F32),
            jax.ShapeDtypeStruct((bt, SSM_COLS), F32),
        ],
        scratch_shapes=[pltpu.VMEM((bt * tt, SSM_COLS), F32), pltpu.VMEM((bt, SSM_COLS), F32)],
        compiler_params=_cparams(("arbitrary",)),
        name="s5_mixer",
    )(u_tm, s0, lam, bmat, cmat, d, wglu, bglu)


def _mem_kv_kernel(m_ref, g_ref, w_ref, ones_ref, gk_ref, k_ref, v_ref):
    xn = _rms(m_ref[...], g_ref[...])
    kv = _dot(xn.astype(BF16), w_ref[...])
    k_ref[...] = _head_rms(kv[:, :MEM_WIDTH], ones_ref[...], gk_ref[...])
    v_ref[...] = kv[:, MEM_WIDTH:]


def _mem_kv(mem_rows, g, w, ones, gk):
    rows = mem_rows.shape[0]
    tt = min(TOKEN_TILE, rows)
    full = lambda shape: pl.BlockSpec(shape, lambda t: (0,) * len(shape))
    return pl.pallas_call(
        _mem_kv_kernel,
        grid=(rows // tt,),
        in_specs=[pl.BlockSpec((tt, D_MODEL), lambda t: (t, 0)), full((1, D_MODEL)),
                  full((D_MODEL, 2 * MEM_WIDTH)), full((MEM_WIDTH, MEM_WIDTH)), full((1, MEM_WIDTH))],
        out_specs=[pl.BlockSpec((tt, MEM_WIDTH), lambda t: (t, 0))] * 2,
        out_shape=[jax.ShapeDtypeStruct((rows, MEM_WIDTH), F32)] * 2,
        compiler_params=_cparams(("parallel",)),
        name="mem_kv",
    )(mem_rows, g, w, ones, gk)


def _softmax_pv(s, v_b, sink=None):
    m = jnp.max(s, axis=-1, keepdims=True)
    if sink is not None:
        m = jnp.maximum(m, sink)
    p = jnp.exp(s - m)
    den = jnp.sum(p, axis=-1, keepdims=True)
    if sink is not None:
        den = den + jnp.exp(sink - m)
    return _dot((p / den).astype(BF16), v_b)


def _attn_prompt_kernel(sinks_ref, q_ref, kp_ref, kc_ref, vp_ref, vc_ref, qm_ref, mk_ref, mv_ref,
                        osw_ref, omem_ref):
    nb = pl.program_id(1)
    scale = np.float32(HEAD_DIM ** -0.5)
    q = q_ref[0].astype(BF16)
    kk = jnp.concatenate([kp_ref[0], kc_ref[0]], axis=0).astype(BF16)
    vv = jnp.concatenate([vp_ref[0], vc_ref[0]], axis=0).astype(BF16)
    qi = lax.broadcasted_iota(jnp.int32, (WINDOW, 2 * WINDOW), 0)
    ki = lax.broadcasted_iota(jnp.int32, (WINDOW, 2 * WINDOW), 1) - WINDOW
    diff = qi - ki
    valid = (diff >= 0) & (diff < WINDOW) & (nb * WINDOW + ki >= 0)
    outs = []
    for hq in range(SWA_Q_HEADS):
        kv = hq // SWA_REP
        s = _dot_nt(q[:, hq * HEAD_DIM:(hq + 1) * HEAD_DIM],
                    kk[:, kv * HEAD_DIM:(kv + 1) * HEAD_DIM]) * scale
        s = jnp.where(valid, s, -jnp.inf)
        outs.append(_softmax_pv(s, vv[:, kv * HEAD_DIM:(kv + 1) * HEAD_DIM], sinks_ref[hq]))
    osw_ref[0] = jnp.concatenate(outs, axis=-1)

    qm = qm_ref[0].astype(BF16)
    mk = mk_ref[0].astype(BF16)
    mv = mv_ref[0].astype(BF16)
    outs = []
    for h in range(MEM_HEADS):
        sl = slice(h * HEAD_DIM, (h + 1) * HEAD_DIM)
        s = _dot_nt(qm[:, sl], mk[:, sl]) * scale
        outs.append(_softmax_pv(s, mv[:, sl]))
    omem_ref[0] = jnp.concatenate(outs, axis=-1)


def _attn_prompt(sinks, q, k, v, qm, mk, mv):
    B, T, _ = q.shape
    nb = T // WINDOW
    blk = lambda w: pl.BlockSpec((1, WINDOW, w), lambda b, n: (b, n, 0))
    prev = lambda w: pl.BlockSpec((1, WINDOW, w), lambda b, n: (b, jnp.maximum(n - 1, 0), 0))
    memb = pl.BlockSpec((1, MEM_TOKENS, MEM_WIDTH), lambda b, n: (b, 0, 0))
    return pl.pallas_call(
        _attn_prompt_kernel,
        grid=(B, nb),
        in_specs=[pl.BlockSpec(memory_space=pltpu.SMEM),
                  blk(SWA_WIDTH), prev(SWA_KV_WIDTH), blk(SWA_KV_WIDTH), prev(SWA_KV_WIDTH),
                  blk(SWA_KV_WIDTH), blk(MEM_WIDTH), memb, memb],
        out_specs=[blk(SWA_WIDTH), blk(MEM_WIDTH)],
        out_shape=[jax.ShapeDtypeStruct((B, T, SWA_WIDTH), F32),
                   jax.ShapeDtypeStruct((B, T, MEM_WIDTH), F32)],
        compiler_params=_cparams(("parallel", "parallel")),
        name="attn_prompt",
    )(sinks, q, k, k, v, v, qm, mk, mv)


def _attn_sample_kernel(sinks_ref, q_ref, kn_ref, vn_ref, ck_ref, cv_ref, qm_ref, mk_ref, mv_ref,
                        osw_ref, omem_ref, *, bb, ts, start):
    scale = np.float32(HEAD_DIM ** -0.5)
    w = ck_ref.shape[1]
    nq = bb * ts
    q = q_ref[...].astype(BF16)
    kn = kn_ref[...].astype(BF16)
    vn = vn_ref[...].astype(BF16)
    ck = ck_ref[...].reshape(bb * w, SWA_KV_WIDTH).astype(BF16)
    cv = cv_ref[...].reshape(bb * w, SWA_KV_WIDTH).astype(BF16)

    rq = lax.broadcasted_iota(jnp.int32, (nq, bb * w), 0)
    cc = lax.broadcasted_iota(jnp.int32, (nq, bb * w), 1)
    qpos = start + _mod(rq, ts)
    kpos = start - w + _mod(cc, w)
    diff = qpos - kpos
    valid_c = (_div(rq, ts) == _div(cc, w)) & (diff >= 0) & (diff < WINDOW) & (kpos >= 0)
    rq = lax.broadcasted_iota(jnp.int32, (nq, nq), 0)
    cn = lax.broadcasted_iota(jnp.int32, (nq, nq), 1)
    diff = _mod(rq, ts) - _mod(cn, ts)
    valid_n = (_div(rq, ts) == _div(cn, ts)) & (diff >= 0) & (diff < WINDOW)

    outs = []
    for hq in range(SWA_Q_HEADS):
        kv = hq // SWA_REP
        qs = q[:, hq * HEAD_DIM:(hq + 1) * HEAD_DIM]
        ks = slice(kv * HEAD_DIM, (kv + 1) * HEAD_DIM)
        s_c = jnp.where(valid_c, _dot_nt(qs, ck[:, ks]) * scale, -jnp.inf)
        s_n = jnp.where(valid_n, _dot_nt(qs, kn[:, ks]) * scale, -jnp.inf)
        sink = sinks_ref[hq]
        m = jnp.maximum(jnp.maximum(jnp.max(s_c, axis=-1, keepdims=True),
                                    jnp.max(s_n, axis=-1, keepdims=True)), sink)
        p_c = jnp.exp(s_c - m)
        p_n = jnp.exp(s_n - m)
        den = (jnp.sum(p_c, axis=-1, keepdims=True) + jnp.sum(p_n, axis=-1, keepdims=True)
               + jnp.exp(sink - m))
        outs.append(_dot((p_c / den).astype(BF16), cv[:, ks]) + _dot((p_n / den).astype(BF16), vn[:, ks]))
    osw_ref[...] = jnp.concatenate(outs, axis=-1)

    qm = qm_ref[...].astype(BF16)
    mk = mk_ref[...].reshape(bb * MEM_TOKENS, MEM_WIDTH).astype(BF16)
    mv = mv_ref[...].reshape(bb * MEM_TOKENS, MEM_WIDTH).astype(BF16)
    rq = lax.broadcasted_iota(jnp.int32, (nq, bb * MEM_TOKENS), 0)
    cm = lax.broadcasted_iota(jnp.int32, (nq, bb * MEM_TOKENS), 1)
    valid_m = _div(rq, ts) == _div(cm, MEM_TOKENS)
    outs = []
    for h in range(MEM_HEADS):
        sl = slice(h * HEAD_DIM, (h + 1) * HEAD_DIM)
        s = jnp.where(valid_m, _dot_nt(qm[:, sl], mk[:, sl]) * scale, -jnp.inf)
        outs.append(_softmax_pv(s, mv[:, sl]))
    omem_ref[...] = jnp.concatenate(outs, axis=-1)


def _attn_sample(sinks, q, kn, vn, ck, cv, qm, cmk, cmv, *, ts, start):
    B, w, _ = ck.shape
    bb = 8
    rows = lambda wd: pl.BlockSpec((bb * ts, wd), lambda i: (i, 0))
    blk3 = lambda n, wd: pl.BlockSpec((bb, n, wd), lambda i: (i, 0, 0))
    return pl.pallas_call(
        functools.partial(_attn_sample_kernel, bb=bb, ts=ts, start=start),
        grid=(B // bb,),
        in_specs=[pl.BlockSpec(memory_space=pltpu.SMEM),
                  rows(SWA_WIDTH), rows(SWA_KV_WIDTH), rows(SWA_KV_WIDTH),
                  blk3(w, SWA_KV_WIDTH), blk3(w, SWA_KV_WIDTH), rows(MEM_WIDTH),
                  blk3(MEM_TOKENS, MEM_WIDTH), blk3(MEM_TOKENS, MEM_WIDTH)],
        out_specs=[rows(SWA_WIDTH), rows(MEM_WIDTH)],
        out_shape=[jax.ShapeDtypeStruct((B * ts, SWA_WIDTH), F32),
                   jax.ShapeDtypeStruct((B * ts, MEM_WIDTH), F32)],
        compiler_params=_cparams(("parallel",)),
        name="attn_sample",
    )(sinks, q, kn, vn, ck, cv, qm, cmk, cmv)


def _out_proj_kernel(x_ref, ossm_ref, osw_ref, omem_ref, wo_ref, g2_ref, h_ref, xn_ref):
    h = x_ref[0]
    h = h + _dot(ossm_ref[...].astype(BF16), wo_ref[0:512, :])
    h = h + _dot(osw_ref[0].astype(BF16), wo_ref[512:768, :])
    h = h + _dot(omem_ref[0].astype(BF16), wo_ref[768:1024, :])
    h_ref[0] = h
    xn_ref[0] = _rms(h, g2_ref[...]).astype(BF16)


def _out_proj(x, ossm, osw, omem, wo, g2):
    B, T, _ = x.shape
    tt = min(TOKEN_TILE, T)
    full = lambda shape: pl.BlockSpec(shape, lambda b, t: (0,) * len(shape))
    blk = lambda wd: pl.BlockSpec((1, tt, wd), lambda b, t: (b, t, 0))
    return pl.pallas_call(
        _out_proj_kernel,
        grid=(B, T // tt),
        in_specs=[blk(D_MODEL), pl.BlockSpec((tt, SSM_WIDTH), lambda b, t: (t, b)),
                  blk(SWA_WIDTH), blk(MEM_WIDTH), full((D_MODEL, D_MODEL)), full((1, D_MODEL))],
        out_specs=[blk(D_MODEL), blk(D_MODEL)],
        out_shape=[jax.ShapeDtypeStruct((B, T, D_MODEL), F32),
                   jax.ShapeDtypeStruct((B, T, D_MODEL), BF16)],
        compiler_params=_cparams(("parallel", "parallel")),
        name="out_proj",
    )(x, ossm, osw, omem, wo, g2)


def _top16(s, iota):
    work = s
    rank = jnp.full(s.shape, float(PEER_TOPK), F32)
    vals = []
    for a in range(PEER_TOPK):
        m = jnp.max(work, axis=0, keepdims=True)
        idx = jnp.min(jnp.where(work == m, iota, float(PEER_KEYS)), axis=0, keepdims=True)
        sel = iota == idx
        rank = jnp.where(sel, float(a), rank)
        work = jnp.where(sel, -jnp.inf, work)
        vals.append(m)
    return vals, rank


def _sort_pairs(n):
    pairs = []
    t = max(1, (n - 1).bit_length())
    p = 1 << (t - 1)
    while p > 0:
        q, r, d = 1 << (t - 1), 0, p
        while d > 0:
            pairs += [(i, i + d) for i in range(n - d) if (i & p) == r]
            d, q, r = q - p, q >> 1, p
        p >>= 1
    return pairs


_SORT16_PAIRS = _sort_pairs(PEER_TOPK)


def _cmp_exchange(xs, i, j):
    xs[i], xs[j] = jnp.maximum(xs[i], xs[j]), jnp.minimum(xs[i], xs[j])


def _top16_sorted(s):
    n = PEER_TOPK
    xs = [s[8 * k:8 * k + 8, :] for k in range(n)]
    for i, j in _SORT16_PAIRS:
        _cmp_exchange(xs, i, j)
    for shift in (4, 2, 1):
        other = [pltpu.roll(x, shift, 0) for x in xs]
        xs = [jnp.maximum(xs[k], other[n - 1 - k]) for k in range(n)]
        for d in (8, 4, 2, 1):
            for k in range(n):
                if k & d == 0:
                    _cmp_exchange(xs, k, k + d)
    return xs


def _joint_counts(v1, v2, cand_scr):
    L = v1[0].shape[1]
    cand_scr[...] = jnp.full((_CAND_ROWS, L), -jnp.inf, F32)
    for a in range(PEER_TOPK):
        for b in range(_CAND_COUNT[a]):
            cand_scr[pl.ds(_CAND_ROW0[a] + b, 1), :] = v1[a] + v2[b]
    work = cand_scr[...]
    iota_c = lax.broadcasted_iota(jnp.int32, (_CAND_ROWS, L), 0).astype(F32)
    taken = jnp.zeros((_CAND_ROWS, L), F32)
    m0 = None
    z = None
    for k in range(PEER_TOPK):
        m = jnp.max(work, axis=0, keepdims=True)
        idx = jnp.min(jnp.where(work == m, iota_c, float(_CAND_ROWS)), axis=0, keepdims=True)
        sel = iota_c == idx
        taken = jnp.where(sel, 1.0, taken)
        work = jnp.where(sel, -jnp.inf, work)
        if k == 0:
            m0 = m
            z = jnp.ones_like(m)
        else:
            z = z + jnp.exp(m - m0)

    cnt = []
    for a in range(PEER_TOPK):
        lo, hi = _CAND_ROW0[a], _CAND_ROW0[a] + _CAND_COUNT[a]
        if hi - lo == 1:
            cnt.append(taken[lo:hi])
            continue
        t0, t1 = (lo // 8) * 8, -(-hi // 8) * 8
        part = taken[t0:t1]
        if (lo, hi) != (t0, t1):
            rows = iota_c[t0:t1]
            part = jnp.where((rows >= float(lo)) & (rows < float(hi)), part, 0.0)
        cnt.append(jnp.sum(part, axis=0, keepdims=True))
    return cnt, z


def _route_tile(s1, s2, cand_scr, n1_out, r2_out):
    n = PEER_TOPK
    t1 = _top16_sorted(s1)
    t2 = _top16_sorted(s2)
    v1 = [t[0:1, :] for t in t1]
    v2 = [t[0:1, :] for t in t2]
    cnt, z = _joint_counts(v1, v2, cand_scr)

    bad = jnp.zeros((8, s1.shape[1]), F32)
    for s, t in ((s1, t1), (s2, t2)):
        gap = t[0] - t[1]
        for a in range(1, n - 1):
            gap = jnp.minimum(gap, t[a] - t[a + 1])
        at_least = jnp.zeros((8, s.shape[1]), F32)
        for k in range(n):
            at_least = at_least + jnp.where(s[8 * k:8 * k + 8, :] >= t[n - 1], 1.0, 0.0)
        at_least = jnp.sum(at_least, axis=0, keepdims=True)
        bad = jnp.where((gap <= 0.0) | (at_least != float(n)), 1.0, bad)
    has_ties = jnp.max(bad) > 0.0

    @pl.when(jnp.logical_not(has_ties))
    def _by_value():
        cnt_b = [jnp.broadcast_to(c, (8, c.shape[1])) for c in cnt]
        ranks = []
        for k in range(n):
            x1 = s1[8 * k:8 * k + 8, :]
            x2 = s2[8 * k:8 * k + 8, :]
            n1 = cnt_b[0]
            r2 = jnp.zeros_like(x2)
            for a in range(n):
                n1 = jnp.where(t1[a] > x1, cnt_b[a + 1] if a + 1 < n else 0.0, n1)
                r2 = jnp.where(t2[a] > x2, float(a + 1), r2)
            n1_out[8 * k:8 * k + 8, :] = n1
            ranks.append(r2)
        r2_out[...] = jnp.concatenate(ranks, axis=0)

    @pl.when(has_ties)
    def _by_index():
        iota = lax.broadcasted_iota(jnp.int32, s1.shape, 0).astype(F32)
        _, r1 = _top16(s1, iota)
        _, r2 = _top16(s2, iota)
        n1 = jnp.zeros(s1.shape, F32)
        for a in range(n):
            n1 = jnp.where(r1 == float(a), cnt[a], n1)
        n1_out[...] = n1
        r2_out[...] = r2

    c1 = jnp.exp(s1 - v1[0]) / z
    e2 = jnp.exp(s2 - v2[0])
    return c1, e2


def _peer_kernel(xn_ref, h_ref, wqt_ref, k1_ref, k2_ref, u_ref, vt_ref, y_ref,
                 n1_scr, c1_scr, s2_scr, r2_scr, e2_scr, acc_scr, a_scr, w_scr,
                 q_scr, xt_scr, e2f_scr, cand_scr, *, n_steps):
    g = pl.program_id(1)
    ng = n_steps
    tt = xn_ref.shape[0]
    eb = u_ref.shape[0]
    keys_per_block = eb // PEER_KEYS
    assert keys_per_block % 8 == 0

    @pl.when(g == 0)
    def _route():
        xt_scr[...] = xn_ref[...].T
        xt = xt_scr[...]
        q_scr[...] = _dot(wqt_ref[...], xt).astype(BF16)
        for h in range(PEER_HEADS):
            for side, dst in enumerate((n1_scr, s2_scr)):
                k_ref = (k1_ref, k2_ref)[side]
                r0 = h * 2 * PEER_HALF + side * PEER_HALF
                dst[h] = _dot(k_ref[...], q_scr[r0:r0 + PEER_HALF, :])

        def body(i, carry):
            h = i // (tt // LANES)
            lanes = pl.ds(pl.multiple_of((i % (tt // LANES)) * LANES, LANES), LANES)
            c1, e2 = _route_tile(n1_scr[h, :, lanes], s2_scr[h, :, lanes], cand_scr,
                                 n1_scr.at[h, :, lanes], s2_scr.at[h, :, lanes])
            c1_scr[h, :, lanes] = c1
            e2f_scr[h, :, lanes] = e2
            return carry

        lax.fori_loop(0, PEER_HEADS * (tt // LANES), body, 0)
        for h in range(PEER_HEADS):
            r2_scr[h] = s2_scr[h].astype(BF16)
            e2_scr[h] = e2f_scr[h].astype(BF16)
        acc_scr[...] = jnp.zeros_like(acc_scr)

    def gate(a_scr, w_scr, block):
        zero = jnp.zeros((), BF16)
        group = PEER_GATE_KEYS
        for kt in range(keys_per_block // 8):
            tile0 = pl.multiple_of(block * keys_per_block + kt * 8, 8)
            for k0 in range(0, 8, group):
                def lane_tile(c, carry, kt=kt, k0=k0, tile0=tile0):
                    lanes = pl.ds(pl.multiple_of(c * LANES, LANES), LANES)
                    gts = [jnp.zeros((PEER_KEYS // 16, 16, LANES), BF16) for _ in range(group)]
                    for h in range(PEER_HEADS):
                        r2 = r2_scr[h, :, lanes].reshape(PEER_KEYS // 16, 16, LANES)
                        e2 = e2_scr[h, :, lanes].reshape(PEER_KEYS // 16, 16, LANES)
                        n_tile = n1_scr[h, pl.ds(tile0, 8), lanes]
                        c_tile = c1_scr[h, pl.ds(tile0, 8), lanes]
                        for ii in range(group):
                            row = slice(k0 + ii, k0 + ii + 1)
                            n_b = jnp.broadcast_to(n_tile[row, :], (16, LANES)).astype(BF16)[None]
                            c_b = jnp.broadcast_to(c_tile[row, :], (16, LANES)).astype(BF16)[None]
                            gts[ii] = gts[ii] + jnp.where(r2 < n_b, e2, zero) * c_b
                    for ii in range(group):
                        r0 = (kt * 8 + k0 + ii) * PEER_KEYS
                        rows = slice(r0, r0 + PEER_KEYS)
                        act = _gelu(a_scr[rows, lanes].astype(BF16))
                        w_scr[rows, lanes] = gts[ii].reshape(PEER_KEYS, LANES) * act
                    return carry

                lax.fori_loop(0, tt // LANES, lane_tile, 0)

    @pl.when(g < ng)
    def _scores():
        a_scr[...] = _dot(u_ref[...], xt_scr[...])

    @pl.when(g > 0)
    def _v_product():
        acc_scr[...] += _dot(vt_ref[...], w_scr[...])

    @pl.when(g < ng)
    def _weights():
        gate(a_scr, w_scr, g)

    @pl.when(g == ng)
    def _fin():
        y_ref[...] = h_ref[...] + acc_scr[...].T


def _peer(xn, h, wqt, k1, k2, u_tab, vt_tab):
    n = xn.shape[0]
    tt = min(TOKEN_TILE, n)
    eb = PEER_EXPERT_BLOCK
    ng = PEER_EXPERTS // eb
    full = lambda shape: pl.BlockSpec(shape, lambda i, g: (0,) * len(shape), pipeline_mode=pl.Buffered(1))
    tok = pl.BlockSpec((tt, D_MODEL), lambda i, g: (i, 0))
    head_f32 = pltpu.VMEM((PEER_HEADS, PEER_KEYS, tt), F32)
    head_bf16 = pltpu.VMEM((PEER_HEADS, PEER_KEYS, tt), BF16)
    return pl.pallas_call(
        functools.partial(_peer_kernel, n_steps=ng),
        grid=(n // tt, ng + 1),
        in_specs=[tok, tok, full((2 * PEER_HEADS * PEER_HALF, D_MODEL)),
                  full((PEER_KEYS, PEER_HALF)), full((PEER_KEYS, PEER_HALF)),
                  pl.BlockSpec((eb, D_MODEL), lambda i, g: (jnp.minimum(g, ng - 1), 0)),
                  pl.BlockSpec((D_MODEL, eb), lambda i, g: (0, jnp.maximum(g - 1, 0)))],
        out_specs=tok,
        out_shape=jax.ShapeDtypeStruct((n, D_MODEL), F32),
        scratch_shapes=[head_f32, head_f32, head_f32, head_bf16, head_bf16,
                        pltpu.VMEM((D_MODEL, tt), F32),
                        pltpu.VMEM((eb, tt), F32), pltpu.VMEM((eb, tt), BF16),
                        pltpu.VMEM((2 * PEER_HEADS * PEER_HALF, tt), BF16),
                        pltpu.VMEM((D_MODEL, tt), BF16), head_f32,
                        pltpu.VMEM((_CAND_ROWS, LANES), F32)],
        compiler_params=_cparams(("parallel", "arbitrary")),
        name="peer",
    )(xn, h, wqt, k1, k2, u_tab, vt_tab)


def _rope_tables(pos):
    half = HEAD_DIM // 2
    inv = ROPE_THETA ** (-jnp.arange(half, dtype=F32) / half)
    ang = pos.astype(F32)[:, None] * inv[None, :]
    cos = jnp.cos(ang)
    sin = jnp.sin(ang)
    cos = jnp.tile(jnp.concatenate([cos, cos], axis=-1), (1, SWA_Q_HEADS))
    sin = jnp.tile(jnp.concatenate([-sin, sin], axis=-1), (1, SWA_Q_HEADS))
    return cos, sin


def _ssm_params(log_dt, a_re, a_im, b_re, b_im, c_re, c_im):
    dt = jnp.exp(log_dt)
    mag = jnp.exp(a_re * dt)
    lam_re = mag * jnp.cos(a_im * dt)
    lam_im = mag * jnp.sin(a_im * dt)
    den = a_re * a_re + a_im * a_im
    z_re = ((lam_re - 1.0) * a_re + lam_im * a_im) / den
    z_im = (lam_im * a_re - (lam_re - 1.0) * a_im) / den
    bb_re = z_re[..., None] * b_re - z_im[..., None] * b_im
    bb_im = z_re[..., None] * b_im + z_im[..., None] * b_re
    hg = SSM_HALF_GROUPS
    eye = jnp.eye(hg, dtype=F32)
    bb = jnp.stack([bb_re, bb_im]).reshape(2, 2, hg, SSM_STATE, SSM_GROUP)
    bmat = jnp.einsum('rjgnc,gh->jgcrhn', bb, eye).reshape(2, hg * SSM_GROUP, 2 * SSM_HALF_STATE)
    cc = jnp.stack([c_re, -c_im]).reshape(2, 2, hg, SSM_GROUP, SSM_STATE)
    cmat = jnp.einsum('rjgcn,gh->jrgnhc', cc, eye).reshape(2, 2 * SSM_HALF_STATE, hg * SSM_GROUP)
    lam = jnp.stack([lam_re.reshape(2, SSM_HALF_STATE), lam_im.reshape(2, SSM_HALF_STATE)], axis=1)
    return lam.reshape(1, SSM_COLS), bmat.astype(BF16), cmat.astype(BF16)


def _state_to_cols(s_re, s_im):
    b = s_re.shape[0]
    st = jnp.stack([s_re.reshape(b, 2, SSM_HALF_STATE), s_im.reshape(b, 2, SSM_HALF_STATE)], axis=2)
    return st.reshape(b, SSM_COLS)


def _cols_to_state(cols):
    b = cols.shape[0]
    st = cols.reshape(b, 2, 2, SSM_HALF_STATE)
    return (st[:, :, 0].reshape(b, SSM_GROUPS, SSM_STATE), st[:, :, 1].reshape(b, SSM_GROUPS, SSM_STATE))


def kernel(x_prompt, x_sample, state_ssm_re, state_ssm_im, cache_win_k, cache_win_v, cache_mem_k, cache_mem_v, mem_prompt, norm1_g, w_in, ssm_log_dt, ssm_a_re, ssm_a_im, ssm_b_re, ssm_b_im, ssm_c_re, ssm_c_im, ssm_d, w_glu, b_glu, swa_q_norm, swa_k_norm, swa_sinks, mem_norm_g, w_mem_kv, mem_q_norm, mem_k_norm, w_out, norm2_g, peer_wq, peer_k1, peer_k2, peer_u, peer_v):
    depth = norm1_g.shape[0]
    assert depth == 1
    l = 0
    B, T, _ = x_prompt.shape
    SB, ST, _ = x_sample.shape
    w = cache_win_k.shape[2]

    row = lambda a: a.reshape(1, -1)
    g1 = row(norm1_g[l])
    g2 = row(norm2_g[l])
    win = w_in[l].astype(BF16)
    wo = w_out[l].astype(BF16)
    wglu = w_glu[l].astype(BF16)
    bglu = row(b_glu[l])
    gq = row(jnp.tile(swa_q_norm[l], SWA_Q_HEADS))
    gk = row(jnp.tile(swa_k_norm[l], SWA_KV_HEADS))
    gm = row(jnp.tile(mem_q_norm[l], MEM_HEADS))
    gmk = row(jnp.tile(mem_k_norm[l], MEM_HEADS))
    gmem = row(mem_norm_g[l])
    wkv = w_mem_kv[l].astype(BF16)
    sinks = swa_sinks[l]
    head_id = np.arange(SWA_WIDTH) // HEAD_DIM
    ones = jnp.asarray(head_id[:, None] == head_id[None, :], dtype=BF16)
    lam, bmat, cmat = _ssm_params(ssm_log_dt[l], ssm_a_re[l], ssm_a_im[l], ssm_b_re[l], ssm_b_im[l],
                                  ssm_c_re[l], ssm_c_im[l])
    dskip = row(ssm_d[l])
    wqt = peer_wq[l].T.astype(BF16)
    k1 = peer_k1[l].astype(BF16)
    k2 = peer_k2[l].astype(BF16)
    u_tab = peer_u[l].astype(BF16)
    vt_tab = peer_v[l].T.astype(BF16)

    cos_p, sin_p = _rope_tables(jnp.arange(T, dtype=jnp.int32))
    u_p, q_p, k_p, v_p, qm_p = _in_proj(x_prompt, cos_p, sin_p, g1, win, ones, gq, gk, gm)
    zeros = jnp.zeros((B, SSM_COLS), F32)
    ossm_p, sfin_p = _s5(u_p.reshape(T * B, SSM_WIDTH), zeros, lam, bmat, cmat, dskip, wglu, bglu,
                         bt=B, tt=S5_TIME_TILE)
    mk, mv = _mem_kv(mem_prompt.reshape(B * MEM_TOKENS, D_MODEL), gmem, wkv, ones, gmk)
    mk = mk.reshape(B, MEM_TOKENS, MEM_WIDTH)
    mv = mv.reshape(B, MEM_TOKENS, MEM_WIDTH)
    osw_p, omem_p = _attn_prompt(sinks, q_p, k_p, v_p, qm_p, mk, mv)
    h_p, xn_p = _out_proj(x_prompt, ossm_p.reshape(T, B * SSM_WIDTH), osw_p, omem_p, wo, g2)
    y_p = _peer(xn_p.reshape(B * T, D_MODEL), h_p.reshape(B * T, D_MODEL), wqt, k1, k2, u_tab, vt_tab)
    y_p = y_p.reshape(B, T, D_MODEL)
    p_sr, p_si = _cols_to_state(sfin_p)
    p_wk = k_p[:, T - w:].reshape(B, w, SWA_KV_HEADS, HEAD_DIM)
    p_wv = v_p[:, T - w:].reshape(B, w, SWA_KV_HEADS, HEAD_DIM)
    p_mk = mk.reshape(B, MEM_TOKENS, MEM_HEADS, HEAD_DIM)
    p_mv = mv.reshape(B, MEM_TOKENS, MEM_HEADS, HEAD_DIM)

    n_s = SB * ST
    pos_s = PAST_LEN + jnp.tile(jnp.arange(ST, dtype=jnp.int32), SB)
    cos_s, sin_s = _rope_tables(pos_s)
    xs = x_sample.reshape(1, n_s, D_MODEL)
    u_s, q_s, k_s, v_s, qm_s = _in_proj(xs, cos_s, sin_s, g1, win, ones, gq, gk, gm)
    u_tm = u_s.reshape(SB, ST, SSM_WIDTH).transpose(1, 0, 2).reshape(n_s, SSM_WIDTH)
    ossm_tm, sfin_s = _s5(u_tm, _state_to_cols(state_ssm_re[l], state_ssm_im[l]), lam, bmat, cmat,
                          dskip, wglu, bglu, bt=SB, tt=ST)
    ossm_s = ossm_tm.reshape(ST, SB, SSM_WIDTH).transpose(1, 0, 2).reshape(n_s, SSM_WIDTH)
    ck = cache_win_k[l].reshape(SB, w, SWA_KV_WIDTH)
    cv = cache_win_v[l].reshape(SB, w, SWA_KV_WIDTH)
    q_s2 = q_s.reshape(n_s, SWA_WIDTH)
    k_s2 = k_s.reshape(n_s, SWA_KV_WIDTH)
    v_s2 = v_s.reshape(n_s, SWA_KV_WIDTH)
    osw_s, omem_s = _attn_sample(sinks, q_s2, k_s2, v_s2, ck, cv, qm_s.reshape(n_s, MEM_WIDTH),
                                 cache_mem_k[l].reshape(SB, MEM_TOKENS, MEM_WIDTH),
                                 cache_mem_v[l].reshape(SB, MEM_TOKENS, MEM_WIDTH),
                                 ts=ST, start=PAST_LEN)
    h_s, xn_s = _out_proj(xs, ossm_s, osw_s.reshape(1, n_s, SWA_WIDTH), omem_s.reshape(1, n_s, MEM_WIDTH),
                          wo, g2)
    y_s = _peer(xn_s.reshape(n_s, D_MODEL), h_s.reshape(n_s, D_MODEL), wqt, k1, k2, u_tab, vt_tab)
    y_s = y_s.reshape(SB, ST, D_MODEL)
    s_sr, s_si = _cols_to_state(sfin_s)
    s_wk = jnp.concatenate([ck, k_s2.reshape(SB, ST, SWA_KV_WIDTH)], axis=1)[:, -w:]
    s_wv = jnp.concatenate([cv, v_s2.reshape(SB, ST, SWA_KV_WIDTH)], axis=1)[:, -w:]
    s_wk = s_wk.reshape(SB, w, SWA_KV_HEADS, HEAD_DIM)
    s_wv = s_wv.reshape(SB, w, SWA_KV_HEADS, HEAD_DIM)

    st = lambda a: a[None]
    return (y_p, y_s, st(p_sr), st(p_si), st(p_wk), st(p_wv), st(p_mk), st(p_mv),
            st(s_sr), st(s_si), st(s_wk), st(s_wv))
```

```python
import functools
import math

import jax
import jax.numpy as jnp
import numpy as np
from jax import lax
from jax.experimental import pallas as pl
from jax.experimental.pallas import tpu as pltpu

F32 = jnp.float32
BF16 = jnp.bfloat16

D_MODEL = 1024
HEAD_DIM = 64
EPS = 1e-6
ROPE_THETA = 10000.0
PAST_LEN = 8192
SSM_WIDTH = 512
SSM_GROUP = 16
SSM_GROUPS = 32
SSM_STATE = 64
SSM_HALF_GROUPS = SSM_GROUPS // 2
SSM_HALF_STATE = SSM_HALF_GROUPS * SSM_STATE
SSM_COLS = 2 * 2 * SSM_HALF_STATE
SWA_Q_HEADS = 4
SWA_KV_HEADS = 2
SWA_REP = 2
SWA_WIDTH = 256
SWA_KV_WIDTH = 128
WINDOW = 128
MEM_TOKENS = 256
MEM_HEADS = 4
MEM_WIDTH = 256
IN_WIDTH = 1280
PEER_HEADS = 8
PEER_KEYS = 128
PEER_EXPERTS = PEER_KEYS * PEER_KEYS
PEER_TOPK = 16
PEER_HALF = 128

LANES = 128
VMEM_LIMIT = 60 * 1024 * 1024

TOKEN_TILE = 512
PEER_EXPERT_BLOCK = 2048
PEER_GATE_KEYS = 4
S5_TIME_TILE = 64

_CAND_COUNT = [PEER_TOPK // (a + 1) for a in range(PEER_TOPK)]
_CAND_ROW0 = [0, 16, 24, 32, 36, 40, 42, 44, 48, 49, 50, 51, 52, 53, 54, 55]
_CAND_ROWS = 56


def _cparams(sem):
    return pltpu.CompilerParams(dimension_semantics=sem, vmem_limit_bytes=VMEM_LIMIT)


def _rms(x, g):
    return x * lax.rsqrt(jnp.mean(x * x, axis=-1, keepdims=True) + EPS) * g


def _gelu(x):
    return 0.5 * x * (1.0 + lax.erf(x * math.sqrt(0.5)))


def _dot(a, b):
    return jnp.dot(a, b, preferred_element_type=F32)


def _dot_nt(a, b):
    return lax.dot_general(a, b, (((1,), (1,)), ((), ())), preferred_element_type=F32)


def _div(x, n):
    return x >> (n.bit_length() - 1) if n & (n - 1) == 0 else x // n


def _mod(x, n):
    return x & (n - 1) if n & (n - 1) == 0 else x % n


def _head_rms(x, ones_bd, g):
    sq = x * x
    hi = sq.astype(BF16)
    lo = (sq - hi.astype(F32)).astype(BF16)
    ms = (_dot(hi, ones_bd) + _dot(lo, ones_bd)) * np.float32(1.0 / HEAD_DIM)
    return x * lax.rsqrt(ms + EPS) * g


def _rope(x, cos, sin_signed):
    w = x.shape[-1]
    lane = lax.broadcasted_iota(jnp.int32, x.shape, 1)
    first_half = _mod(lane, HEAD_DIM) < (HEAD_DIM // 2)
    partner = jnp.where(first_half, pltpu.roll(x, w - HEAD_DIM // 2, 1),
                        pltpu.roll(x, HEAD_DIM // 2, 1))
    return x * cos + partner * sin_signed


def _in_proj_kernel(x_ref, g1_ref, win_ref, ones_ref, gq_ref, gk_ref, gm_ref, cos_ref, sin_ref,
                    u_ref, q_ref, k_ref, v_ref, qm_ref):
    x = x_ref[0]
    xn = _rms(x, g1_ref[...])
    proj = _dot(xn.astype(BF16), win_ref[...])
    u_ref[...] = proj[:, :SSM_WIDTH]
    q = proj[:, 512:768]
    k = proj[:, 768:896]
    v_ref[0] = proj[:, 896:1024]
    qm = proj[:, 1024:1280]
    ones = ones_ref[...]
    cos = cos_ref[...]
    sin = sin_ref[...]
    q_ref[0] = _rope(_head_rms(q, ones, gq_ref[...]), cos, sin)
    k_ref[0] = _rope(_head_rms(k, ones[:SWA_KV_WIDTH, :SWA_KV_WIDTH], gk_ref[...]),
                     cos[:, :SWA_KV_WIDTH], sin[:, :SWA_KV_WIDTH])
    qm_ref[0] = _head_rms(qm, ones, gm_ref[...])


def _in_proj(x, cos, sin, g1, win, ones, gq, gk, gm):
    B, T, _ = x.shape
    tt = min(TOKEN_TILE, T)
    grid = (B, T // tt)
    full = lambda shape: pl.BlockSpec(shape, lambda b, t: (0,) * len(shape))
    return pl.pallas_call(
        _in_proj_kernel,
        grid=grid,
        in_specs=[
            pl.BlockSpec((1, tt, D_MODEL), lambda b, t: (b, t, 0)),
            full((1, D_MODEL)), full((D_MODEL, IN_WIDTH)), full((SWA_WIDTH, SWA_WIDTH)),
            full((1, SWA_WIDTH)), full((1, SWA_KV_WIDTH)), full((1, MEM_WIDTH)),
            pl.BlockSpec((tt, SWA_WIDTH), lambda b, t: (t, 0)),
            pl.BlockSpec((tt, SWA_WIDTH), lambda b, t: (t, 0)),
        ],
        out_specs=[
            pl.BlockSpec((tt, SSM_WIDTH), lambda b, t: (t, b)),
            pl.BlockSpec((1, tt, SWA_WIDTH), lambda b, t: (b, t, 0)),
            pl.BlockSpec((1, tt, SWA_KV_WIDTH), lambda b, t: (b, t, 0)),
            pl.BlockSpec((1, tt, SWA_KV_WIDTH), lambda b, t: (b, t, 0)),
            pl.BlockSpec((1, tt, MEM_WIDTH), lambda b, t: (b, t, 0)),
        ],
        out_shape=[
            jax.ShapeDtypeStruct((T, B * SSM_WIDTH), F32),
            jax.ShapeDtypeStruct((B, T, SWA_WIDTH), F32),
            jax.ShapeDtypeStruct((B, T, SWA_KV_WIDTH), F32),
            jax.ShapeDtypeStruct((B, T, SWA_KV_WIDTH), F32),
            jax.ShapeDtypeStruct((B, T, MEM_WIDTH), F32),
        ],
        compiler_params=_cparams(("parallel", "parallel")),
        name="in_proj",
    )(x, g1, win, ones, gq, gk, gm, cos, sin)


def _s5_kernel(u_ref, s0_ref, lam_ref, bmat_ref, cmat_ref, d_ref, wglu_ref, bglu_ref,
               o_ref, sfin_ref, s_scr, carry_scr, *, bt, tt):
    @pl.when(pl.program_id(0) == 0)
    def _():
        carry_scr[...] = s0_ref[...]

    u = u_ref[...]
    ub = u.astype(BF16)
    hw = 2 * SSM_HALF_STATE
    for j in range(2):
        s_scr[:, j * hw:(j + 1) * hw] = _dot(ub[:, j * 256:(j + 1) * 256], bmat_ref[j])

    def step(t, carry):
        r0 = pl.multiple_of(t * bt, bt)
        for j in range(2):
            c_re = pl.ds(j * hw, SSM_HALF_STATE)
            c_im = pl.ds(j * hw + SSM_HALF_STATE, SSM_HALF_STATE)
            p_re = carry_scr[:, c_re]
            p_im = carry_scr[:, c_im]
            l_re = lam_ref[:, c_re]
            l_im = lam_ref[:, c_im]
            n_re = l_re * p_re - l_im * p_im + s_scr[pl.ds(r0, bt), c_re]
            n_im = l_re * p_im + l_im * p_re + s_scr[pl.ds(r0, bt), c_im]
            s_scr[pl.ds(r0, bt), c_re] = n_re
            s_scr[pl.ds(r0, bt), c_im] = n_im
            carry_scr[:, c_re] = n_re
            carry_scr[:, c_im] = n_im
        return carry

    lax.fori_loop(0, tt, step, 0)
    sfin_ref[...] = carry_scr[...]

    ys = [_dot(s_scr[:, j * hw:(j + 1) * hw].astype(BF16), cmat_ref[j]) for j in range(2)]
    y = jnp.concatenate(ys, axis=-1) + d_ref[...] * u
    y = _gelu(y)
    z = _dot(y.astype(BF16), wglu_ref[...]) + bglu_ref[...]
    o_ref[...] = y * jax.nn.sigmoid(z)


def _s5(u_tm, s0, lam, bmat, cmat, d, wglu, bglu, *, bt, tt):
    rows = u_tm.shape[0]
    nt = rows // (bt * tt)
    full = lambda shape: pl.BlockSpec(shape, lambda t: (0,) * len(shape))
    return pl.pallas_call(
        functools.partial(_s5_kernel, bt=bt, tt=tt),
        grid=(nt,),
        in_specs=[
            pl.BlockSpec((bt * tt, SSM_WIDTH), lambda t: (t, 0)),
            full((bt, SSM_COLS)), full((1, SSM_COLS)),
            full((2, 256, 2 * SSM_HALF_STATE)), full((2, 2 * SSM_HALF_STATE, 256)),
            full((1, SSM_WIDTH)), full((SSM_WIDTH, SSM_WIDTH)), full((1, SSM_WIDTH)),
        ],
        out_specs=[
            pl.BlockSpec((bt * tt, SSM_WIDTH), lambda t: (t, 0)),
            full((bt, SSM_COLS)),
        ],
        out_shape=[
            jax.ShapeDtypeStruct((rows, SSM_WIDTH), F32),
            jax.ShapeDtypeStruct((bt, SSM_COLS), F32),
        ],
        scratch_shapes=[pltpu.VMEM((bt * tt, SSM_COLS), F32), pltpu.VMEM((bt, SSM_COLS), F32)],
        compiler_params=_cparams(("arbitrary",)),
        name="s5_mixer",
    )(u_tm, s0, lam, bmat, cmat, d, wglu, bglu)


def _mem_kv_kernel(m_ref, g_ref, w_ref, ones_ref, gk_ref, k_ref, v_ref):
    xn = _rms(m_ref[...], g_ref[...])
    kv = _dot(xn.astype(BF16), w_ref[...])
    k_ref[...] = _head_rms(kv[:, :MEM_WIDTH], ones_ref[...], gk_ref[...])
    v_ref[...] = kv[:, MEM_WIDTH:]


def _mem_kv(mem_rows, g, w, ones, gk):
    rows = mem_rows.shape[0]
    tt = min(TOKEN_TILE, rows)
    full = lambda shape: pl.BlockSpec(shape, lambda t: (0,) * len(shape))
    return pl.pallas_call(
        _mem_kv_kernel,
        grid=(rows // tt,),
        in_specs=[pl.BlockSpec((tt, D_MODEL), lambda t: (t, 0)), full((1, D_MODEL)),
                  full((D_MODEL, 2 * MEM_WIDTH)), full((MEM_WIDTH, MEM_WIDTH)), full((1, MEM_WIDTH))],
        out_specs=[pl.BlockSpec((tt, MEM_WIDTH), lambda t: (t, 0))] * 2,
        out_shape=[jax.ShapeDtypeStruct((rows, MEM_WIDTH), F32)] * 2,
        compiler_params=_cparams(("parallel",)),
        name="mem_kv",
    )(mem_rows, g, w, ones, gk)


def _softmax_pv(s, v_b, sink=None):
    m = jnp.max(s, axis=-1, keepdims=True)
    if sink is not None:
        m = jnp.maximum(m, sink)
    p = jnp.exp(s - m)
    den = jnp.sum(p, axis=-1, keepdims=True)
    if sink is not None:
        den = den + jnp.exp(sink - m)
    return _dot((p / den).astype(BF16), v_b)


def _attn_prompt_kernel(sinks_ref, q_ref, kp_ref, kc_ref, vp_ref, vc_ref, qm_ref, mk_ref, mv_ref,
                        osw_ref, omem_ref):
    nb = pl.program_id(1)
    scale = np.float32(HEAD_DIM ** -0.5)
    q = q_ref[0].astype(BF16)
    kk = jnp.concatenate([kp_ref[0], kc_ref[0]], axis=0).astype(BF16)
    vv = jnp.concatenate([vp_ref[0], vc_ref[0]], axis=0).astype(BF16)
    qi = lax.broadcasted_iota(jnp.int32, (WINDOW, 2 * WINDOW), 0)
    ki = lax.broadcasted_iota(jnp.int32, (WINDOW, 2 * WINDOW), 1) - WINDOW
    diff = qi - ki
    valid = (diff >= 0) & (diff < WINDOW) & (nb * WINDOW + ki >= 0)
    outs = []
    for hq in range(SWA_Q_HEADS):
        kv = hq // SWA_REP
        s = _dot_nt(q[:, hq * HEAD_DIM:(hq + 1) * HEAD_DIM],
                    kk[:, kv * HEAD_DIM:(kv + 1) * HEAD_DIM]) * scale
        s = jnp.where(valid, s, -jnp.inf)
        outs.append(_softmax_pv(s, vv[:, kv * HEAD_DIM:(kv + 1) * HEAD_DIM], sinks_ref[hq]))
    osw_ref[0] = jnp.concatenate(outs, axis=-1)

    qm = qm_ref[0].astype(BF16)
    mk = mk_ref[0].astype(BF16)
    mv = mv_ref[0].astype(BF16)
    outs = []
    for h in range(MEM_HEADS):
        sl = slice(h * HEAD_DIM, (h + 1) * HEAD_DIM)
        s = _dot_nt(qm[:, sl], mk[:, sl]) * scale
        outs.append(_softmax_pv(s, mv[:, sl]))
    omem_ref[0] = jnp.concatenate(outs, axis=-1)


def _attn_prompt(sinks, q, k, v, qm, mk, mv):
    B, T, _ = q.shape
    nb = T // WINDOW
    blk = lambda w: pl.BlockSpec((1, WINDOW, w), lambda b, n: (b, n, 0))
    prev = lambda w: pl.BlockSpec((1, WINDOW, w), lambda b, n: (b, jnp.maximum(n - 1, 0), 0))
    memb = pl.BlockSpec((1, MEM_TOKENS, MEM_WIDTH), lambda b, n: (b, 0, 0))
    return pl.pallas_call(
        _attn_prompt_kernel,
        grid=(B, nb),
        in_specs=[pl.BlockSpec(memory_space=pltpu.SMEM),
                  blk(SWA_WIDTH), prev(SWA_KV_WIDTH), blk(SWA_KV_WIDTH), prev(SWA_KV_WIDTH),
                  blk(SWA_KV_WIDTH), blk(MEM_WIDTH), memb, memb],
        out_specs=[blk(SWA_WIDTH), blk(MEM_WIDTH)],
        out_shape=[jax.ShapeDtypeStruct((B, T, SWA_WIDTH), F32),
                   jax.ShapeDtypeStruct((B, T, MEM_WIDTH), F32)],
        compiler_params=_cparams(("parallel", "parallel")),
        name="attn_prompt",
    )(sinks, q, k, k, v, v, qm, mk, mv)


def _attn_sample_kernel(sinks_ref, q_ref, kn_ref, vn_ref, ck_ref, cv_ref, qm_ref, mk_ref, mv_ref,
                        osw_ref, omem_ref, *, bb, ts, start):
    scale = np.float32(HEAD_DIM ** -0.5)
    w = ck_ref.shape[1]
    nq = bb * ts
    q = q_ref[...].astype(BF16)
    kn = kn_ref[...].astype(BF16)
    vn = vn_ref[...].astype(BF16)
    ck = ck_ref[...].reshape(bb * w, SWA_KV_WIDTH).astype(BF16)
    cv = cv_ref[...].reshape(bb * w, SWA_KV_WIDTH).astype(BF16)

    rq = lax.broadcasted_iota(jnp.int32, (nq, bb * w), 0)
    cc = lax.broadcasted_iota(jnp.int32, (nq, bb * w), 1)
    qpos = start + _mod(rq, ts)
    kpos = start - w + _mod(cc, w)
    diff = qpos - kpos
    valid_c = (_div(rq, ts) == _div(cc, w)) & (diff >= 0) & (diff < WINDOW) & (kpos >= 0)
    rq = lax.broadcasted_iota(jnp.int32, (nq, nq), 0)
    cn = lax.broadcasted_iota(jnp.int32, (nq, nq), 1)
    diff = _mod(rq, ts) - _mod(cn, ts)
    valid_n = (_div(rq, ts) == _div(cn, ts)) & (diff >= 0) & (diff < WINDOW)

    outs = []
    for hq in range(SWA_Q_HEADS):
        kv = hq // SWA_REP
        qs = q[:, hq * HEAD_DIM:(hq + 1) * HEAD_DIM]
        ks = slice(kv * HEAD_DIM, (kv + 1) * HEAD_DIM)
        s_c = jnp.where(valid_c, _dot_nt(qs, ck[:, ks]) * scale, -jnp.inf)
        s_n = jnp.where(valid_n, _dot_nt(qs, kn[:, ks]) * scale, -jnp.inf)
        sink = sinks_ref[hq]
        m = jnp.maximum(jnp.maximum(jnp.max(s_c, axis=-1, keepdims=True),
                                    jnp.max(s_n, axis=-1, keepdims=True)), sink)
        p_c = jnp.exp(s_c - m)
        p_n = jnp.exp(s_n - m)
        den = (jnp.sum(p_c, axis=-1, keepdims=True) + jnp.sum(p_n, axis=-1, keepdims=True)
               + jnp.exp(sink - m))
        outs.append(_dot((p_c / den).astype(BF16), cv[:, ks]) + _dot((p_n / den).astype(BF16), vn[:, ks]))
    osw_ref[...] = jnp.concatenate(outs, axis=-1)

    qm = qm_ref[...].astype(BF16)
    mk = mk_ref[...].reshape(bb * MEM_TOKENS, MEM_WIDTH).astype(BF16)
    mv = mv_ref[...].reshape(bb * MEM_TOKENS, MEM_WIDTH).astype(BF16)
    rq = lax.broadcasted_iota(jnp.int32, (nq, bb * MEM_TOKENS), 0)
    cm = lax.broadcasted_iota(jnp.int32, (nq, bb * MEM_TOKENS), 1)
    valid_m = _div(rq, ts) == _div(cm, MEM_TOKENS)
    outs = []
    for h in range(MEM_HEADS):
        sl = slice(h * HEAD_DIM, (h + 1) * HEAD_DIM)
        s = jnp.where(valid_m, _dot_nt(qm[:, sl], mk[:, sl]) * scale, -jnp.inf)
        outs.append(_softmax_pv(s, mv[:, sl]))
    omem_ref[...] = jnp.concatenate(outs, axis=-1)


def _attn_sample(sinks, q, kn, vn, ck, cv, qm, cmk, cmv, *, ts, start):
    B, w, _ = ck.shape
    bb = 8
    rows = lambda wd: pl.BlockSpec((bb * ts, wd), lambda i: (i, 0))
    blk3 = lambda n, wd: pl.BlockSpec((bb, n, wd), lambda i: (i, 0, 0))
    return pl.pallas_call(
        functools.partial(_attn_sample_kernel, bb=bb, ts=ts, start=start),
        grid=(B // bb,),
        in_specs=[pl.BlockSpec(memory_space=pltpu.SMEM),
                  rows(SWA_WIDTH), rows(SWA_KV_WIDTH), rows(SWA_KV_WIDTH),
                  blk3(w, SWA_KV_WIDTH), blk3(w, SWA_KV_WIDTH), rows(MEM_WIDTH),
                  blk3(MEM_TOKENS, MEM_WIDTH), blk3(MEM_TOKENS, MEM_WIDTH)],
        out_specs=[rows(SWA_WIDTH), rows(MEM_WIDTH)],
        out_shape=[jax.ShapeDtypeStruct((B * ts, SWA_WIDTH), F32),
                   jax.ShapeDtypeStruct((B * ts, MEM_WIDTH), F32)],
        compiler_params=_cparams(("parallel",)),
        name="attn_sample",
    )(sinks, q, kn, vn, ck, cv, qm, cmk, cmv)


def _out_proj_kernel(x_ref, ossm_ref, osw_ref, omem_ref, wo_ref, g2_ref, h_ref, xn_ref):
    h = x_ref[0]
    h = h + _dot(ossm_ref[...].astype(BF16), wo_ref[0:512, :])
    h = h + _dot(osw_ref[0].astype(BF16), wo_ref[512:768, :])
    h = h + _dot(omem_ref[0].astype(BF16), wo_ref[768:1024, :])
    h_ref[0] = h
    xn_ref[0] = _rms(h, g2_ref[...]).astype(BF16)


def _out_proj(x, ossm, osw, omem, wo, g2):
    B, T, _ = x.shape
    tt = min(TOKEN_TILE, T)
    full = lambda shape: pl.BlockSpec(shape, lambda b, t: (0,) * len(shape))
    blk = lambda wd: pl.BlockSpec((1, tt, wd), lambda b, t: (b, t, 0))
    return pl.pallas_call(
        _out_proj_kernel,
        grid=(B, T // tt),
        in_specs=[blk(D_MODEL), pl.BlockSpec((tt, SSM_WIDTH), lambda b, t: (t, b)),
                  blk(SWA_WIDTH), blk(MEM_WIDTH), full((D_MODEL, D_MODEL)), full((1, D_MODEL))],
        out_specs=[blk(D_MODEL), blk(D_MODEL)],
        out_shape=[jax.ShapeDtypeStruct((B, T, D_MODEL), F32),
                   jax.ShapeDtypeStruct((B, T, D_MODEL), BF16)],
        compiler_params=_cparams(("parallel", "parallel")),
        name="out_proj",
    )(x, ossm, osw, omem, wo, g2)


def _top16(s, iota):
    work = s
    rank = jnp.full(s.shape, float(PEER_TOPK), F32)
    vals = []
    for a in range(PEER_TOPK):
        m = jnp.max(work, axis=0, keepdims=True)
        idx = jnp.min(jnp.where(work == m, iota, float(PEER_KEYS)), axis=0, keepdims=True)
        sel = iota == idx
        rank = jnp.where(sel, float(a), rank)
        work = jnp.where(sel, -jnp.inf, work)
        vals.append(m)
    return vals, rank


def _sort_pairs(n):
    pairs = []
    t = max(1, (n - 1).bit_length())
    p = 1 << (t - 1)
    while p > 0:
        q, r, d = 1 << (t - 1), 0, p
        while d > 0:
            pairs += [(i, i + d) for i in range(n - d) if (i & p) == r]
            d, q, r = q - p, q >> 1, p
        p >>= 1
    return pairs


_SORT16_PAIRS = _sort_pairs(PEER_TOPK)


def _cmp_exchange(xs, i, j):
    xs[i], xs[j] = jnp.maximum(xs[i], xs[j]), jnp.minimum(xs[i], xs[j])


def _top16_sorted(s):
    n = PEER_TOPK
    xs = [s[8 * k:8 * k + 8, :] for k in range(n)]
    for i, j in _SORT16_PAIRS:
        _cmp_exchange(xs, i, j)
    for shift in (4, 2, 1):
        other = [pltpu.roll(x, shift, 0) for x in xs]
        xs = [jnp.maximum(xs[k], other[n - 1 - k]) for k in range(n)]
        for d in (8, 4, 2, 1):
            for k in range(n):
                if k & d == 0:
                    _cmp_exchange(xs, k, k + d)
    return xs


def _joint_counts(v1, v2, cand_scr):
    L = v1[0].shape[1]
    cand_scr[...] = jnp.full((_CAND_ROWS, L), -jnp.inf, F32)
    for a in range(PEER_TOPK):
        for b in range(_CAND_COUNT[a]):
            cand_scr[pl.ds(_CAND_ROW0[a] + b, 1), :] = v1[a] + v2[b]
    work = cand_scr[...]
    iota_c = lax.broadcasted_iota(jnp.int32, (_CAND_ROWS, L), 0).astype(F32)
    taken = jnp.zeros((_CAND_ROWS, L), F32)
    m0 = None
    z = None
    for k in range(PEER_TOPK):
        m = jnp.max(work, axis=0, keepdims=True)
        idx = jnp.min(jnp.where(work == m, iota_c, float(_CAND_ROWS)), axis=0, keepdims=True)
        sel = iota_c == idx
        taken = jnp.where(sel, 1.0, taken)
        work = jnp.where(sel, -jnp.inf, work)
        if k == 0:
            m0 = m
            z = jnp.ones_like(m)
        else:
            z = z + jnp.exp(m - m0)

    cnt = []
    for a in range(PEER_TOPK):
        lo, hi = _CAND_ROW0[a], _CAND_ROW0[a] + _CAND_COUNT[a]
        if hi - lo == 1:
            cnt.append(taken[lo:hi])
            continue
        t0, t1 = (lo // 8) * 8, -(-hi // 8) * 8
        part = taken[t0:t1]
        if (lo, hi) != (t0, t1):
            rows = iota_c[t0:t1]
            part = jnp.where((rows >= float(lo)) & (rows < float(hi)), part, 0.0)
        cnt.append(jnp.sum(part, axis=0, keepdims=True))
    return cnt, z


def _route_tile(s1, s2, cand_scr, n1_out, r2_out):
    n = PEER_TOPK
    t1 = _top16_sorted(s1)
    t2 = _top16_sorted(s2)
    v1 = [t[0:1, :] for t in t1]
    v2 = [t[0:1, :] for t in t2]
    cnt, z = _joint_counts(v1, v2, cand_scr)

    bad = jnp.zeros((8, s1.shape[1]), F32)
    for s, t in ((s1, t1), (s2, t2)):
        gap = t[0] - t[1]
        for a in range(1, n - 1):
            gap = jnp.minimum(gap, t[a] - t[a + 1])
        at_least = jnp.zeros((8, s.shape[1]), F32)
        for k in range(n):
            at_least = at_least + jnp.where(s[8 * k:8 * k + 8, :] >= t[n - 1], 1.0, 0.0)
        at_least = jnp.sum(at_least, axis=0, keepdims=True)
        bad = jnp.where((gap <= 0.0) | (at_least != float(n)), 1.0, bad)
    has_ties = jnp.max(bad) > 0.0

    @pl.when(jnp.logical_not(has_ties))
    def _by_value():
        cnt_b = [jnp.broadcast_to(c, (8, c.shape[1])) for c in cnt]
        ranks = []
        for k in range(n):
            x1 = s1[8 * k:8 * k + 8, :]
            x2 = s2[8 * k:8 * k + 8, :]
            n1 = cnt_b[0]
            r2 = jnp.zeros_like(x2)
            for a in range(n):
                n1 = jnp.where(t1[a] > x1, cnt_b[a + 1] if a + 1 < n else 0.0, n1)
                r2 = jnp.where(t2[a] > x2, float(a + 1), r2)
            n1_out[8 * k:8 * k + 8, :] = n1
            ranks.append(r2)
        r2_out[...] = jnp.concatenate(ranks, axis=0).astype(r2_out.dtype)

    @pl.when(has_ties)
    def _by_index():
        iota = lax.broadcasted_iota(jnp.int32, s1.shape, 0).astype(F32)
        _, r1 = _top16(s1, iota)
        _, r2 = _top16(s2, iota)
        n1 = jnp.zeros(s1.shape, F32)
        for a in range(n):
            n1 = jnp.where(r1 == float(a), cnt[a], n1)
        n1_out[...] = n1
        r2_out[...] = r2.astype(r2_out.dtype)

    c1 = jnp.exp(s1 - v1[0]) / z
    e2 = jnp.exp(s2 - v2[0])
    return c1, e2


def _peer_kernel(xn_ref, h_ref, wqt_ref, k1_ref, k2_ref, u_ref, vt_ref, y_ref,
                 n1_scr, c1_scr, s2_scr, r2_scr, e2_scr, acc_scr, a_scr, w_scr,
                 q_scr, xt_scr, cand_scr, *, n_steps):
    g = pl.program_id(1)
    ng = n_steps
    tt = xn_ref.shape[0]
    eb = u_ref.shape[0]
    keys_per_block = eb // PEER_KEYS
    n_lane_tiles = tt // LANES

    @pl.when(g == 0)
    def _route():
        xt_scr[...] = xn_ref[...].T
        xt = xt_scr[...]
        q_scr[...] = _dot(wqt_ref[...], xt).astype(BF16)
        for h in range(PEER_HEADS):
            for side, dst in enumerate((n1_scr, s2_scr)):
                k_ref = (k1_ref, k2_ref)[side]
                r0 = h * 2 * PEER_HALF + side * PEER_HALF
                s = _dot(k_ref[...], q_scr[r0:r0 + PEER_HALF, :])
                for c in range(n_lane_tiles):
                    dst[h, c] = s[:, c * LANES:(c + 1) * LANES]

        def body(i, carry):
            h = i // n_lane_tiles
            c = i % n_lane_tiles
            c1, e2 = _route_tile(n1_scr[h, c], s2_scr[h, c], cand_scr, n1_scr.at[h, c], r2_scr.at[h, c])
            c1_scr[h, c] = c1
            e2_scr[h, c] = e2.astype(BF16)
            return carry

        lax.fori_loop(0, PEER_HEADS * n_lane_tiles, body, 0)
        acc_scr[...] = jnp.zeros_like(acc_scr)

    def gate(a_scr, w_scr, block):
        zero = jnp.zeros((), BF16)
        group = PEER_GATE_KEYS
        for k0 in range(0, keys_per_block, group):
            def lane_tile(c, carry, k0=k0):
                lanes = pl.ds(pl.multiple_of(c * LANES, LANES), LANES)
                gts = [None] * group
                for h in range(PEER_HEADS):
                    r2 = r2_scr[h, c].reshape(PEER_KEYS // 16, 16, LANES)
                    e2 = e2_scr[h, c].reshape(PEER_KEYS // 16, 16, LANES)
                    for ii in range(group):
                        row = pl.ds(block * keys_per_block + k0 + ii, 1)
                        n_b = jnp.broadcast_to(n1_scr[h, c, row, :], (16, LANES)).astype(BF16)[None]
                        c_b = jnp.broadcast_to(c1_scr[h, c, row, :], (16, LANES)).astype(BF16)[None]
                        term = jnp.where(r2 < n_b, e2, zero) * c_b
                        gts[ii] = term if h == 0 else gts[ii] + term
                for ii in range(group):
                    r0 = (k0 + ii) * PEER_KEYS
                    rows = slice(r0, r0 + PEER_KEYS)
                    act = _gelu(a_scr[rows, lanes].astype(BF16))
                    w_scr[rows, lanes] = gts[ii].reshape(PEER_KEYS, LANES) * act
                return carry

            lax.fori_loop(0, n_lane_tiles, lane_tile, 0)

    @pl.when(g < ng)
    def _scores():
        a_scr[...] = _dot(u_ref[...], xt_scr[...])

    @pl.when(g > 0)
    def _v_product():
        acc_scr[...] += _dot(vt_ref[...], w_scr[...])

    @pl.when(g < ng)
    def _weights():
        gate(a_scr, w_scr, g)

    @pl.when(g == ng)
    def _fin():
        y_ref[...] = h_ref[...] + acc_scr[...].T


def _peer(xn, h, wqt, k1, k2, u_tab, vt_tab):
    n = xn.shape[0]
    tt = min(TOKEN_TILE, n)
    eb = PEER_EXPERT_BLOCK
    ng = PEER_EXPERTS // eb
    full = lambda shape: pl.BlockSpec(shape, lambda i, g: (0,) * len(shape), pipeline_mode=pl.Buffered(1))
    tok = pl.BlockSpec((tt, D_MODEL), lambda i, g: (i, 0))
    head_f32 = pltpu.VMEM((PEER_HEADS, tt // LANES, PEER_KEYS, LANES), F32)
    head_bf16 = pltpu.VMEM((PEER_HEADS, tt // LANES, PEER_KEYS, LANES), BF16)
    return pl.pallas_call(
        functools.partial(_peer_kernel, n_steps=ng),
        grid=(n // tt, ng + 1),
        in_specs=[tok, tok, full((2 * PEER_HEADS * PEER_HALF, D_MODEL)),
                  full((PEER_KEYS, PEER_HALF)), full((PEER_KEYS, PEER_HALF)),
                  pl.BlockSpec((eb, D_MODEL), lambda i, g: (jnp.minimum(g, ng - 1), 0)),
                  pl.BlockSpec((D_MODEL, eb), lambda i, g: (0, jnp.maximum(g - 1, 0)))],
        out_specs=tok,
        out_shape=jax.ShapeDtypeStruct((n, D_MODEL), F32),
        scratch_shapes=[head_f32, head_f32, head_f32, head_bf16, head_bf16,
                        pltpu.VMEM((D_MODEL, tt), F32),
                        pltpu.VMEM((eb, tt), F32), pltpu.VMEM((eb, tt), BF16),
                        pltpu.VMEM((2 * PEER_HEADS * PEER_HALF, tt), BF16),
                        pltpu.VMEM((D_MODEL, tt), BF16),
                        pltpu.VMEM((_CAND_ROWS, LANES), F32)],
        compiler_params=_cparams(("parallel", "arbitrary")),
        name="peer",
    )(xn, h, wqt, k1, k2, u_tab, vt_tab)


def _rope_tables(pos):
    half = HEAD_DIM // 2
    inv = ROPE_THETA ** (-jnp.arange(half, dtype=F32) / half)
    ang = pos.astype(F32)[:, None] * inv[None, :]
    cos = jnp.cos(ang)
    sin = jnp.sin(ang)
    cos = jnp.tile(jnp.concatenate([cos, cos], axis=-1), (1, SWA_Q_HEADS))
    sin = jnp.tile(jnp.concatenate([-sin, sin], axis=-1), (1, SWA_Q_HEADS))
    return cos, sin


def _ssm_params(log_dt, a_re, a_im, b_re, b_im, c_re, c_im):
    dt = jnp.exp(log_dt)
    mag = jnp.exp(a_re * dt)
    lam_re = mag * jnp.cos(a_im * dt)
    lam_im = mag * jnp.sin(a_im * dt)
    den = a_re * a_re + a_im * a_im
    z_re = ((lam_re - 1.0) * a_re + lam_im * a_im) / den
    z_im = (lam_im * a_re - (lam_re - 1.0) * a_im) / den
    bb_re = z_re[..., None] * b_re - z_im[..., None] * b_im
    bb_im = z_re[..., None] * b_im + z_im[..., None] * b_re
    hg = SSM_HALF_GROUPS
    eye = jnp.eye(hg, dtype=F32)
    bb = jnp.stack([bb_re, bb_im]).reshape(2, 2, hg, SSM_STATE, SSM_GROUP)
    bmat = jnp.einsum('rjgnc,gh->jgcrhn', bb, eye).reshape(2, hg * SSM_GROUP, 2 * SSM_HALF_STATE)
    cc = jnp.stack([c_re, -c_im]).reshape(2, 2, hg, SSM_GROUP, SSM_STATE)
    cmat = jnp.einsum('rjgcn,gh->jrgnhc', cc, eye).reshape(2, 2 * SSM_HALF_STATE, hg * SSM_GROUP)
    lam = jnp.stack([lam_re.reshape(2, SSM_HALF_STATE), lam_im.reshape(2, SSM_HALF_STATE)], axis=1)
    return lam.reshape(1, SSM_COLS), bmat.astype(BF16), cmat.astype(BF16)


def _state_to_cols(s_re, s_im):
    b = s_re.shape[0]
    st = jnp.stack([s_re.reshape(b, 2, SSM_HALF_STATE), s_im.reshape(b, 2, SSM_HALF_STATE)], axis=2)
    return st.reshape(b, SSM_COLS)


def _cols_to_state(cols):
    b = cols.shape[0]
    st = cols.reshape(b, 2, 2, SSM_HALF_STATE)
    return (st[:, :, 0].reshape(b, SSM_GROUPS, SSM_STATE), st[:, :, 1].reshape(b, SSM_GROUPS, SSM_STATE))


def kernel(x_prompt, x_sample, state_ssm_re, state_ssm_im, cache_win_k, cache_win_v, cache_mem_k, cache_mem_v, mem_prompt, norm1_g, w_in, ssm_log_dt, ssm_a_re, ssm_a_im, ssm_b_re, ssm_b_im, ssm_c_re, ssm_c_im, ssm_d, w_glu, b_glu, swa_q_norm, swa_k_norm, swa_sinks, mem_norm_g, w_mem_kv, mem_q_norm, mem_k_norm, w_out, norm2_g, peer_wq, peer_k1, peer_k2, peer_u, peer_v):
    depth = norm1_g.shape[0]
    assert depth == 1
    l = 0
    B, T, _ = x_prompt.shape
    SB, ST, _ = x_sample.shape
    w = cache_win_k.shape[2]

    row = lambda a: a.reshape(1, -1)
    g1 = row(norm1_g[l])
    g2 = row(norm2_g[l])
    win = w_in[l].astype(BF16)
    wo = w_out[l].astype(BF16)
    wglu = w_glu[l].astype(BF16)
    bglu = row(b_glu[l])
    gq = row(jnp.tile(swa_q_norm[l], SWA_Q_HEADS))
    gk = row(jnp.tile(swa_k_norm[l], SWA_KV_HEADS))
    gm = row(jnp.tile(mem_q_norm[l], MEM_HEADS))
    gmk = row(jnp.tile(mem_k_norm[l], MEM_HEADS))
    gmem = row(mem_norm_g[l])
    wkv = w_mem_kv[l].astype(BF16)
    sinks = swa_sinks[l]
    head_id = np.arange(SWA_WIDTH) // HEAD_DIM
    ones = jnp.asarray(head_id[:, None] == head_id[None, :], dtype=BF16)
    lam, bmat, cmat = _ssm_params(ssm_log_dt[l], ssm_a_re[l], ssm_a_im[l], ssm_b_re[l], ssm_b_im[l],
                                  ssm_c_re[l], ssm_c_im[l])
    dskip = row(ssm_d[l])
    wqt = peer_wq[l].astype(BF16).T
    k1 = peer_k1[l].astype(BF16)
    k2 = peer_k2[l].astype(BF16)
    u_tab = peer_u[l].astype(BF16)
    vt_tab = peer_v[l].astype(BF16).T

    cos_p, sin_p = _rope_tables(jnp.arange(T, dtype=jnp.int32))
    u_p, q_p, k_p, v_p, qm_p = _in_proj(x_prompt, cos_p, sin_p, g1, win, ones, gq, gk, gm)
    zeros = jnp.zeros((B, SSM_COLS), F32)
    ossm_p, sfin_p = _s5(u_p.reshape(T * B, SSM_WIDTH), zeros, lam, bmat, cmat, dskip, wglu, bglu,
                         bt=B, tt=S5_TIME_TILE)
    mk, mv = _mem_kv(mem_prompt.reshape(B * MEM_TOKENS, D_MODEL), gmem, wkv, ones, gmk)
    mk = mk.reshape(B, MEM_TOKENS, MEM_WIDTH)
    mv = mv.reshape(B, MEM_TOKENS, MEM_WIDTH)
    osw_p, omem_p = _attn_prompt(sinks, q_p, k_p, v_p, qm_p, mk, mv)
    h_p, xn_p = _out_proj(x_prompt, ossm_p.reshape(T, B * SSM_WIDTH), osw_p, omem_p, wo, g2)
    y_p = _peer(xn_p.reshape(B * T, D_MODEL), h_p.reshape(B * T, D_MODEL), wqt, k1, k2, u_tab, vt_tab)
    y_p = y_p.reshape(B, T, D_MODEL)
    p_sr, p_si = _cols_to_state(sfin_p)
    p_wk = k_p[:, T - w:].reshape(B, w, SWA_KV_HEADS, HEAD_DIM)
    p_wv = v_p[:, T - w:].reshape(B, w, SWA_KV_HEADS, HEAD_DIM)
    p_mk = mk.reshape(B, MEM_TOKENS, MEM_HEADS, HEAD_DIM)
    p_mv = mv.reshape(B, MEM_TOKENS, MEM_HEADS, HEAD_DIM)

    n_s = SB * ST
    pos_s = PAST_LEN + jnp.tile(jnp.arange(ST, dtype=jnp.int32), SB)
    cos_s, sin_s = _rope_tables(pos_s)
    xs = x_sample.reshape(1, n_s, D_MODEL)
    u_s, q_s, k_s, v_s, qm_s = _in_proj(xs, cos_s, sin_s, g1, win, ones, gq, gk, gm)
    u_tm = u_s.reshape(SB, ST, SSM_WIDTH).transpose(1, 0, 2).reshape(n_s, SSM_WIDTH)
    ossm_tm, sfin_s = _s5(u_tm, _state_to_cols(state_ssm_re[l], state_ssm_im[l]), lam, bmat, cmat,
                          dskip, wglu, bglu, bt=SB, tt=ST)
    ossm_s = ossm_tm.reshape(ST, SB, SSM_WIDTH).transpose(1, 0, 2).reshape(n_s, SSM_WIDTH)
    ck = cache_win_k[l].reshape(SB, w, SWA_KV_WIDTH)
    cv = cache_win_v[l].reshape(SB, w, SWA_KV_WIDTH)
    q_s2 = q_s.reshape(n_s, SWA_WIDTH)
    k_s2 = k_s.reshape(n_s, SWA_KV_WIDTH)
    v_s2 = v_s.reshape(n_s, SWA_KV_WIDTH)
    osw_s, omem_s = _attn_sample(sinks, q_s2, k_s2, v_s2, ck, cv, qm_s.reshape(n_s, MEM_WIDTH),
                                 cache_mem_k[l].reshape(SB, MEM_TOKENS, MEM_WIDTH),
                                 cache_mem_v[l].reshape(SB, MEM_TOKENS, MEM_WIDTH),
                                 ts=ST, start=PAST_LEN)
    h_s, xn_s = _out_proj(xs, ossm_s, osw_s.reshape(1, n_s, SWA_WIDTH), omem_s.reshape(1, n_s, MEM_WIDTH),
                          wo, g2)
    y_s = _peer(xn_s.reshape(n_s, D_MODEL), h_s.reshape(n_s, D_MODEL), wqt, k1, k2, u_tab, vt_tab)
    y_s = y_s.reshape(SB, ST, D_MODEL)
    s_sr, s_si = _cols_to_state(sfin_s)
    s_wk = jnp.concatenate([ck, k_s2.reshape(SB, ST, SWA_KV_WIDTH)], axis=1)[:, -w:]
    s_wv = jnp.concatenate([cv, v_s2.reshape(SB, ST, SWA_KV_WIDTH)], axis=1)[:, -w:]
    s_wk = s_wk.reshape(SB, w, SWA_KV_HEADS, HEAD_DIM)
    s_wv = s_wv.reshape(SB, w, SWA_KV_HEADS, HEAD_DIM)

    st = lambda a: a[None]
    return (y_p, y_s, st(p_sr), st(p_si), st(p_wk), st(p_wv), st(p_mk), st(p_mv),
            st(s_sr), st(s_si), st(s_wk), st(s_wv))
```

```python
import functools
import math

import jax
import jax.numpy as jnp
import numpy as np
from jax import lax
from jax.experimental import pallas as pl
from jax.experimental.pallas import tpu as pltpu

F32 = jnp.float32
BF16 = jnp.bfloat16

D_MODEL = 1024
HEAD_DIM = 64
EPS = 1e-6
ROPE_THETA = 10000.0
PAST_LEN = 8192
SSM_WIDTH = 512
SSM_GROUP = 16
SSM_GROUPS = 32
SSM_STATE = 64
SSM_HALF_GROUPS = SSM_GROUPS // 2
SSM_HALF_STATE = SSM_HALF_GROUPS * SSM_STATE
SSM_COLS = 2 * 2 * SSM_HALF_STATE
SWA_Q_HEADS = 4
SWA_KV_HEADS = 2
SWA_REP = 2
SWA_WIDTH = 256
SWA_KV_WIDTH = 128
WINDOW = 128
MEM_TOKENS = 256
MEM_HEADS = 4
MEM_WIDTH = 256
IN_WIDTH = 1280
PEER_HEADS = 8
PEER_KEYS = 128
PEER_EXPERTS = PEER_KEYS * PEER_KEYS
PEER_TOPK = 16
PEER_HALF = 128

LANES = 128
VMEM_LIMIT = 60 * 1024 * 1024

TOKEN_TILE = 512
PEER_EXPERT_BLOCK = 2048
PEER_GATE_KEYS = 4
S5_TIME_TILE = 64

_CAND_COUNT = [PEER_TOPK // (a + 1) for a in range(PEER_TOPK)]
_CAND_ROW0 = [0, 16, 24, 32, 36, 40, 42, 44, 48, 49, 50, 51, 52, 53, 54, 55]
_CAND_ROWS = 56


def _cparams(sem):
    return pltpu.CompilerParams(dimension_semantics=sem, vmem_limit_bytes=VMEM_LIMIT)


def _rms(x, g):
    return x * lax.rsqrt(jnp.mean(x * x, axis=-1, keepdims=True) + EPS) * g


def _gelu(x):
    return 0.5 * x * (1.0 + lax.erf(x * math.sqrt(0.5)))


def _dot(a, b):
    return jnp.dot(a, b, preferred_element_type=F32)


def _dot_nt(a, b):
    return lax.dot_general(a, b, (((1,), (1,)), ((), ())), preferred_element_type=F32)


def _div(x, n):
    return x >> (n.bit_length() - 1) if n & (n - 1) == 0 else x // n


def _mod(x, n):
    return x & (n - 1) if n & (n - 1) == 0 else x % n


def _head_rms(x, ones_bd, g):
    sq = x * x
    hi = sq.astype(BF16)
    lo = (sq - hi.astype(F32)).astype(BF16)
    ms = (_dot(hi, ones_bd) + _dot(lo, ones_bd)) * np.float32(1.0 / HEAD_DIM)
    return x * lax.rsqrt(ms + EPS) * g


def _rope(x, cos, sin_signed):
    w = x.shape[-1]
    lane = lax.broadcasted_iota(jnp.int32, x.shape, 1)
    first_half = _mod(lane, HEAD_DIM) < (HEAD_DIM // 2)
    partner = jnp.where(first_half, pltpu.roll(x, w - HEAD_DIM // 2, 1),
                        pltpu.roll(x, HEAD_DIM // 2, 1))
    return x * cos + partner * sin_signed


def _in_proj_kernel(x_ref, g1_ref, win_ref, ones_ref, gq_ref, gk_ref, gm_ref, cos_ref, sin_ref,
                    u_ref, q_ref, k_ref, v_ref, qm_ref):
    x = x_ref[0]
    xn = _rms(x, g1_ref[...])
    proj = _dot(xn.astype(BF16), win_ref[...])
    u_ref[...] = proj[:, :SSM_WIDTH]
    q = proj[:, 512:768]
    k = proj[:, 768:896]
    v_ref[0] = proj[:, 896:1024]
    qm = proj[:, 1024:1280]
    ones = ones_ref[...]
    cos = cos_ref[...]
    sin = sin_ref[...]
    q_ref[0] = _rope(_head_rms(q, ones, gq_ref[...]), cos, sin)
    k_ref[0] = _rope(_head_rms(k, ones[:SWA_KV_WIDTH, :SWA_KV_WIDTH], gk_ref[...]),
                     cos[:, :SWA_KV_WIDTH], sin[:, :SWA_KV_WIDTH])
    qm_ref[0] = _head_rms(qm, ones, gm_ref[...])


def _in_proj(x, cos, sin, g1, win, ones, gq, gk, gm):
    B, T, _ = x.shape
    tt = min(TOKEN_TILE, T)
    grid = (B, T // tt)
    full = lambda shape: pl.BlockSpec(shape, lambda b, t: (0,) * len(shape))
    return pl.pallas_call(
        _in_proj_kernel,
        grid=grid,
        in_specs=[
            pl.BlockSpec((1, tt, D_MODEL), lambda b, t: (b, t, 0)),
            full((1, D_MODEL)), full((D_MODEL, IN_WIDTH)), full((SWA_WIDTH, SWA_WIDTH)),
            full((1, SWA_WIDTH)), full((1, SWA_KV_WIDTH)), full((1, MEM_WIDTH)),
            pl.BlockSpec((tt, SWA_WIDTH), lambda b, t: (t, 0)),
            pl.BlockSpec((tt, SWA_WIDTH), lambda b, t: (t, 0)),
        ],
        out_specs=[
            pl.BlockSpec((tt, SSM_WIDTH), lambda b, t: (t, b)),
            pl.BlockSpec((1, tt, SWA_WIDTH), lambda b, t: (b, t, 0)),
            pl.BlockSpec((1, tt, SWA_KV_WIDTH), lambda b, t: (b, t, 0)),
            pl.BlockSpec((1, tt, SWA_KV_WIDTH), lambda b, t: (b, t, 0)),
            pl.BlockSpec((1, tt, MEM_WIDTH), lambda b, t: (b, t, 0)),
        ],
        out_shape=[
            jax.ShapeDtypeStruct((T, B * SSM_WIDTH), F32),
            jax.ShapeDtypeStruct((B, T, SWA_WIDTH), F32),
            jax.ShapeDtypeStruct((B, T, SWA_KV_WIDTH), F32),
            jax.ShapeDtypeStruct((B, T, SWA_KV_WIDTH), F32),
            jax.ShapeDtypeStruct((B, T, MEM_WIDTH), F32),
        ],
        compiler_params=_cparams(("parallel", "parallel")),
        name="in_proj",
    )(x, g1, win, ones, gq, gk, gm, cos, sin)


def _s5_kernel(u_ref, s0_ref, lam_ref, bmat_ref, cmat_ref, d_ref, wglu_ref, bglu_ref,
               o_ref, sfin_ref, s_scr, carry_scr, *, bt, tt):
    @pl.when(pl.program_id(0) == 0)
    def _():
        carry_scr[...] = s0_ref[...]

    u = u_ref[...]
    ub = u.astype(BF16)
    hw = 2 * SSM_HALF_STATE
    for j in range(2):
        s_scr[:, j * hw:(j + 1) * hw] = _dot(ub[:, j * 256:(j + 1) * 256], bmat_ref[j])

    def step(t, carry):
        r0 = pl.multiple_of(t * bt, bt)
        for j in range(2):
            c_re = pl.ds(j * hw, SSM_HALF_STATE)
            c_im = pl.ds(j * hw + SSM_HALF_STATE, SSM_HALF_STATE)
            p_re = carry_scr[:, c_re]
            p_im = carry_scr[:, c_im]
            l_re = lam_ref[:, c_re]
            l_im = lam_ref[:, c_im]
            n_re = l_re * p_re - l_im * p_im + s_scr[pl.ds(r0, bt), c_re]
            n_im = l_re * p_im + l_im * p_re + s_scr[pl.ds(r0, bt), c_im]
            s_scr[pl.ds(r0, bt), c_re] = n_re
            s_scr[pl.ds(r0, bt), c_im] = n_im
            carry_scr[:, c_re] = n_re
            carry_scr[:, c_im] = n_im
        return carry

    lax.fori_loop(0, tt, step, 0)
    sfin_ref[...] = carry_scr[...]

    ys = [_dot(s_scr[:, j * hw:(j + 1) * hw].astype(BF16), cmat_ref[j]) for j in range(2)]
    y = jnp.concatenate(ys, axis=-1) + d_ref[...] * u
    y = _gelu(y)
    z = _dot(y.astype(BF16), wglu_ref[...]) + bglu_ref[...]
    o_ref[...] = y * jax.nn.sigmoid(z)


def _s5(u_tm, s0, lam, bmat, cmat, d, wglu, bglu, *, bt, tt):
    rows = u_tm.shape[0]
    nt = rows // (bt * tt)
    full = lambda shape: pl.BlockSpec(shape, lambda t: (0,) * len(shape))
    return pl.pallas_call(
        functools.partial(_s5_kernel, bt=bt, tt=tt),
        grid=(nt,),
        in_specs=[
            pl.BlockSpec((bt * tt, SSM_WIDTH), lambda t: (t, 0)),
            full((bt, SSM_COLS)), full((1, SSM_COLS)),
            full((2, 256, 2 * SSM_HALF_STATE)), full((2, 2 * SSM_HALF_STATE, 256)),
            full((1, SSM_WIDTH)), full((SSM_WIDTH, SSM_WIDTH)), full((1, SSM_WIDTH)),
        ],
        out_specs=[
            pl.BlockSpec((bt * tt, SSM_WIDTH), lambda t: (t, 0)),
            full((bt, SSM_COLS)),
        ],
        out_shape=[
            jax.ShapeDtypeStruct((rows, SSM_WIDTH), F32),
            jax.ShapeDtypeStruct((bt, SSM_COLS), F32),
        ],
        scratch_shapes=[pltpu.VMEM((bt * tt, SSM_COLS), F32), pltpu.VMEM((bt, SSM_COLS), F32)],
        compiler_params=_cparams(("arbitrary",)),
        name="s5_mixer",
    )(u_tm, s0, lam, bmat, cmat, d, wglu, bglu)


def _mem_kv_kernel(m_ref, g_ref, w_ref, ones_ref, gk_ref, k_ref, v_ref):
    xn = _rms(m_ref[...], g_ref[...])
    kv = _dot(xn.astype(BF16), w_ref[...])
    k_ref[...] = _head_rms(kv[:, :MEM_WIDTH], ones_ref[...], gk_ref[...])
    v_ref[...] = kv[:, MEM_WIDTH:]


def _mem_kv(mem_rows, g, w, ones, gk):
    rows = mem_rows.shape[0]
    tt = min(TOKEN_TILE, rows)
    full = lambda shape: pl.BlockSpec(shape, lambda t: (0,) * len(shape))
    return pl.pallas_call(
        _mem_kv_kernel,
        grid=(rows // tt,),
        in_specs=[pl.BlockSpec((tt, D_MODEL), lambda t: (t, 0)), full((1, D_MODEL)),
                  full((D_MODEL, 2 * MEM_WIDTH)), full((MEM_WIDTH, MEM_WIDTH)), full((1, MEM_WIDTH))],
        out_specs=[pl.BlockSpec((tt, MEM_WIDTH), lambda t: (t, 0))] * 2,
        out_shape=[jax.ShapeDtypeStruct((rows, MEM_WIDTH), F32)] * 2,
        compiler_params=_cparams(("parallel",)),
        name="mem_kv",
    )(mem_rows, g, w, ones, gk)


def _softmax_pv(s, v_b, sink=None):
    m = jnp.max(s, axis=-1, keepdims=True)
    if sink is not None:
        m = jnp.maximum(m, sink)
    p = jnp.exp(s - m)
    den = jnp.sum(p, axis=-1, keepdims=True)
    if sink is not None:
        den = den + jnp.exp(sink - m)
    return _dot((p / den).astype(BF16), v_b)


def _attn_prompt_kernel(sinks_ref, q_ref, kp_ref, kc_ref, vp_ref, vc_ref, qm_ref, mk_ref, mv_ref,
                        osw_ref, omem_ref):
    nb = pl.program_id(1)
    scale = np.float32(HEAD_DIM ** -0.5)
    q = q_ref[0].astype(BF16)
    kk = jnp.concatenate([kp_ref[0], kc_ref[0]], axis=0).astype(BF16)
    vv = jnp.concatenate([vp_ref[0], vc_ref[0]], axis=0).astype(BF16)
    qi = lax.broadcasted_iota(jnp.int32, (WINDOW, 2 * WINDOW), 0)
    ki = lax.broadcasted_iota(jnp.int32, (WINDOW, 2 * WINDOW), 1) - WINDOW
    diff = qi - ki
    valid = (diff >= 0) & (diff < WINDOW) & (nb * WINDOW + ki >= 0)
    outs = []
    for hq in range(SWA_Q_HEADS):
        kv = hq // SWA_REP
        s = _dot_nt(q[:, hq * HEAD_DIM:(hq + 1) * HEAD_DIM],
                    kk[:, kv * HEAD_DIM:(kv + 1) * HEAD_DIM]) * scale
        s = jnp.where(valid, s, -jnp.inf)
        outs.append(_softmax_pv(s, vv[:, kv * HEAD_DIM:(kv + 1) * HEAD_DIM], sinks_ref[hq]))
    osw_ref[0] = jnp.concatenate(outs, axis=-1)

    qm = qm_ref[0].astype(BF16)
    mk = mk_ref[0].astype(BF16)
    mv = mv_ref[0].astype(BF16)
    outs = []
    for h in range(MEM_HEADS):
        sl = slice(h * HEAD_DIM, (h + 1) * HEAD_DIM)
        s = _dot_nt(qm[:, sl], mk[:, sl]) * scale
        outs.append(_softmax_pv(s, mv[:, sl]))
    omem_ref[0] = jnp.concatenate(outs, axis=-1)


def _attn_prompt(sinks, q, k, v, qm, mk, mv):
    B, T, _ = q.shape
    nb = T // WINDOW
    blk = lambda w: pl.BlockSpec((1, WINDOW, w), lambda b, n: (b, n, 0))
    prev = lambda w: pl.BlockSpec((1, WINDOW, w), lambda b, n: (b, jnp.maximum(n - 1, 0), 0))
    memb = pl.BlockSpec((1, MEM_TOKENS, MEM_WIDTH), lambda b, n: (b, 0, 0))
    return pl.pallas_call(
        _attn_prompt_kernel,
        grid=(B, nb),
        in_specs=[pl.BlockSpec(memory_space=pltpu.SMEM),
                  blk(SWA_WIDTH), prev(SWA_KV_WIDTH), blk(SWA_KV_WIDTH), prev(SWA_KV_WIDTH),
                  blk(SWA_KV_WIDTH), blk(MEM_WIDTH), memb, memb],
        out_specs=[blk(SWA_WIDTH), blk(MEM_WIDTH)],
        out_shape=[jax.ShapeDtypeStruct((B, T, SWA_WIDTH), F32),
                   jax.ShapeDtypeStruct((B, T, MEM_WIDTH), F32)],
        compiler_params=_cparams(("parallel", "parallel")),
        name="attn_prompt",
    )(sinks, q, k, k, v, v, qm, mk, mv)


def _attn_sample_kernel(sinks_ref, q_ref, kn_ref, vn_ref, ck_ref, cv_ref, qm_ref, mk_ref, mv_ref,
                        osw_ref, omem_ref, *, bb, ts, start):
    scale = np.float32(HEAD_DIM ** -0.5)
    w = ck_ref.shape[1]
    nq = bb * ts
    q = q_ref[...].astype(BF16)
    kn = kn_ref[...].astype(BF16)
    vn = vn_ref[...].astype(BF16)
    ck = ck_ref[...].reshape(bb * w, SWA_KV_WIDTH).astype(BF16)
    cv = cv_ref[...].reshape(bb * w, SWA_KV_WIDTH).astype(BF16)

    rq = lax.broadcasted_iota(jnp.int32, (nq, bb * w), 0)
    cc = lax.broadcasted_iota(jnp.int32, (nq, bb * w), 1)
    qpos = start + _mod(rq, ts)
    kpos = start - w + _mod(cc, w)
    diff = qpos - kpos
    valid_c = (_div(rq, ts) == _div(cc, w)) & (diff >= 0) & (diff < WINDOW) & (kpos >= 0)
    rq = lax.broadcasted_iota(jnp.int32, (nq, nq), 0)
    cn = lax.broadcasted_iota(jnp.int32, (nq, nq), 1)
    diff = _mod(rq, ts) - _mod(cn, ts)
    valid_n = (_div(rq, ts) == _div(cn, ts)) & (diff >= 0) & (diff < WINDOW)

    outs = []
    for hq in range(SWA_Q_HEADS):
        kv = hq // SWA_REP
        qs = q[:, hq * HEAD_DIM:(hq + 1) * HEAD_DIM]
        ks = slice(kv * HEAD_DIM, (kv + 1) * HEAD_DIM)
        s_c = jnp.where(valid_c, _dot_nt(qs, ck[:, ks]) * scale, -jnp.inf)
        s_n = jnp.where(valid_n, _dot_nt(qs, kn[:, ks]) * scale, -jnp.inf)
        sink = sinks_ref[hq]
        m = jnp.maximum(jnp.maximum(jnp.max(s_c, axis=-1, keepdims=True),
                                    jnp.max(s_n, axis=-1, keepdims=True)), sink)
        p_c = jnp.exp(s_c - m)
        p_n = jnp.exp(s_n - m)
        den = (jnp.sum(p_c, axis=-1, keepdims=True) + jnp.sum(p_n, axis=-1, keepdims=True)
               + jnp.exp(sink - m))
        outs.append(_dot((p_c / den).astype(BF16), cv[:, ks]) + _dot((p_n / den).astype(BF16), vn[:, ks]))
    osw_ref[...] = jnp.concatenate(outs, axis=-1)

    qm = qm_ref[...].astype(BF16)
    mk = mk_ref[...].reshape(bb * MEM_TOKENS, MEM_WIDTH).astype(BF16)
    mv = mv_ref[...].reshape(bb * MEM_TOKENS, MEM_WIDTH).astype(BF16)
    rq = lax.broadcasted_iota(jnp.int32, (nq, bb * MEM_TOKENS), 0)
    cm = lax.broadcasted_iota(jnp.int32, (nq, bb * MEM_TOKENS), 1)
    valid_m = _div(rq, ts) == _div(cm, MEM_TOKENS)
    outs = []
    for h in range(MEM_HEADS):
        sl = slice(h * HEAD_DIM, (h + 1) * HEAD_DIM)
        s = jnp.where(valid_m, _dot_nt(qm[:, sl], mk[:, sl]) * scale, -jnp.inf)
        outs.append(_softmax_pv(s, mv[:, sl]))
    omem_ref[...] = jnp.concatenate(outs, axis=-1)


def _attn_sample(sinks, q, kn, vn, ck, cv, qm, cmk, cmv, *, ts, start):
    B, w, _ = ck.shape
    bb = 8
    rows = lambda wd: pl.BlockSpec((bb * ts, wd), lambda i: (i, 0))
    blk3 = lambda n, wd: pl.BlockSpec((bb, n, wd), lambda i: (i, 0, 0))
    return pl.pallas_call(
        functools.partial(_attn_sample_kernel, bb=bb, ts=ts, start=start),
        grid=(B // bb,),
        in_specs=[pl.BlockSpec(memory_space=pltpu.SMEM),
                  rows(SWA_WIDTH), rows(SWA_KV_WIDTH), rows(SWA_KV_WIDTH),
                  blk3(w, SWA_KV_WIDTH), blk3(w, SWA_KV_WIDTH), rows(MEM_WIDTH),
                  blk3(MEM_TOKENS, MEM_WIDTH), blk3(MEM_TOKENS, MEM_WIDTH)],
        out_specs=[rows(SWA_WIDTH), rows(MEM_WIDTH)],
        out_shape=[jax.ShapeDtypeStruct((B * ts, SWA_WIDTH), F32),
                   jax.ShapeDtypeStruct((B * ts, MEM_WIDTH), F32)],
        compiler_params=_cparams(("parallel",)),
        name="attn_sample",
    )(sinks, q, kn, vn, ck, cv, qm, cmk, cmv)


def _out_proj_kernel(x_ref, ossm_ref, osw_ref, omem_ref, wo_ref, g2_ref, h_ref, xn_ref):
    h = x_ref[0]
    h = h + _dot(ossm_ref[...].astype(BF16), wo_ref[0:512, :])
    h = h + _dot(osw_ref[0].astype(BF16), wo_ref[512:768, :])
    h = h + _dot(omem_ref[0].astype(BF16), wo_ref[768:1024, :])
    h_ref[0] = h
    xn_ref[0] = _rms(h, g2_ref[...]).astype(BF16)


def _out_proj(x, ossm, osw, omem, wo, g2):
    B, T, _ = x.shape
    tt = min(TOKEN_TILE, T)
    full = lambda shape: pl.BlockSpec(shape, lambda b, t: (0,) * len(shape))
    blk = lambda wd: pl.BlockSpec((1, tt, wd), lambda b, t: (b, t, 0))
    return pl.pallas_call(
        _out_proj_kernel,
        grid=(B, T // tt),
        in_specs=[blk(D_MODEL), pl.BlockSpec((tt, SSM_WIDTH), lambda b, t: (t, b)),
                  blk(SWA_WIDTH), blk(MEM_WIDTH), full((D_MODEL, D_MODEL)), full((1, D_MODEL))],
        out_specs=[blk(D_MODEL), blk(D_MODEL)],
        out_shape=[jax.ShapeDtypeStruct((B, T, D_MODEL), F32),
                   jax.ShapeDtypeStruct((B, T, D_MODEL), BF16)],
        compiler_params=_cparams(("parallel", "parallel")),
        name="out_proj",
    )(x, ossm, osw, omem, wo, g2)


def _top16(s, iota):
    work = s
    rank = jnp.full(s.shape, float(PEER_TOPK), F32)
    vals = []
    for a in range(PEER_TOPK):
        m = jnp.max(work, axis=0, keepdims=True)
        idx = jnp.min(jnp.where(work == m, iota, float(PEER_KEYS)), axis=0, keepdims=True)
        sel = iota == idx
        rank = jnp.where(sel, float(a), rank)
        work = jnp.where(sel, -jnp.inf, work)
        vals.append(m)
    return vals, rank


def _sort_pairs(n):
    pairs = []
    t = max(1, (n - 1).bit_length())
    p = 1 << (t - 1)
    while p > 0:
        q, r, d = 1 << (t - 1), 0, p
        while d > 0:
            pairs += [(i, i + d) for i in range(n - d) if (i & p) == r]
            d, q, r = q - p, q >> 1, p
        p >>= 1
    return pairs


_SORT16_PAIRS = _sort_pairs(PEER_TOPK)


def _cmp_exchange(xs, i, j):
    xs[i], xs[j] = jnp.maximum(xs[i], xs[j]), jnp.minimum(xs[i], xs[j])


def _top16_sorted(s):
    n = PEER_TOPK
    xs = [s[8 * k:8 * k + 8, :] for k in range(n)]
    for i, j in _SORT16_PAIRS:
        _cmp_exchange(xs, i, j)
    for shift in (4, 2, 1):
        other = [pltpu.roll(x, shift, 0) for x in xs]
        xs = [jnp.maximum(xs[k], other[n - 1 - k]) for k in range(n)]
        for d in (8, 4, 2, 1):
            for k in range(n):
                if k & d == 0:
                    _cmp_exchange(xs, k, k + d)
    return xs


def _candidates(v1, v2, cand_scr):
    L = v1[0].shape[1]
    cand_scr[...] = jnp.full((_CAND_ROWS, L), -jnp.inf, F32)
    for a in range(PEER_TOPK):
        for b in range(_CAND_COUNT[a]):
            cand_scr[pl.ds(_CAND_ROW0[a] + b, 1), :] = v1[a] + v2[b]
    return cand_scr[...], lax.broadcasted_iota(jnp.int32, (_CAND_ROWS, L), 0).astype(F32)


def _taken_by_index(cand, iota_c):
    work = cand
    taken = jnp.zeros(cand.shape, F32)
    for _ in range(PEER_TOPK):
        m = jnp.max(work, axis=0, keepdims=True)
        idx = jnp.min(jnp.where(work == m, iota_c, float(_CAND_ROWS)), axis=0, keepdims=True)
        sel = iota_c == idx
        taken = jnp.where(sel, 1.0, taken)
        work = jnp.where(sel, -jnp.inf, work)
    return taken


_SORT8_PAIRS = _sort_pairs(8)


def _candidates_sorted(cand):
    n = PEER_TOPK
    xs = [cand[8 * k:8 * k + 8, :] for k in range(_CAND_ROWS // 8)]
    xs.append(jnp.full_like(xs[0], -jnp.inf))
    for i, j in _SORT8_PAIRS:
        _cmp_exchange(xs, i, j)
    other = [pltpu.roll(x, 4, 0) for x in xs]
    xs = xs + other[::-1]
    for shift in (4, 2, 1):
        if shift != 4:
            other = [pltpu.roll(x, shift, 0) for x in xs]
            xs = [jnp.maximum(xs[k], other[n - 1 - k]) for k in range(n)]
        for d in (8, 4, 2, 1):
            for k in range(n):
                if k & d == 0:
                    _cmp_exchange(xs, k, k + d)
    return xs


def _group_counts(taken, iota_c):
    cnt = []
    for a in range(PEER_TOPK):
        lo, hi = _CAND_ROW0[a], _CAND_ROW0[a] + _CAND_COUNT[a]
        if hi - lo == 1:
            cnt.append(taken[lo:hi])
            continue
        t0, t1 = (lo // 8) * 8, -(-hi // 8) * 8
        part = taken[t0:t1]
        if (lo, hi) != (t0, t1):
            rows = iota_c[t0:t1]
            part = jnp.where((rows >= float(lo)) & (rows < float(hi)), part, 0.0)
        cnt.append(jnp.sum(part, axis=0, keepdims=True))
    return cnt


def _route_tile(s1, s2, cand_scr, n1_out, r2_out):
    n = PEER_TOPK
    t1 = _top16_sorted(s1)
    t2 = _top16_sorted(s2)
    v1 = [t[0:1, :] for t in t1]
    v2 = [t[0:1, :] for t in t2]
    cand, iota_c = _candidates(v1, v2, cand_scr)
    tc = _candidates_sorted(cand)
    z = jnp.ones_like(v1[0])
    for k in range(1, n):
        z = z + jnp.exp(tc[k][0:1, :] - tc[0][0:1, :])
    taken = jnp.where(cand >= tc[n - 1][0:1, :], 1.0, 0.0)
    cnt = _group_counts(taken, iota_c)

    n_taken = jnp.sum(taken, axis=0, keepdims=True)
    bad = jnp.broadcast_to(jnp.where(n_taken != float(n), 1.0, 0.0), (8, s1.shape[1]))
    for s, t in ((s1, t1), (s2, t2)):
        gap = t[0] - t[1]
        for a in range(1, n - 1):
            gap = jnp.minimum(gap, t[a] - t[a + 1])
        at_least = jnp.zeros((8, s.shape[1]), F32)
        for k in range(n):
            at_least = at_least + jnp.where(s[8 * k:8 * k + 8, :] >= t[n - 1], 1.0, 0.0)
        at_least = jnp.sum(at_least, axis=0, keepdims=True)
        bad = jnp.where((gap <= 0.0) | (at_least != float(n)), 1.0, bad)
    has_ties = jnp.max(bad) > 0.0

    @pl.when(jnp.logical_not(has_ties))
    def _by_value():
        cnt_b = [jnp.broadcast_to(c, (8, c.shape[1])) for c in cnt]
        ranks = []
        for k in range(n):
            x1 = s1[8 * k:8 * k + 8, :]
            x2 = s2[8 * k:8 * k + 8, :]
            n1 = cnt_b[0]
            r2 = jnp.zeros_like(x2)
            for a in range(n):
                n1 = jnp.where(t1[a] > x1, cnt_b[a + 1] if a + 1 < n else 0.0, n1)
                r2 = jnp.where(t2[a] > x2, float(a + 1), r2)
            n1_out[8 * k:8 * k + 8, :] = n1
            ranks.append(r2)
        r2_out[...] = jnp.concatenate(ranks, axis=0).astype(r2_out.dtype)

    @pl.when(has_ties)
    def _by_index():
        iota = lax.broadcasted_iota(jnp.int32, s1.shape, 0).astype(F32)
        _, r1 = _top16(s1, iota)
        _, r2 = _top16(s2, iota)
        cnt_exact = _group_counts(_taken_by_index(cand, iota_c), iota_c)
        n1 = jnp.zeros(s1.shape, F32)
        for a in range(n):
            n1 = jnp.where(r1 == float(a), cnt_exact[a], n1)
        n1_out[...] = n1
        r2_out[...] = r2.astype(r2_out.dtype)

    c1 = jnp.exp(s1 - v1[0]) / z
    e2 = jnp.exp(s2 - v2[0])
    return c1, e2


def _peer_kernel(xn_ref, h_ref, wqt_ref, k1_ref, k2_ref, u_ref, vt_ref, y_ref,
                 n1_scr, c1_scr, s2_scr, r2_scr, e2_scr, acc_scr, a_scr, w_scr,
                 q_scr, xt_scr, cand_scr, *, n_steps):
    g = pl.program_id(1)
    ng = n_steps
    tt = xn_ref.shape[0]
    eb = u_ref.shape[0]
    keys_per_block = eb // PEER_KEYS
    n_lane_tiles = tt // LANES

    @pl.when(g == 0)
    def _route():
        xt_scr[...] = xn_ref[...].T
        xt = xt_scr[...]
        q_scr[...] = _dot(wqt_ref[...], xt).astype(BF16)
        for h in range(PEER_HEADS):
            for side, dst in enumerate((n1_scr, s2_scr)):
                k_ref = (k1_ref, k2_ref)[side]
                r0 = h * 2 * PEER_HALF + side * PEER_HALF
                s = _dot(k_ref[...], q_scr[r0:r0 + PEER_HALF, :])
                for c in range(n_lane_tiles):
                    dst[h, c] = s[:, c * LANES:(c + 1) * LANES]

        def body(i, carry):
            h = i // n_lane_tiles
            c = i % n_lane_tiles
            c1, e2 = _route_tile(n1_scr[h, c], s2_scr[h, c], cand_scr, n1_scr.at[h, c], r2_scr.at[h, c])
            c1_scr[h, c] = c1
            e2_scr[h, c] = e2.astype(BF16)
            return carry

        lax.fori_loop(0, PEER_HEADS * n_lane_tiles, body, 0)
        acc_scr[...] = jnp.zeros_like(acc_scr)

    def gate(a_scr, w_scr, block):
        zero = jnp.zeros((), BF16)
        group = PEER_GATE_KEYS
        for k0 in range(0, keys_per_block, group):
            def lane_tile(c, carry, k0=k0):
                lanes = pl.ds(pl.multiple_of(c * LANES, LANES), LANES)
                gts = [None] * group
                for h in range(PEER_HEADS):
                    r2 = r2_scr[h, c].reshape(PEER_KEYS // 16, 16, LANES)
                    e2 = e2_scr[h, c].reshape(PEER_KEYS // 16, 16, LANES)
                    for ii in range(group):
                        row = pl.ds(block * keys_per_block + k0 + ii, 1)
                        n_b = jnp.broadcast_to(n1_scr[h, c, row, :], (16, LANES)).astype(BF16)[None]
                        c_b = jnp.broadcast_to(c1_scr[h, c, row, :], (16, LANES)).astype(BF16)[None]
                        term = jnp.where(r2 < n_b, e2, zero) * c_b
                        gts[ii] = term if h == 0 else gts[ii] + term
                for ii in range(group):
                    r0 = (k0 + ii) * PEER_KEYS
                    rows = slice(r0, r0 + PEER_KEYS)
                    act = _gelu(a_scr[rows, lanes].astype(BF16))
                    w_scr[rows, lanes] = gts[ii].reshape(PEER_KEYS, LANES) * act
                return carry

            lax.fori_loop(0, n_lane_tiles, lane_tile, 0)

    @pl.when(g < ng)
    def _scores():
        a_scr[...] = _dot(u_ref[...], xt_scr[...])

    @pl.when(g > 0)
    def _v_product():
        acc_scr[...] += _dot(vt_ref[...], w_scr[...])

    @pl.when(g < ng)
    def _weights():
        gate(a_scr, w_scr, g)

    @pl.when(g == ng)
    def _fin():
        y_ref[...] = h_ref[...] + acc_scr[...].T


def _peer(xn, h, wqt, k1, k2, u_tab, vt_tab):
    n = xn.shape[0]
    tt = min(TOKEN_TILE, n)
    eb = PEER_EXPERT_BLOCK
    ng = PEER_EXPERTS // eb
    full = lambda shape: pl.BlockSpec(shape, lambda i, g: (0,) * len(shape), pipeline_mode=pl.Buffered(1))
    tok = pl.BlockSpec((tt, D_MODEL), lambda i, g: (i, 0))
    head_f32 = pltpu.VMEM((PEER_HEADS, tt // LANES, PEER_KEYS, LANES), F32)
    head_bf16 = pltpu.VMEM((PEER_HEADS, tt // LANES, PEER_KEYS, LANES), BF16)
    return pl.pallas_call(
        functools.partial(_peer_kernel, n_steps=ng),
        grid=(n // tt, ng + 1),
        in_specs=[tok, tok, full((2 * PEER_HEADS * PEER_HALF, D_MODEL)),
                  full((PEER_KEYS, PEER_HALF)), full((PEER_KEYS, PEER_HALF)),
                  pl.BlockSpec((eb, D_MODEL), lambda i, g: (jnp.minimum(g, ng - 1), 0)),
                  pl.BlockSpec((D_MODEL, eb), lambda i, g: (0, jnp.maximum(g - 1, 0)))],
        out_specs=tok,
        out_shape=jax.ShapeDtypeStruct((n, D_MODEL), F32),
        scratch_shapes=[head_f32, head_f32, head_f32, head_bf16, head_bf16,
                        pltpu.VMEM((D_MODEL, tt), F32),
                        pltpu.VMEM((eb, tt), F32), pltpu.VMEM((eb, tt), BF16),
                        pltpu.VMEM((2 * PEER_HEADS * PEER_HALF, tt), BF16),
                        pltpu.VMEM((D_MODEL, tt), BF16),
                        pltpu.VMEM((_CAND_ROWS, LANES), F32)],
        compiler_params=_cparams(("parallel", "arbitrary")),
        name="peer",
    )(xn, h, wqt, k1, k2, u_tab, vt_tab)


def _rope_tables(pos):
    half = HEAD_DIM // 2
    inv = ROPE_THETA ** (-jnp.arange(half, dtype=F32) / half)
    ang = pos.astype(F32)[:, None] * inv[None, :]
    cos = jnp.cos(ang)
    sin = jnp.sin(ang)
    cos = jnp.tile(jnp.concatenate([cos, cos], axis=-1), (1, SWA_Q_HEADS))
    sin = jnp.tile(jnp.concatenate([-sin, sin], axis=-1), (1, SWA_Q_HEADS))
    return cos, sin


def _ssm_params(log_dt, a_re, a_im, b_re, b_im, c_re, c_im):
    dt = jnp.exp(log_dt)
    mag = jnp.exp(a_re * dt)
    lam_re = mag * jnp.cos(a_im * dt)
    lam_im = mag * jnp.sin(a_im * dt)
    den = a_re * a_re + a_im * a_im
    z_re = ((lam_re - 1.0) * a_re + lam_im * a_im) / den
    z_im = (lam_im * a_re - (lam_re - 1.0) * a_im) / den
    bb_re = z_re[..., None] * b_re - z_im[..., None] * b_im
    bb_im = z_re[..., None] * b_im + z_im[..., None] * b_re
    hg = SSM_HALF_GROUPS
    eye = jnp.eye(hg, dtype=F32)
    bb = jnp.stack([bb_re, bb_im]).reshape(2, 2, hg, SSM_STATE, SSM_GROUP)
    bmat = jnp.einsum('rjgnc,gh->jgcrhn', bb, eye).reshape(2, hg * SSM_GROUP, 2 * SSM_HALF_STATE)
    cc = jnp.stack([c_re, -c_im]).reshape(2, 2, hg, SSM_GROUP, SSM_STATE)
    cmat = jnp.einsum('rjgcn,gh->jrgnhc', cc, eye).reshape(2, 2 * SSM_HALF_STATE, hg * SSM_GROUP)
    lam = jnp.stack([lam_re.reshape(2, SSM_HALF_STATE), lam_im.reshape(2, SSM_HALF_STATE)], axis=1)
    return lam.reshape(1, SSM_COLS), bmat.astype(BF16), cmat.astype(BF16)


def _state_to_cols(s_re, s_im):
    b = s_re.shape[0]
    st = jnp.stack([s_re.reshape(b, 2, SSM_HALF_STATE), s_im.reshape(b, 2, SSM_HALF_STATE)], axis=2)
    return st.reshape(b, SSM_COLS)


def _cols_to_state(cols):
    b = cols.shape[0]
    st = cols.reshape(b, 2, 2, SSM_HALF_STATE)
    return (st[:, :, 0].reshape(b, SSM_GROUPS, SSM_STATE), st[:, :, 1].reshape(b, SSM_GROUPS, SSM_STATE))


def kernel(x_prompt, x_sample, state_ssm_re, state_ssm_im, cache_win_k, cache_win_v, cache_mem_k, cache_mem_v, mem_prompt, norm1_g, w_in, ssm_log_dt, ssm_a_re, ssm_a_im, ssm_b_re, ssm_b_im, ssm_c_re, ssm_c_im, ssm_d, w_glu, b_glu, swa_q_norm, swa_k_norm, swa_sinks, mem_norm_g, w_mem_kv, mem_q_norm, mem_k_norm, w_out, norm2_g, peer_wq, peer_k1, peer_k2, peer_u, peer_v):
    depth = norm1_g.shape[0]
    assert depth == 1
    l = 0
    B, T, _ = x_prompt.shape
    SB, ST, _ = x_sample.shape
    w = cache_win_k.shape[2]

    row = lambda a: a.reshape(1, -1)
    g1 = row(norm1_g[l])
    g2 = row(norm2_g[l])
    win = w_in[l].astype(BF16)
    wo = w_out[l].astype(BF16)
    wglu = w_glu[l].astype(BF16)
    bglu = row(b_glu[l])
    gq = row(jnp.tile(swa_q_norm[l], SWA_Q_HEADS))
    gk = row(jnp.tile(swa_k_norm[l], SWA_KV_HEADS))
    gm = row(jnp.tile(mem_q_norm[l], MEM_HEADS))
    gmk = row(jnp.tile(mem_k_norm[l], MEM_HEADS))
    gmem = row(mem_norm_g[l])
    wkv = w_mem_kv[l].astype(BF16)
    sinks = swa_sinks[l]
    head_id = np.arange(SWA_WIDTH) // HEAD_DIM
    ones = jnp.asarray(head_id[:, None] == head_id[None, :], dtype=BF16)
    lam, bmat, cmat = _ssm_params(ssm_log_dt[l], ssm_a_re[l], ssm_a_im[l], ssm_b_re[l], ssm_b_im[l],
                                  ssm_c_re[l], ssm_c_im[l])
    dskip = row(ssm_d[l])
    wqt = peer_wq[l].astype(BF16).T
    k1 = peer_k1[l].astype(BF16)
    k2 = peer_k2[l].astype(BF16)
    u_tab = peer_u[l].astype(BF16)
    vt_tab = peer_v[l].astype(BF16).T

    cos_p, sin_p = _rope_tables(jnp.arange(T, dtype=jnp.int32))
    u_p, q_p, k_p, v_p, qm_p = _in_proj(x_prompt, cos_p, sin_p, g1, win, ones, gq, gk, gm)
    zeros = jnp.zeros((B, SSM_COLS), F32)
    ossm_p, sfin_p = _s5(u_p.reshape(T * B, SSM_WIDTH), zeros, lam, bmat, cmat, dskip, wglu, bglu,
                         bt=B, tt=S5_TIME_TILE)
    mk, mv = _mem_kv(mem_prompt.reshape(B * MEM_TOKENS, D_MODEL), gmem, wkv, ones, gmk)
    mk = mk.reshape(B, MEM_TOKENS, MEM_WIDTH)
    mv = mv.reshape(B, MEM_TOKENS, MEM_WIDTH)
    osw_p, omem_p = _attn_prompt(sinks, q_p, k_p, v_p, qm_p, mk, mv)
    h_p, xn_p = _out_proj(x_prompt, ossm_p.reshape(T, B * SSM_WIDTH), osw_p, omem_p, wo, g2)
    y_p = _peer(xn_p.reshape(B * T, D_MODEL), h_p.reshape(B * T, D_MODEL), wqt, k1, k2, u_tab, vt_tab)
    y_p = y_p.reshape(B, T, D_MODEL)
    p_sr, p_si = _cols_to_state(sfin_p)
    p_wk = k_p[:, T - w:].reshape(B, w, SWA_KV_HEADS, HEAD_DIM)
    p_wv = v_p[:, T - w:].reshape(B, w, SWA_KV_HEADS, HEAD_DIM)
    p_mk = mk.reshape(B, MEM_TOKENS, MEM_HEADS, HEAD_DIM)
    p_mv = mv.reshape(B, MEM_TOKENS, MEM_HEADS, HEAD_DIM)

    n_s = SB * ST
    pos_s = PAST_LEN + jnp.tile(jnp.arange(ST, dtype=jnp.int32), SB)
    cos_s, sin_s = _rope_tables(pos_s)
    xs = x_sample.reshape(1, n_s, D_MODEL)
    u_s, q_s, k_s, v_s, qm_s = _in_proj(xs, cos_s, sin_s, g1, win, ones, gq, gk, gm)
    u_tm = u_s.reshape(SB, ST, SSM_WIDTH).transpose(1, 0, 2).reshape(n_s, SSM_WIDTH)
    ossm_tm, sfin_s = _s5(u_tm, _state_to_cols(state_ssm_re[l], state_ssm_im[l]), lam, bmat, cmat,
                          dskip, wglu, bglu, bt=SB, tt=ST)
    ossm_s = ossm_tm.reshape(ST, SB, SSM_WIDTH).transpose(1, 0, 2).reshape(n_s, SSM_WIDTH)
    ck = cache_win_k[l].reshape(SB, w, SWA_KV_WIDTH)
    cv = cache_win_v[l].reshape(SB, w, SWA_KV_WIDTH)
    q_s2 = q_s.reshape(n_s, SWA_WIDTH)
    k_s2 = k_s.reshape(n_s, SWA_KV_WIDTH)
    v_s2 = v_s.reshape(n_s, SWA_KV_WIDTH)
    osw_s, omem_s = _attn_sample(sinks, q_s2, k_s2, v_s2, ck, cv, qm_s.reshape(n_s, MEM_WIDTH),
                                 cache_mem_k[l].reshape(SB, MEM_TOKENS, MEM_WIDTH),
                                 cache_mem_v[l].reshape(SB, MEM_TOKENS, MEM_WIDTH),
                                 ts=ST, start=PAST_LEN)
    h_s, xn_s = _out_proj(xs, ossm_s, osw_s.reshape(1, n_s, SWA_WIDTH), omem_s.reshape(1, n_s, MEM_WIDTH),
                          wo, g2)
    y_s = _peer(xn_s.reshape(n_s, D_MODEL), h_s.reshape(n_s, D_MODEL), wqt, k1, k2, u_tab, vt_tab)
    y_s = y_s.reshape(SB, ST, D_MODEL)
    s_sr, s_si = _cols_to_state(sfin_s)
    s_wk = jnp.concatenate([ck, k_s2.reshape(SB, ST, SWA_KV_WIDTH)], axis=1)[:, -w:]
    s_wv = jnp.concatenate([cv, v_s2.reshape(SB, ST, SWA_KV_WIDTH)], axis=1)[:, -w:]
    s_wk = s_wk.reshape(SB, w, SWA_KV_HEADS, HEAD_DIM)
    s_wv = s_wv.reshape(SB, w, SWA_KV_HEADS, HEAD_DIM)

    st = lambda a: a[None]
    return (y_p, y_s, st(p_sr), st(p_si), st(p_wk), st(p_wv), st(p_mk), st(p_mv),
            st(s_sr), st(s_si), st(s_wk), st(s_wv))
```

```python
import functools
import math

import jax
import jax.numpy as jnp
import numpy as np
from jax import lax
from jax.experimental import pallas as pl
from jax.experimental.pallas import tpu as pltpu

F32 = jnp.float32
BF16 = jnp.bfloat16

D_MODEL = 1024
HEAD_DIM = 64
EPS = 1e-6
ROPE_THETA = 10000.0
PAST_LEN = 8192
SSM_WIDTH = 512
SSM_GROUP = 16
SSM_GROUPS = 32
SSM_STATE = 64
SSM_HALF_GROUPS = SSM_GROUPS // 2
SSM_HALF_STATE = SSM_HALF_GROUPS * SSM_STATE
SSM_COLS = 2 * 2 * SSM_HALF_STATE
SWA_Q_HEADS = 4
SWA_KV_HEADS = 2
SWA_REP = 2
SWA_WIDTH = 256
SWA_KV_WIDTH = 128
WINDOW = 128
MEM_TOKENS = 256
MEM_HEADS = 4
MEM_WIDTH = 256
IN_WIDTH = 1280
PEER_HEADS = 8
PEER_KEYS = 128
PEER_EXPERTS = PEER_KEYS * PEER_KEYS
PEER_TOPK = 16
PEER_HALF = 128

LANES = 128
VMEM_LIMIT = 60 * 1024 * 1024

TOKEN_TILE = 512
PEER_EXPERT_BLOCK = 2048
PEER_GATE_KEYS = 4
PEER_ROUTE_TILES = 2
S5_TIME_TILE = 64

_CAND_COUNT = [PEER_TOPK // (a + 1) for a in range(PEER_TOPK)]
_CAND_ROW0 = [0, 16, 24, 32, 36, 40, 42, 44, 48, 49, 50, 51, 52, 53, 54, 55]
_CAND_ROWS = 56


def _cparams(sem):
    return pltpu.CompilerParams(dimension_semantics=sem, vmem_limit_bytes=VMEM_LIMIT)


def _rms(x, g):
    return x * lax.rsqrt(jnp.mean(x * x, axis=-1, keepdims=True) + EPS) * g


def _gelu(x):
    return 0.5 * x * (1.0 + lax.erf(x * math.sqrt(0.5)))


def _dot(a, b):
    return jnp.dot(a, b, preferred_element_type=F32)


def _dot_nt(a, b):
    return lax.dot_general(a, b, (((1,), (1,)), ((), ())), preferred_element_type=F32)


def _div(x, n):
    return x >> (n.bit_length() - 1) if n & (n - 1) == 0 else x // n


def _mod(x, n):
    return x & (n - 1) if n & (n - 1) == 0 else x % n


def _head_rms(x, ones_bd, g):
    sq = x * x
    hi = sq.astype(BF16)
    lo = (sq - hi.astype(F32)).astype(BF16)
    ms = (_dot(hi, ones_bd) + _dot(lo, ones_bd)) * np.float32(1.0 / HEAD_DIM)
    return x * lax.rsqrt(ms + EPS) * g


def _rope(x, cos, sin_signed):
    w = x.shape[-1]
    lane = lax.broadcasted_iota(jnp.int32, x.shape, 1)
    first_half = _mod(lane, HEAD_DIM) < (HEAD_DIM // 2)
    partner = jnp.where(first_half, pltpu.roll(x, w - HEAD_DIM // 2, 1),
                        pltpu.roll(x, HEAD_DIM // 2, 1))
    return x * cos + partner * sin_signed


def _in_proj_kernel(x_ref, g1_ref, win_ref, ones_ref, gq_ref, gk_ref, gm_ref, cos_ref, sin_ref,
                    u_ref, q_ref, k_ref, v_ref, qm_ref):
    x = x_ref[0]
    xn = _rms(x, g1_ref[...])
    proj = _dot(xn.astype(BF16), win_ref[...])
    u_ref[...] = proj[:, :SSM_WIDTH]
    q = proj[:, 512:768]
    k = proj[:, 768:896]
    v_ref[0] = proj[:, 896:1024]
    qm = proj[:, 1024:1280]
    ones = ones_ref[...]
    cos = cos_ref[...]
    sin = sin_ref[...]
    q_ref[0] = _rope(_head_rms(q, ones, gq_ref[...]), cos, sin)
    k_ref[0] = _rope(_head_rms(k, ones[:SWA_KV_WIDTH, :SWA_KV_WIDTH], gk_ref[...]),
                     cos[:, :SWA_KV_WIDTH], sin[:, :SWA_KV_WIDTH])
    qm_ref[0] = _head_rms(qm, ones, gm_ref[...])


def _in_proj(x, cos, sin, g1, win, ones, gq, gk, gm):
    B, T, _ = x.shape
    tt = min(TOKEN_TILE, T)
    grid = (B, T // tt)
    full = lambda shape: pl.BlockSpec(shape, lambda b, t: (0,) * len(shape))
    return pl.pallas_call(
        _in_proj_kernel,
        grid=grid,
        in_specs=[
            pl.BlockSpec((1, tt, D_MODEL), lambda b, t: (b, t, 0)),
            full((1, D_MODEL)), full((D_MODEL, IN_WIDTH)), full((SWA_WIDTH, SWA_WIDTH)),
            full((1, SWA_WIDTH)), full((1, SWA_KV_WIDTH)), full((1, MEM_WIDTH)),
            pl.BlockSpec((tt, SWA_WIDTH), lambda b, t: (t, 0)),
            pl.BlockSpec((tt, SWA_WIDTH), lambda b, t: (t, 0)),
        ],
        out_specs=[
            pl.BlockSpec((tt, SSM_WIDTH), lambda b, t: (t, b)),
            pl.BlockSpec((1, tt, SWA_WIDTH), lambda b, t: (b, t, 0)),
            pl.BlockSpec((1, tt, SWA_KV_WIDTH), lambda b, t: (b, t, 0)),
            pl.BlockSpec((1, tt, SWA_KV_WIDTH), lambda b, t: (b, t, 0)),
            pl.BlockSpec((1, tt, MEM_WIDTH), lambda b, t: (b, t, 0)),
        ],
        out_shape=[
            jax.ShapeDtypeStruct((T, B * SSM_WIDTH), F32),
            jax.ShapeDtypeStruct((B, T, SWA_WIDTH), F32),
            jax.ShapeDtypeStruct((B, T, SWA_KV_WIDTH), F32),
            jax.ShapeDtypeStruct((B, T, SWA_KV_WIDTH), F32),
            jax.ShapeDtypeStruct((B, T, MEM_WIDTH), F32),
        ],
        compiler_params=_cparams(("parallel", "parallel")),
        name="in_proj",
    )(x, g1, win, ones, gq, gk, gm, cos, sin)


def _s5_kernel(u_ref, s0_ref, lam_ref, bmat_ref, cmat_ref, d_ref, wglu_ref, bglu_ref,
               o_ref, sfin_ref, s_scr, carry_scr, *, bt, tt):
    @pl.when(pl.program_id(0) == 0)
    def _():
        carry_scr[...] = s0_ref[...]

    u = u_ref[...]
    ub = u.astype(BF16)
    hw = 2 * SSM_HALF_STATE
    for j in range(2):
        s_scr[:, j * hw:(j + 1) * hw] = _dot(ub[:, j * 256:(j + 1) * 256], bmat_ref[j])

    def step(t, carry):
        r0 = pl.multiple_of(t * bt, bt)
        for j in range(2):
            c_re = pl.ds(j * hw, SSM_HALF_STATE)
            c_im = pl.ds(j * hw + SSM_HALF_STATE, SSM_HALF_STATE)
            p_re = carry_scr[:, c_re]
            p_im = carry_scr[:, c_im]
            l_re = lam_ref[:, c_re]
            l_im = lam_ref[:, c_im]
            n_re = l_re * p_re - l_im * p_im + s_scr[pl.ds(r0, bt), c_re]
            n_im = l_re * p_im + l_im * p_re + s_scr[pl.ds(r0, bt), c_im]
            s_scr[pl.ds(r0, bt), c_re] = n_re
            s_scr[pl.ds(r0, bt), c_im] = n_im
            carry_scr[:, c_re] = n_re
            carry_scr[:, c_im] = n_im
        return carry

    lax.fori_loop(0, tt, step, 0)
    sfin_ref[...] = carry_scr[...]

    ys = [_dot(s_scr[:, j * hw:(j + 1) * hw].astype(BF16), cmat_ref[j]) for j in range(2)]
    y = jnp.concatenate(ys, axis=-1) + d_ref[...] * u
    y = _gelu(y)
    z = _dot(y.astype(BF16), wglu_ref[...]) + bglu_ref[...]
    o_ref[...] = y * jax.nn.sigmoid(z)


def _s5(u_tm, s0, lam, bmat, cmat, d, wglu, bglu, *, bt, tt):
    rows = u_tm.shape[0]
    nt = rows // (bt * tt)
    full = lambda shape: pl.BlockSpec(shape, lambda t: (0,) * len(shape))
    return pl.pallas_call(
        functools.partial(_s5_kernel, bt=bt, tt=tt),
        grid=(nt,),
        in_specs=[
            pl.BlockSpec((bt * tt, SSM_WIDTH), lambda t: (t, 0)),
            full((bt, SSM_COLS)), full((1, SSM_COLS)),
            full((2, 256, 2 * SSM_HALF_STATE)), full((2, 2 * SSM_HALF_STATE, 256)),
            full((1, SSM_WIDTH)), full((SSM_WIDTH, SSM_WIDTH)), full((1, SSM_WIDTH)),
        ],
        out_specs=[
            pl.BlockSpec((bt * tt, SSM_WIDTH), lambda t: (t, 0)),
            full((bt, SSM_COLS)),
        ],
        out_shape=[
            jax.ShapeDtypeStruct((rows, SSM_WIDTH), F32),
            jax.ShapeDtypeStruct((bt, SSM_COLS), F32),
        ],
        scratch_shapes=[pltpu.VMEM((bt * tt, SSM_COLS), F32), pltpu.VMEM((bt, SSM_COLS), F32)],
        compiler_params=_cparams(("arbitrary",)),
        name="s5_mixer",
    )(u_tm, s0, lam, bmat, cmat, d, wglu, bglu)


def _mem_kv_kernel(m_ref, g_ref, w_ref, ones_ref, gk_ref, k_ref, v_ref):
    xn = _rms(m_ref[...], g_ref[...])
    kv = _dot(xn.astype(BF16), w_ref[...])
    k_ref[...] = _head_rms(kv[:, :MEM_WIDTH], ones_ref[...], gk_ref[...])
    v_ref[...] = kv[:, MEM_WIDTH:]


def _mem_kv(mem_rows, g, w, ones, gk):
    rows = mem_rows.shape[0]
    tt = min(TOKEN_TILE, rows)
    full = lambda shape: pl.BlockSpec(shape, lambda t: (0,) * len(shape))
    return pl.pallas_call(
        _mem_kv_kernel,
        grid=(rows // tt,),
        in_specs=[pl.BlockSpec((tt, D_MODEL), lambda t: (t, 0)), full((1, D_MODEL)),
                  full((D_MODEL, 2 * MEM_WIDTH)), full((MEM_WIDTH, MEM_WIDTH)), full((1, MEM_WIDTH))],
        out_specs=[pl.BlockSpec((tt, MEM_WIDTH), lambda t: (t, 0))] * 2,
        out_shape=[jax.ShapeDtypeStruct((rows, MEM_WIDTH), F32)] * 2,
        compiler_params=_cparams(("parallel",)),
        name="mem_kv",
    )(mem_rows, g, w, ones, gk)


def _softmax_pv(s, v_b, sink=None):
    m = jnp.max(s, axis=-1, keepdims=True)
    if sink is not None:
        m = jnp.maximum(m, sink)
    p = jnp.exp(s - m)
    den = jnp.sum(p, axis=-1, keepdims=True)
    if sink is not None:
        den = den + jnp.exp(sink - m)
    return _dot((p / den).astype(BF16), v_b)


def _attn_prompt_kernel(sinks_ref, q_ref, kp_ref, kc_ref, vp_ref, vc_ref, qm_ref, mk_ref, mv_ref,
                        osw_ref, omem_ref):
    nb = pl.program_id(1)
    scale = np.float32(HEAD_DIM ** -0.5)
    q = q_ref[0].astype(BF16)
    kk = jnp.concatenate([kp_ref[0], kc_ref[0]], axis=0).astype(BF16)
    vv = jnp.concatenate([vp_ref[0], vc_ref[0]], axis=0).astype(BF16)
    qi = lax.broadcasted_iota(jnp.int32, (WINDOW, 2 * WINDOW), 0)
    ki = lax.broadcasted_iota(jnp.int32, (WINDOW, 2 * WINDOW), 1) - WINDOW
    diff = qi - ki
    valid = (diff >= 0) & (diff < WINDOW) & (nb * WINDOW + ki >= 0)
    outs = []
    for hq in range(SWA_Q_HEADS):
        kv = hq // SWA_REP
        s = _dot_nt(q[:, hq * HEAD_DIM:(hq + 1) * HEAD_DIM],
                    kk[:, kv * HEAD_DIM:(kv + 1) * HEAD_DIM]) * scale
        s = jnp.where(valid, s, -jnp.inf)
        outs.append(_softmax_pv(s, vv[:, kv * HEAD_DIM:(kv + 1) * HEAD_DIM], sinks_ref[hq]))
    osw_ref[0] = jnp.concatenate(outs, axis=-1)

    qm = qm_ref[0].astype(BF16)
    mk = mk_ref[0].astype(BF16)
    mv = mv_ref[0].astype(BF16)
    outs = []
    for h in range(MEM_HEADS):
        sl = slice(h * HEAD_DIM, (h + 1) * HEAD_DIM)
        s = _dot_nt(qm[:, sl], mk[:, sl]) * scale
        outs.append(_softmax_pv(s, mv[:, sl]))
    omem_ref[0] = jnp.concatenate(outs, axis=-1)


def _attn_prompt(sinks, q, k, v, qm, mk, mv):
    B, T, _ = q.shape
    nb = T // WINDOW
    blk = lambda w: pl.BlockSpec((1, WINDOW, w), lambda b, n: (b, n, 0))
    prev = lambda w: pl.BlockSpec((1, WINDOW, w), lambda b, n: (b, jnp.maximum(n - 1, 0), 0))
    memb = pl.BlockSpec((1, MEM_TOKENS, MEM_WIDTH), lambda b, n: (b, 0, 0))
    return pl.pallas_call(
        _attn_prompt_kernel,
        grid=(B, nb),
        in_specs=[pl.BlockSpec(memory_space=pltpu.SMEM),
                  blk(SWA_WIDTH), prev(SWA_KV_WIDTH), blk(SWA_KV_WIDTH), prev(SWA_KV_WIDTH),
                  blk(SWA_KV_WIDTH), blk(MEM_WIDTH), memb, memb],
        out_specs=[blk(SWA_WIDTH), blk(MEM_WIDTH)],
        out_shape=[jax.ShapeDtypeStruct((B, T, SWA_WIDTH), F32),
                   jax.ShapeDtypeStruct((B, T, MEM_WIDTH), F32)],
        compiler_params=_cparams(("parallel", "parallel")),
        name="attn_prompt",
    )(sinks, q, k, k, v, v, qm, mk, mv)


def _attn_sample_kernel(sinks_ref, q_ref, kn_ref, vn_ref, ck_ref, cv_ref, qm_ref, mk_ref, mv_ref,
                        osw_ref, omem_ref, *, bb, ts, start):
    scale = np.float32(HEAD_DIM ** -0.5)
    w = ck_ref.shape[1]
    nq = bb * ts
    q = q_ref[...].astype(BF16)
    kn = kn_ref[...].astype(BF16)
    vn = vn_ref[...].astype(BF16)
    ck = ck_ref[...].reshape(bb * w, SWA_KV_WIDTH).astype(BF16)
    cv = cv_ref[...].reshape(bb * w, SWA_KV_WIDTH).astype(BF16)

    rq = lax.broadcasted_iota(jnp.int32, (nq, bb * w), 0)
    cc = lax.broadcasted_iota(jnp.int32, (nq, bb * w), 1)
    qpos = start + _mod(rq, ts)
    kpos = start - w + _mod(cc, w)
    diff = qpos - kpos
    valid_c = (_div(rq, ts) == _div(cc, w)) & (diff >= 0) & (diff < WINDOW) & (kpos >= 0)
    rq = lax.broadcasted_iota(jnp.int32, (nq, nq), 0)
    cn = lax.broadcasted_iota(jnp.int32, (nq, nq), 1)
    diff = _mod(rq, ts) - _mod(cn, ts)
    valid_n = (_div(rq, ts) == _div(cn, ts)) & (diff >= 0) & (diff < WINDOW)

    outs = []
    for hq in range(SWA_Q_HEADS):
        kv = hq // SWA_REP
        qs = q[:, hq * HEAD_DIM:(hq + 1) * HEAD_DIM]
        ks = slice(kv * HEAD_DIM, (kv + 1) * HEAD_DIM)
        s_c = jnp.where(valid_c, _dot_nt(qs, ck[:, ks]) * scale, -jnp.inf)
        s_n = jnp.where(valid_n, _dot_nt(qs, kn[:, ks]) * scale, -jnp.inf)
        sink = sinks_ref[hq]
        m = jnp.maximum(jnp.maximum(jnp.max(s_c, axis=-1, keepdims=True),
                                    jnp.max(s_n, axis=-1, keepdims=True)), sink)
        p_c = jnp.exp(s_c - m)
        p_n = jnp.exp(s_n - m)
        den = (jnp.sum(p_c, axis=-1, keepdims=True) + jnp.sum(p_n, axis=-1, keepdims=True)
               + jnp.exp(sink - m))
        outs.append(_dot((p_c / den).astype(BF16), cv[:, ks]) + _dot((p_n / den).astype(BF16), vn[:, ks]))
    osw_ref[...] = jnp.concatenate(outs, axis=-1)

    qm = qm_ref[...].astype(BF16)
    mk = mk_ref[...].reshape(bb * MEM_TOKENS, MEM_WIDTH).astype(BF16)
    mv = mv_ref[...].reshape(bb * MEM_TOKENS, MEM_WIDTH).astype(BF16)
    rq = lax.broadcasted_iota(jnp.int32, (nq, bb * MEM_TOKENS), 0)
    cm = lax.broadcasted_iota(jnp.int32, (nq, bb * MEM_TOKENS), 1)
    valid_m = _div(rq, ts) == _div(cm, MEM_TOKENS)
    outs = []
    for h in range(MEM_HEADS):
        sl = slice(h * HEAD_DIM, (h + 1) * HEAD_DIM)
        s = jnp.where(valid_m, _dot_nt(qm[:, sl], mk[:, sl]) * scale, -jnp.inf)
        outs.append(_softmax_pv(s, mv[:, sl]))
    omem_ref[...] = jnp.concatenate(outs, axis=-1)


def _attn_sample(sinks, q, kn, vn, ck, cv, qm, cmk, cmv, *, ts, start):
    B, w, _ = ck.shape
    bb = 8
    rows = lambda wd: pl.BlockSpec((bb * ts, wd), lambda i: (i, 0))
    blk3 = lambda n, wd: pl.BlockSpec((bb, n, wd), lambda i: (i, 0, 0))
    return pl.pallas_call(
        functools.partial(_attn_sample_kernel, bb=bb, ts=ts, start=start),
        grid=(B // bb,),
        in_specs=[pl.BlockSpec(memory_space=pltpu.SMEM),
                  rows(SWA_WIDTH), rows(SWA_KV_WIDTH), rows(SWA_KV_WIDTH),
                  blk3(w, SWA_KV_WIDTH), blk3(w, SWA_KV_WIDTH), rows(MEM_WIDTH),
                  blk3(MEM_TOKENS, MEM_WIDTH), blk3(MEM_TOKENS, MEM_WIDTH)],
        out_specs=[rows(SWA_WIDTH), rows(MEM_WIDTH)],
        out_shape=[jax.ShapeDtypeStruct((B * ts, SWA_WIDTH), F32),
                   jax.ShapeDtypeStruct((B * ts, MEM_WIDTH), F32)],
        compiler_params=_cparams(("parallel",)),
        name="attn_sample",
    )(sinks, q, kn, vn, ck, cv, qm, cmk, cmv)


def _out_proj_kernel(x_ref, ossm_ref, osw_ref, omem_ref, wo_ref, g2_ref, h_ref, xn_ref):
    h = x_ref[0]
    h = h + _dot(ossm_ref[...].astype(BF16), wo_ref[0:512, :])
    h = h + _dot(osw_ref[0].astype(BF16), wo_ref[512:768, :])
    h = h + _dot(omem_ref[0].astype(BF16), wo_ref[768:1024, :])
    h_ref[0] = h
    xn_ref[0] = _rms(h, g2_ref[...]).astype(BF16)


def _out_proj(x, ossm, osw, omem, wo, g2):
    B, T, _ = x.shape
    tt = min(TOKEN_TILE, T)
    full = lambda shape: pl.BlockSpec(shape, lambda b, t: (0,) * len(shape))
    blk = lambda wd: pl.BlockSpec((1, tt, wd), lambda b, t: (b, t, 0))
    return pl.pallas_call(
        _out_proj_kernel,
        grid=(B, T // tt),
        in_specs=[blk(D_MODEL), pl.BlockSpec((tt, SSM_WIDTH), lambda b, t: (t, b)),
                  blk(SWA_WIDTH), blk(MEM_WIDTH), full((D_MODEL, D_MODEL)), full((1, D_MODEL))],
        out_specs=[blk(D_MODEL), blk(D_MODEL)],
        out_shape=[jax.ShapeDtypeStruct((B, T, D_MODEL), F32),
                   jax.ShapeDtypeStruct((B, T, D_MODEL), BF16)],
        compiler_params=_cparams(("parallel", "parallel")),
        name="out_proj",
    )(x, ossm, osw, omem, wo, g2)


def _top16(s, iota):
    work = s
    rank = jnp.full(s.shape, float(PEER_TOPK), F32)
    vals = []
    for a in range(PEER_TOPK):
        m = jnp.max(work, axis=0, keepdims=True)
        idx = jnp.min(jnp.where(work == m, iota, float(PEER_KEYS)), axis=0, keepdims=True)
        sel = iota == idx
        rank = jnp.where(sel, float(a), rank)
        work = jnp.where(sel, -jnp.inf, work)
        vals.append(m)
    return vals, rank


def _sort_pairs(n):
    pairs = []
    t = max(1, (n - 1).bit_length())
    p = 1 << (t - 1)
    while p > 0:
        q, r, d = 1 << (t - 1), 0, p
        while d > 0:
            pairs += [(i, i + d) for i in range(n - d) if (i & p) == r]
            d, q, r = q - p, q >> 1, p
        p >>= 1
    return pairs


_SORT16_PAIRS = _sort_pairs(PEER_TOPK)


def _cmp_exchange(xs, i, j):
    xs[i], xs[j] = jnp.maximum(xs[i], xs[j]), jnp.minimum(xs[i], xs[j])


def _top16_sorted(s):
    n = PEER_TOPK
    xs = [s[8 * k:8 * k + 8, :] for k in range(n)]
    for i, j in _SORT16_PAIRS:
        _cmp_exchange(xs, i, j)
    for shift in (4, 2, 1):
        other = [pltpu.roll(x, shift, 0) for x in xs]
        xs = [jnp.maximum(xs[k], other[n - 1 - k]) for k in range(n)]
        for d in (8, 4, 2, 1):
            for k in range(n):
                if k & d == 0:
                    _cmp_exchange(xs, k, k + d)
    return xs


def _candidates(v1, v2, cand_scr):
    L = v1[0].shape[1]
    cand_scr[...] = jnp.full((_CAND_ROWS, L), -jnp.inf, F32)
    for a in range(PEER_TOPK):
        for b in range(_CAND_COUNT[a]):
            cand_scr[pl.ds(_CAND_ROW0[a] + b, 1), :] = v1[a] + v2[b]
    return cand_scr[...], lax.broadcasted_iota(jnp.int32, (_CAND_ROWS, L), 0).astype(F32)


def _taken_by_index(cand, iota_c):
    work = cand
    taken = jnp.zeros(cand.shape, F32)
    for _ in range(PEER_TOPK):
        m = jnp.max(work, axis=0, keepdims=True)
        idx = jnp.min(jnp.where(work == m, iota_c, float(_CAND_ROWS)), axis=0, keepdims=True)
        sel = iota_c == idx
        taken = jnp.where(sel, 1.0, taken)
        work = jnp.where(sel, -jnp.inf, work)
    return taken


_SORT8_PAIRS = _sort_pairs(8)


def _candidates_sorted(cand):
    n = PEER_TOPK
    xs = [cand[8 * k:8 * k + 8, :] for k in range(_CAND_ROWS // 8)]
    xs.append(jnp.full_like(xs[0], -jnp.inf))
    for i, j in _SORT8_PAIRS:
        _cmp_exchange(xs, i, j)
    other = [pltpu.roll(x, 4, 0) for x in xs]
    xs = xs + other[::-1]
    for shift in (4, 2, 1):
        if shift != 4:
            other = [pltpu.roll(x, shift, 0) for x in xs]
            xs = [jnp.maximum(xs[k], other[n - 1 - k]) for k in range(n)]
        for d in (8, 4, 2, 1):
            for k in range(n):
                if k & d == 0:
                    _cmp_exchange(xs, k, k + d)
    return xs


def _group_counts(taken, iota_c):
    cnt = []
    for a in range(PEER_TOPK):
        lo, hi = _CAND_ROW0[a], _CAND_ROW0[a] + _CAND_COUNT[a]
        if hi - lo == 1:
            cnt.append(taken[lo:hi])
            continue
        t0, t1 = (lo // 8) * 8, -(-hi // 8) * 8
        part = taken[t0:t1]
        if (lo, hi) != (t0, t1):
            rows = iota_c[t0:t1]
            part = jnp.where((rows >= float(lo)) & (rows < float(hi)), part, 0.0)
        cnt.append(jnp.sum(part, axis=0, keepdims=True))
    return cnt


def _route_tiles(tiles):
    n = PEER_TOPK
    states = []
    bad_any = None
    for s1, s2, cand_scr, n1_out, r2_out in tiles:
        t1 = _top16_sorted(s1)
        t2 = _top16_sorted(s2)
        v1 = [t[0:1, :] for t in t1]
        v2 = [t[0:1, :] for t in t2]
        cand, iota_c = _candidates(v1, v2, cand_scr)
        tc = _candidates_sorted(cand)
        z = jnp.ones_like(v1[0])
        for k in range(1, n):
            z = z + jnp.exp(tc[k][0:1, :] - tc[0][0:1, :])
        taken = jnp.where(cand >= tc[n - 1][0:1, :], 1.0, 0.0)
        cnt = _group_counts(taken, iota_c)

        n_taken = jnp.sum(taken, axis=0, keepdims=True)
        bad = jnp.broadcast_to(jnp.where(n_taken != float(n), 1.0, 0.0), (8, s1.shape[1]))
        for s, t in ((s1, t1), (s2, t2)):
            gap = t[0] - t[1]
            for a in range(1, n - 1):
                gap = jnp.minimum(gap, t[a] - t[a + 1])
            at_least = jnp.zeros((8, s.shape[1]), F32)
            for k in range(n):
                at_least = at_least + jnp.where(s[8 * k:8 * k + 8, :] >= t[n - 1], 1.0, 0.0)
            at_least = jnp.sum(at_least, axis=0, keepdims=True)
            bad = jnp.where((gap <= 0.0) | (at_least != float(n)), 1.0, bad)
        bad_any = bad if bad_any is None else jnp.maximum(bad_any, bad)
        states.append((s1, s2, t1, t2, v1, v2, cand, iota_c, cnt, z, n1_out, r2_out))
    has_ties = jnp.max(bad_any) > 0.0

    @pl.when(jnp.logical_not(has_ties))
    def _by_value():
        for s1, s2, t1, t2, _, _, _, _, cnt, _, n1_out, r2_out in states:
            cnt_b = [jnp.broadcast_to(c, (8, c.shape[1])) for c in cnt]
            ranks = []
            for k in range(n):
                x1 = s1[8 * k:8 * k + 8, :]
                x2 = s2[8 * k:8 * k + 8, :]
                n1 = cnt_b[0]
                r2 = jnp.zeros_like(x2)
                for a in range(n):
                    n1 = jnp.where(t1[a] > x1, cnt_b[a + 1] if a + 1 < n else 0.0, n1)
                    r2 = jnp.where(t2[a] > x2, float(a + 1), r2)
                n1_out[8 * k:8 * k + 8, :] = n1
                ranks.append(r2)
            r2_out[...] = jnp.concatenate(ranks, axis=0).astype(r2_out.dtype)

    @pl.when(has_ties)
    def _by_index():
        for s1, s2, _, _, _, _, cand, iota_c, _, _, n1_out, r2_out in states:
            iota = lax.broadcasted_iota(jnp.int32, s1.shape, 0).astype(F32)
            _, r1 = _top16(s1, iota)
            _, r2 = _top16(s2, iota)
            cnt_exact = _group_counts(_taken_by_index(cand, iota_c), iota_c)
            n1 = jnp.zeros(s1.shape, F32)
            for a in range(n):
                n1 = jnp.where(r1 == float(a), cnt_exact[a], n1)
            n1_out[...] = n1
            r2_out[...] = r2.astype(r2_out.dtype)

    return [(jnp.exp(st[0] - st[4][0]) / st[9], jnp.exp(st[1] - st[5][0])) for st in states]


def _peer_kernel(xn_ref, h_ref, wqt_ref, k1_ref, k2_ref, u_ref, vt_ref, y_ref,
                 n1_scr, c1_scr, s2_scr, r2_scr, e2_scr, acc_scr, a_scr, w_scr,
                 q_scr, xt_scr, cand_scr, *, n_steps):
    g = pl.program_id(1)
    ng = n_steps
    tt = xn_ref.shape[0]
    eb = u_ref.shape[0]
    keys_per_block = eb // PEER_KEYS
    n_lane_tiles = tt // LANES

    @pl.when(g == 0)
    def _route():
        xt_scr[...] = xn_ref[...].T
        xt = xt_scr[...]
        q_scr[...] = _dot(wqt_ref[...], xt).astype(BF16)
        for h in range(PEER_HEADS):
            for side, dst in enumerate((n1_scr, s2_scr)):
                k_ref = (k1_ref, k2_ref)[side]
                r0 = h * 2 * PEER_HALF + side * PEER_HALF
                s = _dot(k_ref[...], q_scr[r0:r0 + PEER_HALF, :])
                for c in range(n_lane_tiles):
                    dst[h, c] = s[:, c * LANES:(c + 1) * LANES]

        per_trip = PEER_ROUTE_TILES
        trips_per_head = n_lane_tiles // per_trip

        def body(i, carry):
            h = i // trips_per_head
            cs = [(i % trips_per_head) * per_trip + k for k in range(per_trip)]
            outs = _route_tiles([(n1_scr[h, c], s2_scr[h, c], cand_scr.at[k], n1_scr.at[h, c], r2_scr.at[h, c])
                                 for k, c in enumerate(cs)])
            for c, (c1, e2) in zip(cs, outs):
                c1_scr[h, c] = c1
                e2_scr[h, c] = e2.astype(BF16)
            return carry

        lax.fori_loop(0, PEER_HEADS * trips_per_head, body, 0)
        acc_scr[...] = jnp.zeros_like(acc_scr)

    def gate(a_scr, w_scr, block):
        zero = jnp.zeros((), BF16)
        group = PEER_GATE_KEYS
        for k0 in range(0, keys_per_block, group):
            def lane_tile(c, carry, k0=k0):
                lanes = pl.ds(pl.multiple_of(c * LANES, LANES), LANES)
                gts = [None] * group
                for h in range(PEER_HEADS):
                    r2 = r2_scr[h, c].reshape(PEER_KEYS // 16, 16, LANES)
                    e2 = e2_scr[h, c].reshape(PEER_KEYS // 16, 16, LANES)
                    for ii in range(group):
                        row = pl.ds(block * keys_per_block + k0 + ii, 1)
                        n_b = jnp.broadcast_to(n1_scr[h, c, row, :], (16, LANES)).astype(BF16)[None]
                        c_b = jnp.broadcast_to(c1_scr[h, c, row, :], (16, LANES)).astype(BF16)[None]
                        term = jnp.where(r2 < n_b, e2, zero) * c_b
                        gts[ii] = term if h == 0 else gts[ii] + term
                for ii in range(group):
                    r0 = (k0 + ii) * PEER_KEYS
                    rows = slice(r0, r0 + PEER_KEYS)
                    act = _gelu(a_scr[rows, lanes].astype(BF16))
                    w_scr[rows, lanes] = gts[ii].reshape(PEER_KEYS, LANES) * act
                return carry

            lax.fori_loop(0, n_lane_tiles, lane_tile, 0)

    @pl.when(g < ng)
    def _scores():
        a_scr[...] = _dot(u_ref[...], xt_scr[...])

    @pl.when(g > 0)
    def _v_product():
        acc_scr[...] += _dot(vt_ref[...], w_scr[...])

    @pl.when(g < ng)
    def _weights():
        gate(a_scr, w_scr, g)

    @pl.when(g == ng)
    def _fin():
        y_ref[...] = h_ref[...] + acc_scr[...].T


def _peer(xn, h, wqt, k1, k2, u_tab, vt_tab):
    n = xn.shape[0]
    tt = min(TOKEN_TILE, n)
    eb = PEER_EXPERT_BLOCK
    ng = PEER_EXPERTS // eb
    full = lambda shape: pl.BlockSpec(shape, lambda i, g: (0,) * len(shape), pipeline_mode=pl.Buffered(1))
    tok = pl.BlockSpec((tt, D_MODEL), lambda i, g: (i, 0))
    head_f32 = pltpu.VMEM((PEER_HEADS, tt // LANES, PEER_KEYS, LANES), F32)
    head_bf16 = pltpu.VMEM((PEER_HEADS, tt // LANES, PEER_KEYS, LANES), BF16)
    return pl.pallas_call(
        functools.partial(_peer_kernel, n_steps=ng),
        grid=(n // tt, ng + 1),
        in_specs=[tok, tok, full((2 * PEER_HEADS * PEER_HALF, D_MODEL)),
                  full((PEER_KEYS, PEER_HALF)), full((PEER_KEYS, PEER_HALF)),
                  pl.BlockSpec((eb, D_MODEL), lambda i, g: (jnp.minimum(g, ng - 1), 0)),
                  pl.BlockSpec((D_MODEL, eb), lambda i, g: (0, jnp.maximum(g - 1, 0)))],
        out_specs=tok,
        out_shape=jax.ShapeDtypeStruct((n, D_MODEL), F32),
        scratch_shapes=[head_f32, head_f32, head_f32, head_bf16, head_bf16,
                        pltpu.VMEM((D_MODEL, tt), F32),
                        pltpu.VMEM((eb, tt), F32), pltpu.VMEM((eb, tt), BF16),
                        pltpu.VMEM((2 * PEER_HEADS * PEER_HALF, tt), BF16),
                        pltpu.VMEM((D_MODEL, tt), BF16),
                        pltpu.VMEM((PEER_ROUTE_TILES, _CAND_ROWS, LANES), F32)],
        compiler_params=_cparams(("parallel", "arbitrary")),
        name="peer",
    )(xn, h, wqt, k1, k2, u_tab, vt_tab)


def _rope_tables(pos):
    half = HEAD_DIM // 2
    inv = ROPE_THETA ** (-jnp.arange(half, dtype=F32) / half)
    ang = pos.astype(F32)[:, None] * inv[None, :]
    cos = jnp.cos(ang)
    sin = jnp.sin(ang)
    cos = jnp.tile(jnp.concatenate([cos, cos], axis=-1), (1, SWA_Q_HEADS))
    sin = jnp.tile(jnp.concatenate([-sin, sin], axis=-1), (1, SWA_Q_HEADS))
    return cos, sin


def _ssm_params(log_dt, a_re, a_im, b_re, b_im, c_re, c_im):
    dt = jnp.exp(log_dt)
    mag = jnp.exp(a_re * dt)
    lam_re = mag * jnp.cos(a_im * dt)
    lam_im = mag * jnp.sin(a_im * dt)
    den = a_re * a_re + a_im * a_im
    z_re = ((lam_re - 1.0) * a_re + lam_im * a_im) / den
    z_im = (lam_im * a_re - (lam_re - 1.0) * a_im) / den
    bb_re = z_re[..., None] * b_re - z_im[..., None] * b_im
    bb_im = z_re[..., None] * b_im + z_im[..., None] * b_re
    hg = SSM_HALF_GROUPS
    eye = jnp.eye(hg, dtype=F32)
    bb = jnp.stack([bb_re, bb_im]).reshape(2, 2, hg, SSM_STATE, SSM_GROUP)
    bmat = jnp.einsum('rjgnc,gh->jgcrhn', bb, eye).reshape(2, hg * SSM_GROUP, 2 * SSM_HALF_STATE)
    cc = jnp.stack([c_re, -c_im]).reshape(2, 2, hg, SSM_GROUP, SSM_STATE)
    cmat = jnp.einsum('rjgcn,gh->jrgnhc', cc, eye).reshape(2, 2 * SSM_HALF_STATE, hg * SSM_GROUP)
    lam = jnp.stack([lam_re.reshape(2, SSM_HALF_STATE), lam_im.reshape(2, SSM_HALF_STATE)], axis=1)
    return lam.reshape(1, SSM_COLS), bmat.astype(BF16), cmat.astype(BF16)


def _state_to_cols(s_re, s_im):
    b = s_re.shape[0]
    st = jnp.stack([s_re.reshape(b, 2, SSM_HALF_STATE), s_im.reshape(b, 2, SSM_HALF_STATE)], axis=2)
    return st.reshape(b, SSM_COLS)


def _cols_to_state(cols):
    b = cols.shape[0]
    st = cols.reshape(b, 2, 2, SSM_HALF_STATE)
    return (st[:, :, 0].reshape(b, SSM_GROUPS, SSM_STATE), st[:, :, 1].reshape(b, SSM_GROUPS, SSM_STATE))


def kernel(x_prompt, x_sample, state_ssm_re, state_ssm_im, cache_win_k, cache_win_v, cache_mem_k, cache_mem_v, mem_prompt, norm1_g, w_in, ssm_log_dt, ssm_a_re, ssm_a_im, ssm_b_re, ssm_b_im, ssm_c_re, ssm_c_im, ssm_d, w_glu, b_glu, swa_q_norm, swa_k_norm, swa_sinks, mem_norm_g, w_mem_kv, mem_q_norm, mem_k_norm, w_out, norm2_g, peer_wq, peer_k1, peer_k2, peer_u, peer_v):
    depth = norm1_g.shape[0]
    assert depth == 1
    l = 0
    B, T, _ = x_prompt.shape
    SB, ST, _ = x_sample.shape
    w = cache_win_k.shape[2]

    row = lambda a: a.reshape(1, -1)
    g1 = row(norm1_g[l])
    g2 = row(norm2_g[l])
    win = w_in[l].astype(BF16)
    wo = w_out[l].astype(BF16)
    wglu = w_glu[l].astype(BF16)
    bglu = row(b_glu[l])
    gq = row(jnp.tile(swa_q_norm[l], SWA_Q_HEADS))
    gk = row(jnp.tile(swa_k_norm[l], SWA_KV_HEADS))
    gm = row(jnp.tile(mem_q_norm[l], MEM_HEADS))
    gmk = row(jnp.tile(mem_k_norm[l], MEM_HEADS))
    gmem = row(mem_norm_g[l])
    wkv = w_mem_kv[l].astype(BF16)
    sinks = swa_sinks[l]
    head_id = np.arange(SWA_WIDTH) // HEAD_DIM
    ones = jnp.asarray(head_id[:, None] == head_id[None, :], dtype=BF16)
    lam, bmat, cmat = _ssm_params(ssm_log_dt[l], ssm_a_re[l], ssm_a_im[l], ssm_b_re[l], ssm_b_im[l],
                                  ssm_c_re[l], ssm_c_im[l])
    dskip = row(ssm_d[l])
    wqt = peer_wq[l].astype(BF16).T
    k1 = peer_k1[l].astype(BF16)
    k2 = peer_k2[l].astype(BF16)
    u_tab = peer_u[l].astype(BF16)
    vt_tab = peer_v[l].astype(BF16).T

    cos_p, sin_p = _rope_tables(jnp.arange(T, dtype=jnp.int32))
    u_p, q_p, k_p, v_p, qm_p = _in_proj(x_prompt, cos_p, sin_p, g1, win, ones, gq, gk, gm)
    zeros = jnp.zeros((B, SSM_COLS), F32)
    ossm_p, sfin_p = _s5(u_p.reshape(T * B, SSM_WIDTH), zeros, lam, bmat, cmat, dskip, wglu, bglu,
                         bt=B, tt=S5_TIME_TILE)
    mk, mv = _mem_kv(mem_prompt.reshape(B * MEM_TOKENS, D_MODEL), gmem, wkv, ones, gmk)
    mk = mk.reshape(B, MEM_TOKENS, MEM_WIDTH)
    mv = mv.reshape(B, MEM_TOKENS, MEM_WIDTH)
    osw_p, omem_p = _attn_prompt(sinks, q_p, k_p, v_p, qm_p, mk, mv)
    h_p, xn_p = _out_proj(x_prompt, ossm_p.reshape(T, B * SSM_WIDTH), osw_p, omem_p, wo, g2)
    y_p = _peer(xn_p.reshape(B * T, D_MODEL), h_p.reshape(B * T, D_MODEL), wqt, k1, k2, u_tab, vt_tab)
    y_p = y_p.reshape(B, T, D_MODEL)
    p_sr, p_si = _cols_to_state(sfin_p)
    p_wk = k_p[:, T - w:].reshape(B, w, SWA_KV_HEADS, HEAD_DIM)
    p_wv = v_p[:, T - w:].reshape(B, w, SWA_KV_HEADS, HEAD_DIM)
    p_mk = mk.reshape(B, MEM_TOKENS, MEM_HEADS, HEAD_DIM)
    p_mv = mv.reshape(B, MEM_TOKENS, MEM_HEADS, HEAD_DIM)

    n_s = SB * ST
    pos_s = PAST_LEN + jnp.tile(jnp.arange(ST, dtype=jnp.int32), SB)
    cos_s, sin_s = _rope_tables(pos_s)
    xs = x_sample.reshape(1, n_s, D_MODEL)
    u_s, q_s, k_s, v_s, qm_s = _in_proj(xs, cos_s, sin_s, g1, win, ones, gq, gk, gm)
    u_tm = u_s.reshape(SB, ST, SSM_WIDTH).transpose(1, 0, 2).reshape(n_s, SSM_WIDTH)
    ossm_tm, sfin_s = _s5(u_tm, _state_to_cols(state_ssm_re[l], state_ssm_im[l]), lam, bmat, cmat,
                          dskip, wglu, bglu, bt=SB, tt=ST)
    ossm_s = ossm_tm.reshape(ST, SB, SSM_WIDTH).transpose(1, 0, 2).reshape(n_s, SSM_WIDTH)
    ck = cache_win_k[l].reshape(SB, w, SWA_KV_WIDTH)
    cv = cache_win_v[l].reshape(SB, w, SWA_KV_WIDTH)
    q_s2 = q_s.reshape(n_s, SWA_WIDTH)
    k_s2 = k_s.reshape(n_s, SWA_KV_WIDTH)
    v_s2 = v_s.reshape(n_s, SWA_KV_WIDTH)
    osw_s, omem_s = _attn_sample(sinks, q_s2, k_s2, v_s2, ck, cv, qm_s.reshape(n_s, MEM_WIDTH),
                                 cache_mem_k[l].reshape(SB, MEM_TOKENS, MEM_WIDTH),
                                 cache_mem_v[l].reshape(SB, MEM_TOKENS, MEM_WIDTH),
                                 ts=ST, start=PAST_LEN)
    h_s, xn_s = _out_proj(xs, ossm_s, osw_s.reshape(1, n_s, SWA_WIDTH), omem_s.reshape(1, n_s, MEM_WIDTH),
                          wo, g2)
    y_s = _peer(xn_s.reshape(n_s, D_MODEL), h_s.reshape(n_s, D_MODEL), wqt, k1, k2, u_tab, vt_tab)
    y_s = y_s.reshape(SB, ST, D_MODEL)
    s_sr, s_si = _cols_to_state(sfin_s)
    s_wk = jnp.concatenate([ck, k_s2.reshape(SB, ST, SWA_KV_WIDTH)], axis=1)[:, -w:]
    s_wv = jnp.concatenate([cv, v_s2.reshape(SB, ST, SWA_KV_WIDTH)], axis=1)[:, -w:]
    s_wk = s_wk.reshape(SB, w, SWA_KV_HEADS, HEAD_DIM)
    s_wv = s_wv.reshape(SB, w, SWA_KV_HEADS, HEAD_DIM)

    st = lambda a: a[None]
    return (y_p, y_s, st(p_sr), st(p_si), st(p_wk), st(p_wv), st(p_mk), st(p_mv),
            st(s_sr), st(s_si), st(s_wk), st(s_wv))
```

```python
import functools
import math

import jax
import jax.numpy as jnp
import numpy as np
from jax import lax
from jax.experimental import pallas as pl
from jax.experimental.pallas import tpu as pltpu

F32 = jnp.float32
BF16 = jnp.bfloat16

D_MODEL = 1024
HEAD_DIM = 64
EPS = 1e-6
ROPE_THETA = 10000.0
PAST_LEN = 8192
SSM_WIDTH = 512
SSM_GROUP = 16
SSM_GROUPS = 32
SSM_STATE = 64
SSM_HALF_GROUPS = SSM_GROUPS // 2
SSM_HALF_STATE = SSM_HALF_GROUPS * SSM_STATE
SSM_COLS = 2 * 2 * SSM_HALF_STATE
SWA_Q_HEADS = 4
SWA_KV_HEADS = 2
SWA_REP = 2
SWA_WIDTH = 256
SWA_KV_WIDTH = 128
WINDOW = 128
MEM_TOKENS = 256
MEM_HEADS = 4
MEM_WIDTH = 256
IN_WIDTH = 1280
PEER_HEADS = 8
PEER_KEYS = 128
PEER_EXPERTS = PEER_KEYS * PEER_KEYS
PEER_TOPK = 16
PEER_HALF = 128

LANES = 128
VMEM_LIMIT = 60 * 1024 * 1024

TOKEN_TILE = 512
PEER_EXPERT_BLOCK = 2048
PEER_GATE_KEYS = 4
PEER_ROUTE_TILES = 2
S5_TIME_TILE = 64
ATTN_BLOCKS = 2

_CAND_COUNT = [PEER_TOPK // (a + 1) for a in range(PEER_TOPK)]
_CAND_ROW0 = [0, 16, 24, 32, 36, 40, 42, 44, 48, 49, 50, 51, 52, 53, 54, 55]
_CAND_ROWS = 56


def _cparams(sem):
    return pltpu.CompilerParams(dimension_semantics=sem, vmem_limit_bytes=VMEM_LIMIT)


def _rms(x, g):
    return x * lax.rsqrt(jnp.mean(x * x, axis=-1, keepdims=True) + EPS) * g


def _gelu(x):
    return 0.5 * x * (1.0 + lax.erf(x * math.sqrt(0.5)))


def _dot(a, b):
    return jnp.dot(a, b, preferred_element_type=F32)


def _dot_nt(a, b):
    return lax.dot_general(a, b, (((1,), (1,)), ((), ())), preferred_element_type=F32)


def _div(x, n):
    return x >> (n.bit_length() - 1) if n & (n - 1) == 0 else x // n


def _mod(x, n):
    return x & (n - 1) if n & (n - 1) == 0 else x % n


def _head_rms(x, ones_bd, g):
    sq = x * x
    hi = sq.astype(BF16)
    lo = (sq - hi.astype(F32)).astype(BF16)
    ms = (_dot(hi, ones_bd) + _dot(lo, ones_bd)) * np.float32(1.0 / HEAD_DIM)
    return x * lax.rsqrt(ms + EPS) * g


def _rope(x, cos, sin_signed):
    w = x.shape[-1]
    lane = lax.broadcasted_iota(jnp.int32, x.shape, 1)
    first_half = _mod(lane, HEAD_DIM) < (HEAD_DIM // 2)
    partner = jnp.where(first_half, pltpu.roll(x, w - HEAD_DIM // 2, 1),
                        pltpu.roll(x, HEAD_DIM // 2, 1))
    return x * cos + partner * sin_signed


def _in_proj_kernel(x_ref, g1_ref, win_ref, ones_ref, gq_ref, gk_ref, gm_ref, cos_ref, sin_ref,
                    u_ref, q_ref, k_ref, v_ref, qm_ref):
    x = x_ref[0]
    xn = _rms(x, g1_ref[...])
    proj = _dot(xn.astype(BF16), win_ref[...])
    u_ref[...] = proj[:, :SSM_WIDTH]
    q = proj[:, 512:768]
    k = proj[:, 768:896]
    v_ref[0] = proj[:, 896:1024]
    qm = proj[:, 1024:1280]
    ones = ones_ref[...]
    cos = cos_ref[...]
    sin = sin_ref[...]
    q_ref[0] = _rope(_head_rms(q, ones, gq_ref[...]), cos, sin)
    k_ref[0] = _rope(_head_rms(k, ones[:SWA_KV_WIDTH, :SWA_KV_WIDTH], gk_ref[...]),
                     cos[:, :SWA_KV_WIDTH], sin[:, :SWA_KV_WIDTH])
    qm_ref[0] = _head_rms(qm, ones, gm_ref[...])


def _in_proj(x, cos, sin, g1, win, ones, gq, gk, gm):
    B, T, _ = x.shape
    tt = min(TOKEN_TILE, T)
    grid = (B, T // tt)
    full = lambda shape: pl.BlockSpec(shape, lambda b, t: (0,) * len(shape))
    return pl.pallas_call(
        _in_proj_kernel,
        grid=grid,
        in_specs=[
            pl.BlockSpec((1, tt, D_MODEL), lambda b, t: (b, t, 0)),
            full((1, D_MODEL)), full((D_MODEL, IN_WIDTH)), full((SWA_WIDTH, SWA_WIDTH)),
            full((1, SWA_WIDTH)), full((1, SWA_KV_WIDTH)), full((1, MEM_WIDTH)),
            pl.BlockSpec((tt, SWA_WIDTH), lambda b, t: (t, 0)),
            pl.BlockSpec((tt, SWA_WIDTH), lambda b, t: (t, 0)),
        ],
        out_specs=[
            pl.BlockSpec((tt, SSM_WIDTH), lambda b, t: (t, b)),
            pl.BlockSpec((1, tt, SWA_WIDTH), lambda b, t: (b, t, 0)),
            pl.BlockSpec((1, tt, SWA_KV_WIDTH), lambda b, t: (b, t, 0)),
            pl.BlockSpec((1, tt, SWA_KV_WIDTH), lambda b, t: (b, t, 0)),
            pl.BlockSpec((1, tt, MEM_WIDTH), lambda b, t: (b, t, 0)),
        ],
        out_shape=[
            jax.ShapeDtypeStruct((T, B * SSM_WIDTH), F32),
            jax.ShapeDtypeStruct((B, T, SWA_WIDTH), F32),
            jax.ShapeDtypeStruct((B, T, SWA_KV_WIDTH), F32),
            jax.ShapeDtypeStruct((B, T, SWA_KV_WIDTH), F32),
            jax.ShapeDtypeStruct((B, T, MEM_WIDTH), F32),
        ],
        compiler_params=_cparams(("parallel", "parallel")),
        name="in_proj",
    )(x, g1, win, ones, gq, gk, gm, cos, sin)


def _s5_kernel(u_ref, s0_ref, lam_ref, bmat_ref, cmat_ref, d_ref, wglu_ref, bglu_ref,
               o_ref, sfin_ref, s_scr, carry_scr, *, bt, tt):
    @pl.when(pl.program_id(0) == 0)
    def _():
        carry_scr[...] = s0_ref[...]

    u = u_ref[...]
    ub = u.astype(BF16)
    hw = 2 * SSM_HALF_STATE
    for j in range(2):
        s_scr[:, j * hw:(j + 1) * hw] = _dot(ub[:, j * 256:(j + 1) * 256], bmat_ref[j])

    def step(t, carry):
        r0 = pl.multiple_of(t * bt, bt)
        for j in range(2):
            c_re = pl.ds(j * hw, SSM_HALF_STATE)
            c_im = pl.ds(j * hw + SSM_HALF_STATE, SSM_HALF_STATE)
            p_re = carry_scr[:, c_re]
            p_im = carry_scr[:, c_im]
            l_re = lam_ref[:, c_re]
            l_im = lam_ref[:, c_im]
            n_re = l_re * p_re - l_im * p_im + s_scr[pl.ds(r0, bt), c_re]
            n_im = l_re * p_im + l_im * p_re + s_scr[pl.ds(r0, bt), c_im]
            s_scr[pl.ds(r0, bt), c_re] = n_re
            s_scr[pl.ds(r0, bt), c_im] = n_im
            carry_scr[:, c_re] = n_re
            carry_scr[:, c_im] = n_im
        return carry

    lax.fori_loop(0, tt, step, 0)
    sfin_ref[...] = carry_scr[...]

    ys = [_dot(s_scr[:, j * hw:(j + 1) * hw].astype(BF16), cmat_ref[j]) for j in range(2)]
    y = jnp.concatenate(ys, axis=-1) + d_ref[...] * u
    y = _gelu(y)
    z = _dot(y.astype(BF16), wglu_ref[...]) + bglu_ref[...]
    o_ref[...] = y * jax.nn.sigmoid(z)


def _s5(u_tm, s0, lam, bmat, cmat, d, wglu, bglu, *, bt, tt):
    rows = u_tm.shape[0]
    nt = rows // (bt * tt)
    full = lambda shape: pl.BlockSpec(shape, lambda t: (0,) * len(shape))
    return pl.pallas_call(
        functools.partial(_s5_kernel, bt=bt, tt=tt),
        grid=(nt,),
        in_specs=[
            pl.BlockSpec((bt * tt, SSM_WIDTH), lambda t: (t, 0)),
            full((bt, SSM_COLS)), full((1, SSM_COLS)),
            full((2, 256, 2 * SSM_HALF_STATE)), full((2, 2 * SSM_HALF_STATE, 256)),
            full((1, SSM_WIDTH)), full((SSM_WIDTH, SSM_WIDTH)), full((1, SSM_WIDTH)),
        ],
        out_specs=[
            pl.BlockSpec((bt * tt, SSM_WIDTH), lambda t: (t, 0)),
            full((bt, SSM_COLS)),
        ],
        out_shape=[
            jax.ShapeDtypeStruct((rows, SSM_WIDTH), F32),
            jax.ShapeDtypeStruct((bt, SSM_COLS), F32),
        ],
        scratch_shapes=[pltpu.VMEM((bt * tt, SSM_COLS), F32), pltpu.VMEM((bt, SSM_COLS), F32)],
        compiler_params=_cparams(("arbitrary",)),
        name="s5_mixer",
    )(u_tm, s0, lam, bmat, cmat, d, wglu, bglu)


def _mem_kv_kernel(m_ref, g_ref, w_ref, ones_ref, gk_ref, k_ref, v_ref):
    xn = _rms(m_ref[...], g_ref[...])
    kv = _dot(xn.astype(BF16), w_ref[...])
    k_ref[...] = _head_rms(kv[:, :MEM_WIDTH], ones_ref[...], gk_ref[...])
    v_ref[...] = kv[:, MEM_WIDTH:]


def _mem_kv(mem_rows, g, w, ones, gk):
    rows = mem_rows.shape[0]
    tt = min(TOKEN_TILE, rows)
    full = lambda shape: pl.BlockSpec(shape, lambda t: (0,) * len(shape))
    return pl.pallas_call(
        _mem_kv_kernel,
        grid=(rows // tt,),
        in_specs=[pl.BlockSpec((tt, D_MODEL), lambda t: (t, 0)), full((1, D_MODEL)),
                  full((D_MODEL, 2 * MEM_WIDTH)), full((MEM_WIDTH, MEM_WIDTH)), full((1, MEM_WIDTH))],
        out_specs=[pl.BlockSpec((tt, MEM_WIDTH), lambda t: (t, 0))] * 2,
        out_shape=[jax.ShapeDtypeStruct((rows, MEM_WIDTH), F32)] * 2,
        compiler_params=_cparams(("parallel",)),
        name="mem_kv",
    )(mem_rows, g, w, ones, gk)


def _softmax_pv(s, v_b, sink=None):
    m = jnp.max(s, axis=-1, keepdims=True)
    if sink is not None:
        m = jnp.maximum(m, sink)
    p = jnp.exp(s - m)
    den = jnp.sum(p, axis=-1, keepdims=True)
    if sink is not None:
        den = den + jnp.exp(sink - m)
    return _dot((p / den).astype(BF16), v_b)


def _attn_prompt_kernel(sinks_ref, q_ref, kp_ref, kc_ref, vp_ref, vc_ref, qm_ref, mk_ref, mv_ref,
                        osw_ref, omem_ref):
    n = pl.program_id(1)
    scale = np.float32(HEAD_DIM ** -0.5)
    q = q_ref[0].astype(BF16)
    kk = jnp.concatenate([kp_ref[0], kc_ref[0]], axis=0).astype(BF16)
    vv = jnp.concatenate([vp_ref[0], vc_ref[0]], axis=0).astype(BF16)
    qi = lax.broadcasted_iota(jnp.int32, (WINDOW, 2 * WINDOW), 0)
    ki = lax.broadcasted_iota(jnp.int32, (WINDOW, 2 * WINDOW), 1) - WINDOW
    diff = qi - ki
    for j in range(ATTN_BLOCKS):
        valid = (diff >= 0) & (diff < WINDOW) & ((n * ATTN_BLOCKS + j) * WINDOW + ki >= 0)
        rows = slice(j * WINDOW, (j + 1) * WINDOW)
        keys = slice(j * WINDOW, (j + 2) * WINDOW)
        outs = []
        for hq in range(SWA_Q_HEADS):
            kv = hq // SWA_REP
            s = _dot_nt(q[rows, hq * HEAD_DIM:(hq + 1) * HEAD_DIM],
                        kk[keys, kv * HEAD_DIM:(kv + 1) * HEAD_DIM]) * scale
            s = jnp.where(valid, s, -jnp.inf)
            outs.append(_softmax_pv(s, vv[keys, kv * HEAD_DIM:(kv + 1) * HEAD_DIM], sinks_ref[hq]))
        osw_ref[0, rows, :] = jnp.concatenate(outs, axis=-1)

    qm = qm_ref[0].astype(BF16)
    mk = mk_ref[0].astype(BF16)
    mv = mv_ref[0].astype(BF16)
    outs = []
    for h in range(MEM_HEADS):
        sl = slice(h * HEAD_DIM, (h + 1) * HEAD_DIM)
        s = _dot_nt(qm[:, sl], mk[:, sl]) * scale
        outs.append(_softmax_pv(s, mv[:, sl]))
    omem_ref[0] = jnp.concatenate(outs, axis=-1)


def _attn_prompt(sinks, q, k, v, qm, mk, mv):
    B, T, _ = q.shape
    rows = ATTN_BLOCKS * WINDOW
    blk = lambda w: pl.BlockSpec((1, rows, w), lambda b, n: (b, n, 0))
    prev = lambda w: pl.BlockSpec((1, WINDOW, w), lambda b, n: (b, jnp.maximum(n * ATTN_BLOCKS - 1, 0), 0))
    memb = pl.BlockSpec((1, MEM_TOKENS, MEM_WIDTH), lambda b, n: (b, 0, 0))
    return pl.pallas_call(
        _attn_prompt_kernel,
        grid=(B, T // rows),
        in_specs=[pl.BlockSpec(memory_space=pltpu.SMEM),
                  blk(SWA_WIDTH), prev(SWA_KV_WIDTH), blk(SWA_KV_WIDTH), prev(SWA_KV_WIDTH),
                  blk(SWA_KV_WIDTH), blk(MEM_WIDTH), memb, memb],
        out_specs=[blk(SWA_WIDTH), blk(MEM_WIDTH)],
        out_shape=[jax.ShapeDtypeStruct((B, T, SWA_WIDTH), F32),
                   jax.ShapeDtypeStruct((B, T, MEM_WIDTH), F32)],
        compiler_params=_cparams(("parallel", "parallel")),
        name="attn_prompt",
    )(sinks, q, k, k, v, v, qm, mk, mv)


def _attn_sample_kernel(sinks_ref, q_ref, kn_ref, vn_ref, ck_ref, cv_ref, qm_ref, mk_ref, mv_ref,
                        osw_ref, omem_ref, *, bb, ts, start):
    scale = np.float32(HEAD_DIM ** -0.5)
    w = ck_ref.shape[1]
    nq = bb * ts
    q = q_ref[...].astype(BF16)
    kn = kn_ref[...].astype(BF16)
    vn = vn_ref[...].astype(BF16)
    ck = ck_ref[...].reshape(bb * w, SWA_KV_WIDTH).astype(BF16)
    cv = cv_ref[...].reshape(bb * w, SWA_KV_WIDTH).astype(BF16)

    rq = lax.broadcasted_iota(jnp.int32, (nq, bb * w), 0)
    cc = lax.broadcasted_iota(jnp.int32, (nq, bb * w), 1)
    qpos = start + _mod(rq, ts)
    kpos = start - w + _mod(cc, w)
    diff = qpos - kpos
    valid_c = (_div(rq, ts) == _div(cc, w)) & (diff >= 0) & (diff < WINDOW) & (kpos >= 0)
    rq = lax.broadcasted_iota(jnp.int32, (nq, nq), 0)
    cn = lax.broadcasted_iota(jnp.int32, (nq, nq), 1)
    diff = _mod(rq, ts) - _mod(cn, ts)
    valid_n = (_div(rq, ts) == _div(cn, ts)) & (diff >= 0) & (diff < WINDOW)

    outs = []
    for hq in range(SWA_Q_HEADS):
        kv = hq // SWA_REP
        qs = q[:, hq * HEAD_DIM:(hq + 1) * HEAD_DIM]
        ks = slice(kv * HEAD_DIM, (kv + 1) * HEAD_DIM)
        s_c = jnp.where(valid_c, _dot_nt(qs, ck[:, ks]) * scale, -jnp.inf)
        s_n = jnp.where(valid_n, _dot_nt(qs, kn[:, ks]) * scale, -jnp.inf)
        sink = sinks_ref[hq]
        m = jnp.maximum(jnp.maximum(jnp.max(s_c, axis=-1, keepdims=True),
                                    jnp.max(s_n, axis=-1, keepdims=True)), sink)
        p_c = jnp.exp(s_c - m)
        p_n = jnp.exp(s_n - m)
        den = (jnp.sum(p_c, axis=-1, keepdims=True) + jnp.sum(p_n, axis=-1, keepdims=True)
               + jnp.exp(sink - m))
        outs.append(_dot((p_c / den).astype(BF16), cv[:, ks]) + _dot((p_n / den).astype(BF16), vn[:, ks]))
    osw_ref[...] = jnp.concatenate(outs, axis=-1)

    qm = qm_ref[...].astype(BF16)
    mk = mk_ref[...].reshape(bb * MEM_TOKENS, MEM_WIDTH).astype(BF16)
    mv = mv_ref[...].reshape(bb * MEM_TOKENS, MEM_WIDTH).astype(BF16)
    rq = lax.broadcasted_iota(jnp.int32, (nq, bb * MEM_TOKENS), 0)
    cm = lax.broadcasted_iota(jnp.int32, (nq, bb * MEM_TOKENS), 1)
    valid_m = _div(rq, ts) == _div(cm, MEM_TOKENS)
    outs = []
    for h in range(MEM_HEADS):
        sl = slice(h * HEAD_DIM, (h + 1) * HEAD_DIM)
        s = jnp.where(valid_m, _dot_nt(qm[:, sl], mk[:, sl]) * scale, -jnp.inf)
        outs.append(_softmax_pv(s, mv[:, sl]))
    omem_ref[...] = jnp.concatenate(outs, axis=-1)


def _attn_sample(sinks, q, kn, vn, ck, cv, qm, cmk, cmv, *, ts, start):
    B, w, _ = ck.shape
    bb = 8
    rows = lambda wd: pl.BlockSpec((bb * ts, wd), lambda i: (i, 0))
    blk3 = lambda n, wd: pl.BlockSpec((bb, n, wd), lambda i: (i, 0, 0))
    return pl.pallas_call(
        functools.partial(_attn_sample_kernel, bb=bb, ts=ts, start=start),
        grid=(B // bb,),
        in_specs=[pl.BlockSpec(memory_space=pltpu.SMEM),
                  rows(SWA_WIDTH), rows(SWA_KV_WIDTH), rows(SWA_KV_WIDTH),
                  blk3(w, SWA_KV_WIDTH), blk3(w, SWA_KV_WIDTH), rows(MEM_WIDTH),
                  blk3(MEM_TOKENS, MEM_WIDTH), blk3(MEM_TOKENS, MEM_WIDTH)],
        out_specs=[rows(SWA_WIDTH), rows(MEM_WIDTH)],
        out_shape=[jax.ShapeDtypeStruct((B * ts, SWA_WIDTH), F32),
                   jax.ShapeDtypeStruct((B * ts, MEM_WIDTH), F32)],
        compiler_params=_cparams(("parallel",)),
        name="attn_sample",
    )(sinks, q, kn, vn, ck, cv, qm, cmk, cmv)


def _out_proj_kernel(x_ref, ossm_ref, osw_ref, omem_ref, wo_ref, g2_ref, h_ref, xn_ref):
    h = x_ref[0]
    h = h + _dot(ossm_ref[...].astype(BF16), wo_ref[0:512, :])
    h = h + _dot(osw_ref[0].astype(BF16), wo_ref[512:768, :])
    h = h + _dot(omem_ref[0].astype(BF16), wo_ref[768:1024, :])
    h_ref[0] = h
    xn_ref[0] = _rms(h, g2_ref[...]).astype(BF16)


def _out_proj(x, ossm, osw, omem, wo, g2):
    B, T, _ = x.shape
    tt = min(TOKEN_TILE, T)
    full = lambda shape: pl.BlockSpec(shape, lambda b, t: (0,) * len(shape))
    blk = lambda wd: pl.BlockSpec((1, tt, wd), lambda b, t: (b, t, 0))
    return pl.pallas_call(
        _out_proj_kernel,
        grid=(B, T // tt),
        in_specs=[blk(D_MODEL), pl.BlockSpec((tt, SSM_WIDTH), lambda b, t: (t, b)),
                  blk(SWA_WIDTH), blk(MEM_WIDTH), full((D_MODEL, D_MODEL)), full((1, D_MODEL))],
        out_specs=[blk(D_MODEL), blk(D_MODEL)],
        out_shape=[jax.ShapeDtypeStruct((B, T, D_MODEL), F32),
                   jax.ShapeDtypeStruct((B, T, D_MODEL), BF16)],
        compiler_params=_cparams(("parallel", "parallel")),
        name="out_proj",
    )(x, ossm, osw, omem, wo, g2)


def _top16(s, iota):
    work = s
    rank = jnp.full(s.shape, float(PEER_TOPK), F32)
    vals = []
    for a in range(PEER_TOPK):
        m = jnp.max(work, axis=0, keepdims=True)
        idx = jnp.min(jnp.where(work == m, iota, float(PEER_KEYS)), axis=0, keepdims=True)
        sel = iota == idx
        rank = jnp.where(sel, float(a), rank)
        work = jnp.where(sel, -jnp.inf, work)
        vals.append(m)
    return vals, rank


def _sort_pairs(n):
    pairs = []
    t = max(1, (n - 1).bit_length())
    p = 1 << (t - 1)
    while p > 0:
        q, r, d = 1 << (t - 1), 0, p
        while d > 0:
            pairs += [(i, i + d) for i in range(n - d) if (i & p) == r]
            d, q, r = q - p, q >> 1, p
        p >>= 1
    return pairs


_SORT16_PAIRS = _sort_pairs(PEER_TOPK)


def _cmp_exchange(xs, i, j):
    xs[i], xs[j] = jnp.maximum(xs[i], xs[j]), jnp.minimum(xs[i], xs[j])


def _top16_sorted(s):
    n = PEER_TOPK
    xs = [s[8 * k:8 * k + 8, :] for k in range(n)]
    for i, j in _SORT16_PAIRS:
        _cmp_exchange(xs, i, j)
    for shift in (4, 2, 1):
        other = [pltpu.roll(x, shift, 0) for x in xs]
        xs = [jnp.maximum(xs[k], other[n - 1 - k]) for k in range(n)]
        for d in (8, 4, 2, 1):
            for k in range(n):
                if k & d == 0:
                    _cmp_exchange(xs, k, k + d)
    return xs


def _candidates(v1, v2, cand_scr):
    L = v1[0].shape[1]
    cand_scr[...] = jnp.full((_CAND_ROWS, L), -jnp.inf, F32)
    for a in range(PEER_TOPK):
        for b in range(_CAND_COUNT[a]):
            cand_scr[pl.ds(_CAND_ROW0[a] + b, 1), :] = v1[a] + v2[b]
    return cand_scr[...], lax.broadcasted_iota(jnp.int32, (_CAND_ROWS, L), 0).astype(F32)


def _taken_by_index(cand, iota_c):
    work = cand
    taken = jnp.zeros(cand.shape, F32)
    for _ in range(PEER_TOPK):
        m = jnp.max(work, axis=0, keepdims=True)
        idx = jnp.min(jnp.where(work == m, iota_c, float(_CAND_ROWS)), axis=0, keepdims=True)
        sel = iota_c == idx
        taken = jnp.where(sel, 1.0, taken)
        work = jnp.where(sel, -jnp.inf, work)
    return taken


_SORT8_PAIRS = _sort_pairs(8)


def _candidates_sorted(cand):
    n = PEER_TOPK
    xs = [cand[8 * k:8 * k + 8, :] for k in range(_CAND_ROWS // 8)]
    xs.append(jnp.full_like(xs[0], -jnp.inf))
    for i, j in _SORT8_PAIRS:
        _cmp_exchange(xs, i, j)
    other = [pltpu.roll(x, 4, 0) for x in xs]
    xs = xs + other[::-1]
    for shift in (4, 2, 1):
        if shift != 4:
            other = [pltpu.roll(x, shift, 0) for x in xs]
            xs = [jnp.maximum(xs[k], other[n - 1 - k]) for k in range(n)]
        for d in (8, 4, 2, 1):
            for k in range(n):
                if k & d == 0:
                    _cmp_exchange(xs, k, k + d)
    return xs


def _group_counts(taken, iota_c):
    cnt = []
    for a in range(PEER_TOPK):
        lo, hi = _CAND_ROW0[a], _CAND_ROW0[a] + _CAND_COUNT[a]
        if hi - lo == 1:
            cnt.append(taken[lo:hi])
            continue
        t0, t1 = (lo // 8) * 8, -(-hi // 8) * 8
        part = taken[t0:t1]
        if (lo, hi) != (t0, t1):
            rows = iota_c[t0:t1]
            part = jnp.where((rows >= float(lo)) & (rows < float(hi)), part, 0.0)
        cnt.append(jnp.sum(part, axis=0, keepdims=True))
    return cnt


def _route_tiles(tiles):
    n = PEER_TOPK
    states = []
    bad_any = None
    for s1, s2, cand_scr, n1_out, r2_out in tiles:
        t1 = _top16_sorted(s1)
        t2 = _top16_sorted(s2)
        v1 = [t[0:1, :] for t in t1]
        v2 = [t[0:1, :] for t in t2]
        cand, iota_c = _candidates(v1, v2, cand_scr)
        tc = _candidates_sorted(cand)
        z = jnp.ones_like(v1[0])
        for k in range(1, n):
            z = z + jnp.exp(tc[k][0:1, :] - tc[0][0:1, :])
        taken = jnp.where(cand >= tc[n - 1][0:1, :], 1.0, 0.0)
        cnt = _group_counts(taken, iota_c)

        n_taken = jnp.sum(taken, axis=0, keepdims=True)
        bad = jnp.broadcast_to(jnp.where(n_taken != float(n), 1.0, 0.0), (8, s1.shape[1]))
        for s, t in ((s1, t1), (s2, t2)):
            gap = t[0] - t[1]
            for a in range(1, n - 1):
                gap = jnp.minimum(gap, t[a] - t[a + 1])
            at_least = jnp.zeros((8, s.shape[1]), F32)
            for k in range(n):
                at_least = at_least + jnp.where(s[8 * k:8 * k + 8, :] >= t[n - 1], 1.0, 0.0)
            at_least = jnp.sum(at_least, axis=0, keepdims=True)
            bad = jnp.where((gap <= 0.0) | (at_least != float(n)), 1.0, bad)
        bad_any = bad if bad_any is None else jnp.maximum(bad_any, bad)
        states.append((s1, s2, t1, t2, v1, v2, cand, iota_c, cnt, z, n1_out, r2_out))
    has_ties = jnp.max(bad_any) > 0.0

    @pl.when(jnp.logical_not(has_ties))
    def _by_value():
        for s1, s2, t1, t2, _, _, _, _, cnt, _, n1_out, r2_out in states:
            cnt_b = [jnp.broadcast_to(c, (8, c.shape[1])) for c in cnt]
            ranks = []
            for k in range(n):
                x1 = s1[8 * k:8 * k + 8, :]
                x2 = s2[8 * k:8 * k + 8, :]
                n1 = cnt_b[0]
                r2 = jnp.zeros_like(x2)
                for a in range(n):
                    n1 = jnp.where(t1[a] > x1, cnt_b[a + 1] if a + 1 < n else 0.0, n1)
                    r2 = jnp.where(t2[a] > x2, float(a + 1), r2)
                n1_out[8 * k:8 * k + 8, :] = n1
                ranks.append(r2)
            r2_out[...] = jnp.concatenate(ranks, axis=0).astype(r2_out.dtype)

    @pl.when(has_ties)
    def _by_index():
        for s1, s2, _, _, _, _, cand, iota_c, _, _, n1_out, r2_out in states:
            iota = lax.broadcasted_iota(jnp.int32, s1.shape, 0).astype(F32)
            _, r1 = _top16(s1, iota)
            _, r2 = _top16(s2, iota)
            cnt_exact = _group_counts(_taken_by_index(cand, iota_c), iota_c)
            n1 = jnp.zeros(s1.shape, F32)
            for a in range(n):
                n1 = jnp.where(r1 == float(a), cnt_exact[a], n1)
            n1_out[...] = n1
            r2_out[...] = r2.astype(r2_out.dtype)

    return [(jnp.exp(st[0] - st[4][0]) / st[9], jnp.exp(st[1] - st[5][0])) for st in states]


def _peer_kernel(xn_ref, h_ref, wqt_ref, k1_ref, k2_ref, u_ref, vt_ref, y_ref,
                 n1_scr, c1_scr, s2_scr, r2_scr, e2_scr, acc_scr, a_scr, w_scr,
                 q_scr, xt_scr, cand_scr, *, n_steps):
    g = pl.program_id(1)
    ng = n_steps
    tt = xn_ref.shape[0]
    eb = u_ref.shape[0]
    keys_per_block = eb // PEER_KEYS
    n_lane_tiles = tt // LANES

    @pl.when(g == 0)
    def _route():
        xt_scr[...] = xn_ref[...].T
        xt = xt_scr[...]
        q_scr[...] = _dot(wqt_ref[...], xt).astype(BF16)
        for h in range(PEER_HEADS):
            for side, dst in enumerate((n1_scr, s2_scr)):
                k_ref = (k1_ref, k2_ref)[side]
                r0 = h * 2 * PEER_HALF + side * PEER_HALF
                s = _dot(k_ref[...], q_scr[r0:r0 + PEER_HALF, :])
                for c in range(n_lane_tiles):
                    dst[h, c] = s[:, c * LANES:(c + 1) * LANES]

        per_trip = PEER_ROUTE_TILES
        trips_per_head = n_lane_tiles // per_trip

        def body(i, carry):
            h = i // trips_per_head
            cs = [(i % trips_per_head) * per_trip + k for k in range(per_trip)]
            outs = _route_tiles([(n1_scr[h, c], s2_scr[h, c], cand_scr.at[k], n1_scr.at[h, c], r2_scr.at[h, c])
                                 for k, c in enumerate(cs)])
            for c, (c1, e2) in zip(cs, outs):
                c1_scr[h, c] = c1
                e2_scr[h, c] = e2.astype(BF16)
            return carry

        lax.fori_loop(0, PEER_HEADS * trips_per_head, body, 0)
        acc_scr[...] = jnp.zeros_like(acc_scr)

    def gate(a_scr, w_scr, block):
        zero = jnp.zeros((), BF16)
        group = PEER_GATE_KEYS
        for k0 in range(0, keys_per_block, group):
            def lane_tile(c, carry, k0=k0):
                lanes = pl.ds(pl.multiple_of(c * LANES, LANES), LANES)
                gts = [None] * group
                for h in range(PEER_HEADS):
                    r2 = r2_scr[h, c].reshape(PEER_KEYS // 16, 16, LANES)
                    e2 = e2_scr[h, c].reshape(PEER_KEYS // 16, 16, LANES)
                    for ii in range(group):
                        row = pl.ds(block * keys_per_block + k0 + ii, 1)
                        n_b = jnp.broadcast_to(n1_scr[h, c, row, :], (16, LANES)).astype(BF16)[None]
                        c_b = jnp.broadcast_to(c1_scr[h, c, row, :], (16, LANES)).astype(BF16)[None]
                        term = jnp.where(r2 < n_b, e2, zero) * c_b
                        gts[ii] = term if h == 0 else gts[ii] + term
                for ii in range(group):
                    r0 = (k0 + ii) * PEER_KEYS
                    rows = slice(r0, r0 + PEER_KEYS)
                    act = _gelu(a_scr[rows, lanes].astype(BF16))
                    w_scr[rows, lanes] = gts[ii].reshape(PEER_KEYS, LANES) * act
                return carry

            lax.fori_loop(0, n_lane_tiles, lane_tile, 0)

    @pl.when(g < ng)
    def _scores():
        a_scr[...] = _dot(u_ref[...], xt_scr[...])

    @pl.when(g > 0)
    def _v_product():
        acc_scr[...] += _dot(vt_ref[...], w_scr[...])

    @pl.when(g < ng)
    def _weights():
        gate(a_scr, w_scr, g)

    @pl.when(g == ng)
    def _fin():
        y_ref[...] = h_ref[...] + acc_scr[...].T


def _peer(xn, h, wqt, k1, k2, u_tab, vt_tab):
    n = xn.shape[0]
    tt = min(TOKEN_TILE, n)
    eb = PEER_EXPERT_BLOCK
    ng = PEER_EXPERTS // eb
    full = lambda shape: pl.BlockSpec(shape, lambda i, g: (0,) * len(shape), pipeline_mode=pl.Buffered(1))
    tok = pl.BlockSpec((tt, D_MODEL), lambda i, g: (i, 0))
    head_f32 = pltpu.VMEM((PEER_HEADS, tt // LANES, PEER_KEYS, LANES), F32)
    head_bf16 = pltpu.VMEM((PEER_HEADS, tt // LANES, PEER_KEYS, LANES), BF16)
    return pl.pallas_call(
        functools.partial(_peer_kernel, n_steps=ng),
        grid=(n // tt, ng + 1),
        in_specs=[tok, tok, full((2 * PEER_HEADS * PEER_HALF, D_MODEL)),
                  full((PEER_KEYS, PEER_HALF)), full((PEER_KEYS, PEER_HALF)),
                  pl.BlockSpec((eb, D_MODEL), lambda i, g: (jnp.minimum(g, ng - 1), 0)),
                  pl.BlockSpec((D_MODEL, eb), lambda i, g: (0, jnp.maximum(g - 1, 0)))],
        out_specs=tok,
        out_shape=jax.ShapeDtypeStruct((n, D_MODEL), F32),
        scratch_shapes=[head_f32, head_f32, head_f32, head_bf16, head_bf16,
                        pltpu.VMEM((D_MODEL, tt), F32),
                        pltpu.VMEM((eb, tt), F32), pltpu.VMEM((eb, tt), BF16),
                        pltpu.VMEM((2 * PEER_HEADS * PEER_HALF, tt), BF16),
                        pltpu.VMEM((D_MODEL, tt), BF16),
                        pltpu.VMEM((PEER_ROUTE_TILES, _CAND_ROWS, LANES), F32)],
        compiler_params=_cparams(("parallel", "arbitrary")),
        name="peer",
    )(xn, h, wqt, k1, k2, u_tab, vt_tab)


def _rope_tables(pos):
    half = HEAD_DIM // 2
    inv = ROPE_THETA ** (-jnp.arange(half, dtype=F32) / half)
    ang = pos.astype(F32)[:, None] * inv[None, :]
    cos = jnp.cos(ang)
    sin = jnp.sin(ang)
    cos = jnp.tile(jnp.concatenate([cos, cos], axis=-1), (1, SWA_Q_HEADS))
    sin = jnp.tile(jnp.concatenate([-sin, sin], axis=-1), (1, SWA_Q_HEADS))
    return cos, sin


def _ssm_params(log_dt, a_re, a_im, b_re, b_im, c_re, c_im):
    dt = jnp.exp(log_dt)
    mag = jnp.exp(a_re * dt)
    lam_re = mag * jnp.cos(a_im * dt)
    lam_im = mag * jnp.sin(a_im * dt)
    den = a_re * a_re + a_im * a_im
    z_re = ((lam_re - 1.0) * a_re + lam_im * a_im) / den
    z_im = (lam_im * a_re - (lam_re - 1.0) * a_im) / den
    bb_re = z_re[..., None] * b_re - z_im[..., None] * b_im
    bb_im = z_re[..., None] * b_im + z_im[..., None] * b_re
    hg = SSM_HALF_GROUPS
    eye = jnp.eye(hg, dtype=F32)
    bb = jnp.stack([bb_re, bb_im]).reshape(2, 2, hg, SSM_STATE, SSM_GROUP)
    bmat = jnp.einsum('rjgnc,gh->jgcrhn', bb, eye).reshape(2, hg * SSM_GROUP, 2 * SSM_HALF_STATE)
    cc = jnp.stack([c_re, -c_im]).reshape(2, 2, hg, SSM_GROUP, SSM_STATE)
    cmat = jnp.einsum('rjgcn,gh->jrgnhc', cc, eye).reshape(2, 2 * SSM_HALF_STATE, hg * SSM_GROUP)
    lam = jnp.stack([lam_re.reshape(2, SSM_HALF_STATE), lam_im.reshape(2, SSM_HALF_STATE)], axis=1)
    return lam.reshape(1, SSM_COLS), bmat.astype(BF16), cmat.astype(BF16)


def _state_to_cols(s_re, s_im):
    b = s_re.shape[0]
    st = jnp.stack([s_re.reshape(b, 2, SSM_HALF_STATE), s_im.reshape(b, 2, SSM_HALF_STATE)], axis=2)
    return st.reshape(b, SSM_COLS)


def _cols_to_state(cols):
    b = cols.shape[0]
    st = cols.reshape(b, 2, 2, SSM_HALF_STATE)
    return (st[:, :, 0].reshape(b, SSM_GROUPS, SSM_STATE), st[:, :, 1].reshape(b, SSM_GROUPS, SSM_STATE))


def kernel(x_prompt, x_sample, state_ssm_re, state_ssm_im, cache_win_k, cache_win_v, cache_mem_k, cache_mem_v, mem_prompt, norm1_g, w_in, ssm_log_dt, ssm_a_re, ssm_a_im, ssm_b_re, ssm_b_im, ssm_c_re, ssm_c_im, ssm_d, w_glu, b_glu, swa_q_norm, swa_k_norm, swa_sinks, mem_norm_g, w_mem_kv, mem_q_norm, mem_k_norm, w_out, norm2_g, peer_wq, peer_k1, peer_k2, peer_u, peer_v):
    depth = norm1_g.shape[0]
    assert depth == 1
    l = 0
    B, T, _ = x_prompt.shape
    SB, ST, _ = x_sample.shape
    w = cache_win_k.shape[2]

    row = lambda a: a.reshape(1, -1)
    g1 = row(norm1_g[l])
    g2 = row(norm2_g[l])
    win = w_in[l].astype(BF16)
    wo = w_out[l].astype(BF16)
    wglu = w_glu[l].astype(BF16)
    bglu = row(b_glu[l])
    gq = row(jnp.tile(swa_q_norm[l], SWA_Q_HEADS))
    gk = row(jnp.tile(swa_k_norm[l], SWA_KV_HEADS))
    gm = row(jnp.tile(mem_q_norm[l], MEM_HEADS))
    gmk = row(jnp.tile(mem_k_norm[l], MEM_HEADS))
    gmem = row(mem_norm_g[l])
    wkv = w_mem_kv[l].astype(BF16)
    sinks = swa_sinks[l]
    head_id = np.arange(SWA_WIDTH) // HEAD_DIM
    ones = jnp.asarray(head_id[:, None] == head_id[None, :], dtype=BF16)
    lam, bmat, cmat = _ssm_params(ssm_log_dt[l], ssm_a_re[l], ssm_a_im[l], ssm_b_re[l], ssm_b_im[l],
                                  ssm_c_re[l], ssm_c_im[l])
    dskip = row(ssm_d[l])
    wqt = peer_wq[l].astype(BF16).T
    k1 = peer_k1[l].astype(BF16)
    k2 = peer_k2[l].astype(BF16)
    u_tab = peer_u[l].astype(BF16)
    vt_tab = peer_v[l].astype(BF16).T

    cos_p, sin_p = _rope_tables(jnp.arange(T, dtype=jnp.int32))
    u_p, q_p, k_p, v_p, qm_p = _in_proj(x_prompt, cos_p, sin_p, g1, win, ones, gq, gk, gm)
    zeros = jnp.zeros((B, SSM_COLS), F32)
    ossm_p, sfin_p = _s5(u_p.reshape(T * B, SSM_WIDTH), zeros, lam, bmat, cmat, dskip, wglu, bglu,
                         bt=B, tt=S5_TIME_TILE)
    mk, mv = _mem_kv(mem_prompt.reshape(B * MEM_TOKENS, D_MODEL), gmem, wkv, ones, gmk)
    mk = mk.reshape(B, MEM_TOKENS, MEM_WIDTH)
    mv = mv.reshape(B, MEM_TOKENS, MEM_WIDTH)
    osw_p, omem_p = _attn_prompt(sinks, q_p, k_p, v_p, qm_p, mk, mv)
    h_p, xn_p = _out_proj(x_prompt, ossm_p.reshape(T, B * SSM_WIDTH), osw_p, omem_p, wo, g2)
    y_p = _peer(xn_p.reshape(B * T, D_MODEL), h_p.reshape(B * T, D_MODEL), wqt, k1, k2, u_tab, vt_tab)
    y_p = y_p.reshape(B, T, D_MODEL)
    p_sr, p_si = _cols_to_state(sfin_p)
    p_wk = k_p[:, T - w:].reshape(B, w, SWA_KV_HEADS, HEAD_DIM)
    p_wv = v_p[:, T - w:].reshape(B, w, SWA_KV_HEADS, HEAD_DIM)
    p_mk = mk.reshape(B, MEM_TOKENS, MEM_HEADS, HEAD_DIM)
    p_mv = mv.reshape(B, MEM_TOKENS, MEM_HEADS, HEAD_DIM)

    n_s = SB * ST
    pos_s = PAST_LEN + jnp.tile(jnp.arange(ST, dtype=jnp.int32), SB)
    cos_s, sin_s = _rope_tables(pos_s)
    xs = x_sample.reshape(1, n_s, D_MODEL)
    u_s, q_s, k_s, v_s, qm_s = _in_proj(xs, cos_s, sin_s, g1, win, ones, gq, gk, gm)
    u_tm = u_s.reshape(SB, ST, SSM_WIDTH).transpose(1, 0, 2).reshape(n_s, SSM_WIDTH)
    ossm_tm, sfin_s = _s5(u_tm, _state_to_cols(state_ssm_re[l], state_ssm_im[l]), lam, bmat, cmat,
                          dskip, wglu, bglu, bt=SB, tt=ST)
    ossm_s = ossm_tm.reshape(ST, SB, SSM_WIDTH).transpose(1, 0, 2).reshape(n_s, SSM_WIDTH)
    ck = cache_win_k[l].reshape(SB, w, SWA_KV_WIDTH)
    cv = cache_win_v[l].reshape(SB, w, SWA_KV_WIDTH)
    q_s2 = q_s.reshape(n_s, SWA_WIDTH)
    k_s2 = k_s.reshape(n_s, SWA_KV_WIDTH)
    v_s2 = v_s.reshape(n_s, SWA_KV_WIDTH)
    osw_s, omem_s = _attn_sample(sinks, q_s2, k_s2, v_s2, ck, cv, qm_s.reshape(n_s, MEM_WIDTH),
                                 cache_mem_k[l].reshape(SB, MEM_TOKENS, MEM_WIDTH),
                                 cache_mem_v[l].reshape(SB, MEM_TOKENS, MEM_WIDTH),
                                 ts=ST, start=PAST_LEN)
    h_s, xn_s = _out_proj(xs, ossm_s, osw_s.reshape(1, n_s, SWA_WIDTH), omem_s.reshape(1, n_s, MEM_WIDTH),
                          wo, g2)
    y_s = _peer(xn_s.reshape(n_s, D_MODEL), h_s.reshape(n_s, D_MODEL), wqt, k1, k2, u_tab, vt_tab)
    y_s = y_s.reshape(SB, ST, D_MODEL)
    s_sr, s_si = _cols_to_state(sfin_s)
    s_wk = jnp.concatenate([ck, k_s2.reshape(SB, ST, SWA_KV_WIDTH)], axis=1)[:, -w:]
    s_wv = jnp.concatenate([cv, v_s2.reshape(SB, ST, SWA_KV_WIDTH)], axis=1)[:, -w:]
    s_wk = s_wk.reshape(SB, w, SWA_KV_HEADS, HEAD_DIM)
    s_wv = s_wv.reshape(SB, w, SWA_KV_HEADS, HEAD_DIM)

    st = lambda a: a[None]
    return (y_p, y_s, st(p_sr), st(p_si), st(p_wk), st(p_wv), st(p_mk), st(p_mv),
            st(s_sr), st(s_si), st(s_wk), st(s_wv))
```

```python
import functools
import math

import jax
import jax.numpy as jnp
import numpy as np
from jax import lax
from jax.experimental import pallas as pl
from jax.experimental.pallas import tpu as pltpu

F32 = jnp.float32
BF16 = jnp.bfloat16

D_MODEL = 1024
HEAD_DIM = 64
EPS = 1e-6
ROPE_THETA = 10000.0
PAST_LEN = 8192
SSM_WIDTH = 512
SSM_GROUP = 16
SSM_GROUPS = 32
SSM_STATE = 64
SSM_HALF_GROUPS = SSM_GROUPS // 2
SSM_HALF_STATE = SSM_HALF_GROUPS * SSM_STATE
SSM_COLS = 2 * 2 * SSM_HALF_STATE
SWA_Q_HEADS = 4
SWA_KV_HEADS = 2
SWA_REP = 2
SWA_WIDTH = 256
SWA_KV_WIDTH = 128
WINDOW = 128
MEM_TOKENS = 256
MEM_HEADS = 4
MEM_WIDTH = 256
IN_WIDTH = 1280
PEER_HEADS = 8
PEER_KEYS = 128
PEER_EXPERTS = PEER_KEYS * PEER_KEYS
PEER_TOPK = 16
PEER_HALF = 128

LANES = 128
VMEM_LIMIT = 60 * 1024 * 1024

TOKEN_TILE = 512
PEER_EXPERT_BLOCK = 2048
PEER_GATE_KEYS = 4
PEER_ROUTE_TILES = 2
S5_TIME_TILE = 64
ATTN_BLOCKS = 2

_CAND_COUNT = [PEER_TOPK // (a + 1) for a in range(PEER_TOPK)]
_CAND_ROW0 = [0, 16, 24, 32, 36, 40, 42, 44, 48, 49, 50, 51, 52, 53, 54, 55]
_CAND_ROWS = 56


def _cparams(sem):
    return pltpu.CompilerParams(dimension_semantics=sem, vmem_limit_bytes=VMEM_LIMIT)


def _rms(x, g):
    return x * lax.rsqrt(jnp.mean(x * x, axis=-1, keepdims=True) + EPS) * g


def _gelu(x):
    return 0.5 * x * (1.0 + lax.erf(x * math.sqrt(0.5)))


def _dot(a, b):
    return jnp.dot(a, b, preferred_element_type=F32)


def _dot_nt(a, b):
    return lax.dot_general(a, b, (((1,), (1,)), ((), ())), preferred_element_type=F32)


def _div(x, n):
    return x >> (n.bit_length() - 1) if n & (n - 1) == 0 else x // n


def _mod(x, n):
    return x & (n - 1) if n & (n - 1) == 0 else x % n


def _head_rms(x, ones_bd, g):
    sq = x * x
    hi = sq.astype(BF16)
    lo = (sq - hi.astype(F32)).astype(BF16)
    ms = (_dot(hi, ones_bd) + _dot(lo, ones_bd)) * np.float32(1.0 / HEAD_DIM)
    return x * lax.rsqrt(ms + EPS) * g


def _rope(x, cos, sin_signed):
    w = x.shape[-1]
    lane = lax.broadcasted_iota(jnp.int32, x.shape, 1)
    first_half = _mod(lane, HEAD_DIM) < (HEAD_DIM // 2)
    partner = jnp.where(first_half, pltpu.roll(x, w - HEAD_DIM // 2, 1),
                        pltpu.roll(x, HEAD_DIM // 2, 1))
    return x * cos + partner * sin_signed


def _in_proj_kernel(x_ref, g1_ref, win_ref, ones_ref, gq_ref, gk_ref, gm_ref, cos_ref, sin_ref,
                    u_ref, q_ref, k_ref, v_ref, qm_ref):
    x = x_ref[0]
    xn = _rms(x, g1_ref[...])
    proj = _dot(xn.astype(BF16), win_ref[...])
    u_ref[...] = proj[:, :SSM_WIDTH]
    q = proj[:, 512:768]
    k = proj[:, 768:896]
    v_ref[0] = proj[:, 896:1024]
    qm = proj[:, 1024:1280]
    ones = ones_ref[...]
    cos = cos_ref[...]
    sin = sin_ref[...]
    q_ref[0] = _rope(_head_rms(q, ones, gq_ref[...]), cos, sin)
    k_ref[0] = _rope(_head_rms(k, ones[:SWA_KV_WIDTH, :SWA_KV_WIDTH], gk_ref[...]),
                     cos[:, :SWA_KV_WIDTH], sin[:, :SWA_KV_WIDTH])
    qm_ref[0] = _head_rms(qm, ones, gm_ref[...])


def _in_proj(x, cos, sin, g1, win, ones, gq, gk, gm):
    B, T, _ = x.shape
    tt = min(TOKEN_TILE, T)
    grid = (B, T // tt)
    full = lambda shape: pl.BlockSpec(shape, lambda b, t: (0,) * len(shape))
    return pl.pallas_call(
        _in_proj_kernel,
        grid=grid,
        in_specs=[
            pl.BlockSpec((1, tt, D_MODEL), lambda b, t: (b, t, 0)),
            full((1, D_MODEL)), full((D_MODEL, IN_WIDTH)), full((SWA_WIDTH, SWA_WIDTH)),
            full((1, SWA_WIDTH)), full((1, SWA_KV_WIDTH)), full((1, MEM_WIDTH)),
            pl.BlockSpec((tt, SWA_WIDTH), lambda b, t: (t, 0)),
            pl.BlockSpec((tt, SWA_WIDTH), lambda b, t: (t, 0)),
        ],
        out_specs=[
            pl.BlockSpec((tt, SSM_WIDTH), lambda b, t: (t, b)),
            pl.BlockSpec((1, tt, SWA_WIDTH), lambda b, t: (b, t, 0)),
            pl.BlockSpec((1, tt, SWA_KV_WIDTH), lambda b, t: (b, t, 0)),
            pl.BlockSpec((1, tt, SWA_KV_WIDTH), lambda b, t: (b, t, 0)),
            pl.BlockSpec((1, tt, MEM_WIDTH), lambda b, t: (b, t, 0)),
        ],
        out_shape=[
            jax.ShapeDtypeStruct((T, B * SSM_WIDTH), F32),
            jax.ShapeDtypeStruct((B, T, SWA_WIDTH), F32),
            jax.ShapeDtypeStruct((B, T, SWA_KV_WIDTH), F32),
            jax.ShapeDtypeStruct((B, T, SWA_KV_WIDTH), F32),
            jax.ShapeDtypeStruct((B, T, MEM_WIDTH), F32),
        ],
        compiler_params=_cparams(("parallel", "parallel")),
        name="in_proj",
    )(x, g1, win, ones, gq, gk, gm, cos, sin)


def _s5_kernel(u_ref, s0_ref, lam_ref, bmat_ref, cmat_ref, d_ref, wglu_ref, bglu_ref,
               o_ref, sfin_ref, s_scr, carry_scr, *, bt, tt):
    @pl.when(pl.program_id(0) == 0)
    def _():
        carry_scr[...] = s0_ref[...]

    u = u_ref[...]
    ub = u.astype(BF16)
    hw = 2 * SSM_HALF_STATE
    for j in range(2):
        s_scr[:, j * hw:(j + 1) * hw] = _dot(ub[:, j * 256:(j + 1) * 256], bmat_ref[j])

    def step(t, carry):
        r0 = pl.multiple_of(t * bt, bt)
        for j in range(2):
            c_re = pl.ds(j * hw, SSM_HALF_STATE)
            c_im = pl.ds(j * hw + SSM_HALF_STATE, SSM_HALF_STATE)
            p_re = carry_scr[:, c_re]
            p_im = carry_scr[:, c_im]
            l_re = lam_ref[:, c_re]
            l_im = lam_ref[:, c_im]
            n_re = l_re * p_re - l_im * p_im + s_scr[pl.ds(r0, bt), c_re]
            n_im = l_re * p_im + l_im * p_re + s_scr[pl.ds(r0, bt), c_im]
            s_scr[pl.ds(r0, bt), c_re] = n_re
            s_scr[pl.ds(r0, bt), c_im] = n_im
            carry_scr[:, c_re] = n_re
            carry_scr[:, c_im] = n_im
        return carry

    lax.fori_loop(0, tt, step, 0)
    sfin_ref[...] = carry_scr[...]

    ys = [_dot(s_scr[:, j * hw:(j + 1) * hw].astype(BF16), cmat_ref[j]) for j in range(2)]
    y = jnp.concatenate(ys, axis=-1) + d_ref[...] * u
    y = _gelu(y)
    z = _dot(y.astype(BF16), wglu_ref[...]) + bglu_ref[...]
    o_ref[...] = y * jax.nn.sigmoid(z)


def _s5(u_tm, s0, lam, bmat, cmat, d, wglu, bglu, *, bt, tt):
    rows = u_tm.shape[0]
    nt = rows // (bt * tt)
    full = lambda shape: pl.BlockSpec(shape, lambda t: (0,) * len(shape))
    return pl.pallas_call(
        functools.partial(_s5_kernel, bt=bt, tt=tt),
        grid=(nt,),
        in_specs=[
            pl.BlockSpec((bt * tt, SSM_WIDTH), lambda t: (t, 0)),
            full((bt, SSM_COLS)), full((1, SSM_COLS)),
            full((2, 256, 2 * SSM_HALF_STATE)), full((2, 2 * SSM_HALF_STATE, 256)),
            full((1, SSM_WIDTH)), full((SSM_WIDTH, SSM_WIDTH)), full((1, SSM_WIDTH)),
        ],
        out_specs=[
            pl.BlockSpec((bt * tt, SSM_WIDTH), lambda t: (t, 0)),
            full((bt, SSM_COLS)),
        ],
        out_shape=[
            jax.ShapeDtypeStruct((rows, SSM_WIDTH), F32),
            jax.ShapeDtypeStruct((bt, SSM_COLS), F32),
        ],
        scratch_shapes=[pltpu.VMEM((bt * tt, SSM_COLS), F32), pltpu.VMEM((bt, SSM_COLS), F32)],
        compiler_params=_cparams(("arbitrary",)),
        name="s5_mixer",
    )(u_tm, s0, lam, bmat, cmat, d, wglu, bglu)


def _mem_kv_kernel(m_ref, g_ref, w_ref, ones_ref, gk_ref, k_ref, v_ref):
    xn = _rms(m_ref[...], g_ref[...])
    kv = _dot(xn.astype(BF16), w_ref[...])
    k_ref[...] = _head_rms(kv[:, :MEM_WIDTH], ones_ref[...], gk_ref[...])
    v_ref[...] = kv[:, MEM_WIDTH:]


def _mem_kv(mem_rows, g, w, ones, gk):
    rows = mem_rows.shape[0]
    tt = min(TOKEN_TILE, rows)
    full = lambda shape: pl.BlockSpec(shape, lambda t: (0,) * len(shape))
    return pl.pallas_call(
        _mem_kv_kernel,
        grid=(rows // tt,),
        in_specs=[pl.BlockSpec((tt, D_MODEL), lambda t: (t, 0)), full((1, D_MODEL)),
                  full((D_MODEL, 2 * MEM_WIDTH)), full((MEM_WIDTH, MEM_WIDTH)), full((1, MEM_WIDTH))],
        out_specs=[pl.BlockSpec((tt, MEM_WIDTH), lambda t: (t, 0))] * 2,
        out_shape=[jax.ShapeDtypeStruct((rows, MEM_WIDTH), F32)] * 2,
        compiler_params=_cparams(("parallel",)),
        name="mem_kv",
    )(mem_rows, g, w, ones, gk)


def _softmax_pv(s, v_b, sink=None):
    m = jnp.max(s, axis=-1, keepdims=True)
    if sink is not None:
        m = jnp.maximum(m, sink)
    p = jnp.exp(s - m)
    den = jnp.sum(p, axis=-1, keepdims=True)
    if sink is not None:
        den = den + jnp.exp(sink - m)
    return _dot((p / den).astype(BF16), v_b)


def _attn_prompt_kernel(sinks_ref, q_ref, kp_ref, kc_ref, vp_ref, vc_ref, qm_ref, mk_ref, mv_ref,
                        osw_ref, omem_ref):
    n = pl.program_id(1)
    scale = np.float32(HEAD_DIM ** -0.5)
    q = q_ref[0].astype(BF16)
    kk = jnp.concatenate([kp_ref[0], kc_ref[0]], axis=0).astype(BF16)
    vv = jnp.concatenate([vp_ref[0], vc_ref[0]], axis=0).astype(BF16)
    qi = lax.broadcasted_iota(jnp.int32, (WINDOW, 2 * WINDOW), 0)
    ki = lax.broadcasted_iota(jnp.int32, (WINDOW, 2 * WINDOW), 1) - WINDOW
    diff = qi - ki
    for j in range(ATTN_BLOCKS):
        valid = (diff >= 0) & (diff < WINDOW) & ((n * ATTN_BLOCKS + j) * WINDOW + ki >= 0)
        rows = slice(j * WINDOW, (j + 1) * WINDOW)
        keys = slice(j * WINDOW, (j + 2) * WINDOW)
        outs = []
        for hq in range(SWA_Q_HEADS):
            kv = hq // SWA_REP
            s = _dot_nt(q[rows, hq * HEAD_DIM:(hq + 1) * HEAD_DIM],
                        kk[keys, kv * HEAD_DIM:(kv + 1) * HEAD_DIM]) * scale
            s = jnp.where(valid, s, -jnp.inf)
            outs.append(_softmax_pv(s, vv[keys, kv * HEAD_DIM:(kv + 1) * HEAD_DIM], sinks_ref[hq]))
        osw_ref[0, rows, :] = jnp.concatenate(outs, axis=-1)

    qm = qm_ref[0].astype(BF16)
    mk = mk_ref[0].astype(BF16)
    mv = mv_ref[0].astype(BF16)
    outs = []
    for h in range(MEM_HEADS):
        sl = slice(h * HEAD_DIM, (h + 1) * HEAD_DIM)
        s = _dot_nt(qm[:, sl], mk[:, sl]) * scale
        outs.append(_softmax_pv(s, mv[:, sl]))
    omem_ref[0] = jnp.concatenate(outs, axis=-1)


def _attn_prompt(sinks, q, k, v, qm, mk, mv):
    B, T, _ = q.shape
    rows = ATTN_BLOCKS * WINDOW
    blk = lambda w: pl.BlockSpec((1, rows, w), lambda b, n: (b, n, 0))
    prev = lambda w: pl.BlockSpec((1, WINDOW, w), lambda b, n: (b, jnp.maximum(n * ATTN_BLOCKS - 1, 0), 0))
    memb = pl.BlockSpec((1, MEM_TOKENS, MEM_WIDTH), lambda b, n: (b, 0, 0))
    return pl.pallas_call(
        _attn_prompt_kernel,
        grid=(B, T // rows),
        in_specs=[pl.BlockSpec(memory_space=pltpu.SMEM),
                  blk(SWA_WIDTH), prev(SWA_KV_WIDTH), blk(SWA_KV_WIDTH), prev(SWA_KV_WIDTH),
                  blk(SWA_KV_WIDTH), blk(MEM_WIDTH), memb, memb],
        out_specs=[blk(SWA_WIDTH), blk(MEM_WIDTH)],
        out_shape=[jax.ShapeDtypeStruct((B, T, SWA_WIDTH), F32),
                   jax.ShapeDtypeStruct((B, T, MEM_WIDTH), F32)],
        compiler_params=_cparams(("parallel", "parallel")),
        name="attn_prompt",
    )(sinks, q, k, k, v, v, qm, mk, mv)


def _attn_sample_kernel(sinks_ref, q_ref, kn_ref, vn_ref, ck_ref, cv_ref, qm_ref, mk_ref, mv_ref,
                        osw_ref, omem_ref, *, bb, ts, start):
    scale = np.float32(HEAD_DIM ** -0.5)
    w = ck_ref.shape[1]
    nq = bb * ts
    q = q_ref[...].astype(BF16)
    kn = kn_ref[...].astype(BF16)
    vn = vn_ref[...].astype(BF16)
    ck = ck_ref[...].reshape(bb * w, SWA_KV_WIDTH).astype(BF16)
    cv = cv_ref[...].reshape(bb * w, SWA_KV_WIDTH).astype(BF16)

    rq = lax.broadcasted_iota(jnp.int32, (nq, bb * w), 0)
    cc = lax.broadcasted_iota(jnp.int32, (nq, bb * w), 1)
    qpos = start + _mod(rq, ts)
    kpos = start - w + _mod(cc, w)
    diff = qpos - kpos
    valid_c = (_div(rq, ts) == _div(cc, w)) & (diff >= 0) & (diff < WINDOW) & (kpos >= 0)
    rq = lax.broadcasted_iota(jnp.int32, (nq, nq), 0)
    cn = lax.broadcasted_iota(jnp.int32, (nq, nq), 1)
    diff = _mod(rq, ts) - _mod(cn, ts)
    valid_n = (_div(rq, ts) == _div(cn, ts)) & (diff >= 0) & (diff < WINDOW)

    outs = []
    for hq in range(SWA_Q_HEADS):
        kv = hq // SWA_REP
        qs = q[:, hq * HEAD_DIM:(hq + 1) * HEAD_DIM]
        ks = slice(kv * HEAD_DIM, (kv + 1) * HEAD_DIM)
        s_c = jnp.where(valid_c, _dot_nt(qs, ck[:, ks]) * scale, -jnp.inf)
        s_n = jnp.where(valid_n, _dot_nt(qs, kn[:, ks]) * scale, -jnp.inf)
        sink = sinks_ref[hq]
        m = jnp.maximum(jnp.maximum(jnp.max(s_c, axis=-1, keepdims=True),
                                    jnp.max(s_n, axis=-1, keepdims=True)), sink)
        p_c = jnp.exp(s_c - m)
        p_n = jnp.exp(s_n - m)
        den = (jnp.sum(p_c, axis=-1, keepdims=True) + jnp.sum(p_n, axis=-1, keepdims=True)
               + jnp.exp(sink - m))
        outs.append(_dot((p_c / den).astype(BF16), cv[:, ks]) + _dot((p_n / den).astype(BF16), vn[:, ks]))
    osw_ref[...] = jnp.concatenate(outs, axis=-1)

    qm = qm_ref[...].astype(BF16)
    mk = mk_ref[...].reshape(bb * MEM_TOKENS, MEM_WIDTH).astype(BF16)
    mv = mv_ref[...].reshape(bb * MEM_TOKENS, MEM_WIDTH).astype(BF16)
    rq = lax.broadcasted_iota(jnp.int32, (nq, bb * MEM_TOKENS), 0)
    cm = lax.broadcasted_iota(jnp.int32, (nq, bb * MEM_TOKENS), 1)
    valid_m = _div(rq, ts) == _div(cm, MEM_TOKENS)
    outs = []
    for h in range(MEM_HEADS):
        sl = slice(h * HEAD_DIM, (h + 1) * HEAD_DIM)
        s = jnp.where(valid_m, _dot_nt(qm[:, sl], mk[:, sl]) * scale, -jnp.inf)
        outs.append(_softmax_pv(s, mv[:, sl]))
    omem_ref[...] = jnp.concatenate(outs, axis=-1)


def _attn_sample(sinks, q, kn, vn, ck, cv, qm, cmk, cmv, *, ts, start):
    B, w, _ = ck.shape
    bb = 8
    rows = lambda wd: pl.BlockSpec((bb * ts, wd), lambda i: (i, 0))
    blk3 = lambda n, wd: pl.BlockSpec((bb, n, wd), lambda i: (i, 0, 0))
    return pl.pallas_call(
        functools.partial(_attn_sample_kernel, bb=bb, ts=ts, start=start),
        grid=(B // bb,),
        in_specs=[pl.BlockSpec(memory_space=pltpu.SMEM),
                  rows(SWA_WIDTH), rows(SWA_KV_WIDTH), rows(SWA_KV_WIDTH),
                  blk3(w, SWA_KV_WIDTH), blk3(w, SWA_KV_WIDTH), rows(MEM_WIDTH),
                  blk3(MEM_TOKENS, MEM_WIDTH), blk3(MEM_TOKENS, MEM_WIDTH)],
        out_specs=[rows(SWA_WIDTH), rows(MEM_WIDTH)],
        out_shape=[jax.ShapeDtypeStruct((B * ts, SWA_WIDTH), F32),
                   jax.ShapeDtypeStruct((B * ts, MEM_WIDTH), F32)],
        compiler_params=_cparams(("parallel",)),
        name="attn_sample",
    )(sinks, q, kn, vn, ck, cv, qm, cmk, cmv)


def _out_proj_kernel(x_ref, ossm_ref, osw_ref, omem_ref, wo_ref, g2_ref, h_ref, xn_ref):
    h = x_ref[0]
    h = h + _dot(ossm_ref[...].astype(BF16), wo_ref[0:512, :])
    h = h + _dot(osw_ref[0].astype(BF16), wo_ref[512:768, :])
    h = h + _dot(omem_ref[0].astype(BF16), wo_ref[768:1024, :])
    h_ref[0] = h
    xn_ref[0] = _rms(h, g2_ref[...]).astype(BF16)


def _out_proj(x, ossm, osw, omem, wo, g2):
    B, T, _ = x.shape
    tt = min(TOKEN_TILE, T)
    full = lambda shape: pl.BlockSpec(shape, lambda b, t: (0,) * len(shape))
    blk = lambda wd: pl.BlockSpec((1, tt, wd), lambda b, t: (b, t, 0))
    return pl.pallas_call(
        _out_proj_kernel,
        grid=(B, T // tt),
        in_specs=[blk(D_MODEL), pl.BlockSpec((tt, SSM_WIDTH), lambda b, t: (t, b)),
                  blk(SWA_WIDTH), blk(MEM_WIDTH), full((D_MODEL, D_MODEL)), full((1, D_MODEL))],
        out_specs=[blk(D_MODEL), blk(D_MODEL)],
        out_shape=[jax.ShapeDtypeStruct((B, T, D_MODEL), F32),
                   jax.ShapeDtypeStruct((B, T, D_MODEL), BF16)],
        compiler_params=_cparams(("parallel", "parallel")),
        name="out_proj",
    )(x, ossm, osw, omem, wo, g2)


def _top16(s, iota):
    work = s
    rank = jnp.full(s.shape, float(PEER_TOPK), F32)
    vals = []
    for a in range(PEER_TOPK):
        m = jnp.max(work, axis=0, keepdims=True)
        idx = jnp.min(jnp.where(work == m, iota, float(PEER_KEYS)), axis=0, keepdims=True)
        sel = iota == idx
        rank = jnp.where(sel, float(a), rank)
        work = jnp.where(sel, -jnp.inf, work)
        vals.append(m)
    return vals, rank


def _sort_pairs(n):
    pairs = []
    t = max(1, (n - 1).bit_length())
    p = 1 << (t - 1)
    while p > 0:
        q, r, d = 1 << (t - 1), 0, p
        while d > 0:
            pairs += [(i, i + d) for i in range(n - d) if (i & p) == r]
            d, q, r = q - p, q >> 1, p
        p >>= 1
    return pairs


_SORT16_PAIRS = _sort_pairs(PEER_TOPK)


def _cmp_exchange(xs, i, j):
    xs[i], xs[j] = jnp.maximum(xs[i], xs[j]), jnp.minimum(xs[i], xs[j])


def _top16_sorted(s):
    n = PEER_TOPK
    xs = [s[8 * k:8 * k + 8, :] for k in range(n)]
    for i, j in _SORT16_PAIRS:
        _cmp_exchange(xs, i, j)
    for shift in (4, 2, 1):
        other = [pltpu.roll(x, shift, 0) for x in xs]
        xs = [jnp.maximum(xs[k], other[n - 1 - k]) for k in range(n)]
        for d in (8, 4, 2, 1):
            for k in range(n):
                if k & d == 0:
                    _cmp_exchange(xs, k, k + d)
    return xs


def _candidates(v1, v2, cand_scr):
    L = v1[0].shape[1]
    cand_scr[...] = jnp.full((_CAND_ROWS, L), -jnp.inf, F32)
    for a in range(PEER_TOPK):
        for b in range(_CAND_COUNT[a]):
            cand_scr[pl.ds(_CAND_ROW0[a] + b, 1), :] = v1[a] + v2[b]
    return cand_scr[...], lax.broadcasted_iota(jnp.int32, (_CAND_ROWS, L), 0).astype(F32)


def _taken_by_index(cand, iota_c):
    work = cand
    taken = jnp.zeros(cand.shape, F32)
    for _ in range(PEER_TOPK):
        m = jnp.max(work, axis=0, keepdims=True)
        idx = jnp.min(jnp.where(work == m, iota_c, float(_CAND_ROWS)), axis=0, keepdims=True)
        sel = iota_c == idx
        taken = jnp.where(sel, 1.0, taken)
        work = jnp.where(sel, -jnp.inf, work)
    return taken


_SORT8_PAIRS = _sort_pairs(8)


def _candidates_sorted(cand):
    n = PEER_TOPK
    xs = [cand[8 * k:8 * k + 8, :] for k in range(_CAND_ROWS // 8)]
    xs.append(jnp.full_like(xs[0], -jnp.inf))
    for i, j in _SORT8_PAIRS:
        _cmp_exchange(xs, i, j)
    other = [pltpu.roll(x, 4, 0) for x in xs]
    xs = xs + other[::-1]
    for shift in (4, 2, 1):
        if shift != 4:
            other = [pltpu.roll(x, shift, 0) for x in xs]
            xs = [jnp.maximum(xs[k], other[n - 1 - k]) for k in range(n)]
        for d in (8, 4, 2, 1):
            for k in range(n):
                if k & d == 0:
                    _cmp_exchange(xs, k, k + d)
    return xs


def _group_counts(taken, iota_c):
    cnt = []
    for a in range(PEER_TOPK):
        lo, hi = _CAND_ROW0[a], _CAND_ROW0[a] + _CAND_COUNT[a]
        if hi - lo == 1:
            cnt.append(taken[lo:hi])
            continue
        t0, t1 = (lo // 8) * 8, -(-hi // 8) * 8
        part = taken[t0:t1]
        if (lo, hi) != (t0, t1):
            rows = iota_c[t0:t1]
            part = jnp.where((rows >= float(lo)) & (rows < float(hi)), part, 0.0)
        cnt.append(jnp.sum(part, axis=0, keepdims=True))
    return cnt


def _route_tiles(tiles):
    n = PEER_TOPK
    states = []
    bad_any = None
    for s1, s2, cand_scr, n1_out, r2_out in tiles:
        t1 = _top16_sorted(s1)
        t2 = _top16_sorted(s2)
        v1 = [t[0:1, :] for t in t1]
        v2 = [t[0:1, :] for t in t2]
        cand, iota_c = _candidates(v1, v2, cand_scr)
        tc = _candidates_sorted(cand)
        z = jnp.ones_like(v1[0])
        for k in range(1, n):
            z = z + jnp.exp(tc[k][0:1, :] - tc[0][0:1, :])
        taken = jnp.where(cand >= tc[n - 1][0:1, :], 1.0, 0.0)
        cnt = _group_counts(taken, iota_c)

        n_taken = jnp.sum(taken, axis=0, keepdims=True)
        bad = jnp.broadcast_to(jnp.where(n_taken != float(n), 1.0, 0.0), (8, s1.shape[1]))
        for s, t in ((s1, t1), (s2, t2)):
            gap = t[0] - t[1]
            for a in range(1, n - 1):
                gap = jnp.minimum(gap, t[a] - t[a + 1])
            at_least = jnp.zeros((8, s.shape[1]), F32)
            for k in range(n):
                at_least = at_least + jnp.where(s[8 * k:8 * k + 8, :] >= t[n - 1], 1.0, 0.0)
            at_least = jnp.sum(at_least, axis=0, keepdims=True)
            bad = jnp.where((gap <= 0.0) | (at_least != float(n)), 1.0, bad)
        bad_any = bad if bad_any is None else jnp.maximum(bad_any, bad)
        states.append((s1, s2, t1, t2, v1, v2, cand, iota_c, cnt, z, n1_out, r2_out))
    has_ties = jnp.max(bad_any) > 0.0

    @pl.when(jnp.logical_not(has_ties))
    def _by_value():
        for s1, s2, t1, t2, _, _, _, _, cnt, _, n1_out, r2_out in states:
            cnt_b = [jnp.broadcast_to(c, (8, c.shape[1])) for c in cnt]
            ranks = []
            for k in range(n):
                x1 = s1[8 * k:8 * k + 8, :]
                x2 = s2[8 * k:8 * k + 8, :]
                n1 = cnt_b[0]
                r2 = jnp.zeros_like(x2)
                for a in range(n):
                    n1 = jnp.where(t1[a] > x1, cnt_b[a + 1] if a + 1 < n else 0.0, n1)
                    r2 = jnp.where(t2[a] > x2, float(a + 1), r2)
                n1_out[8 * k:8 * k + 8, :] = n1
                ranks.append(r2)
            r2_out[...] = jnp.concatenate(ranks, axis=0).astype(r2_out.dtype)

    @pl.when(has_ties)
    def _by_index():
        for s1, s2, _, _, _, _, cand, iota_c, _, _, n1_out, r2_out in states:
            iota = lax.broadcasted_iota(jnp.int32, s1.shape, 0).astype(F32)
            _, r1 = _top16(s1, iota)
            _, r2 = _top16(s2, iota)
            cnt_exact = _group_counts(_taken_by_index(cand, iota_c), iota_c)
            n1 = jnp.zeros(s1.shape, F32)
            for a in range(n):
                n1 = jnp.where(r1 == float(a), cnt_exact[a], n1)
            n1_out[...] = n1
            r2_out[...] = r2.astype(r2_out.dtype)

    return [(jnp.exp(st[0] - st[4][0]) / st[9], jnp.exp(st[1] - st[5][0])) for st in states]


def _peer_kernel(xn_ref, h_ref, wqt_ref, k1_ref, k2_ref, u_ref, vt_ref, y_ref,
                 n1_scr, c1_scr, s2_scr, r2_scr, e2_scr, acc_scr, a_scr, w_scr,
                 q_scr, xt_scr, cand_scr, *, n_steps):
    g = pl.program_id(1)
    ng = n_steps
    tt = xn_ref.shape[0]
    eb = u_ref.shape[0]
    keys_per_block = eb // PEER_KEYS
    n_lane_tiles = tt // LANES

    @pl.when(g == 0)
    def _route():
        xt_scr[...] = xn_ref[...].T
        xt = xt_scr[...]
        q_scr[...] = _dot(wqt_ref[...], xt).astype(BF16)
        for h in range(PEER_HEADS):
            for side, dst in enumerate((n1_scr, s2_scr)):
                k_ref = (k1_ref, k2_ref)[side]
                r0 = h * 2 * PEER_HALF + side * PEER_HALF
                s = _dot(k_ref[...], q_scr[r0:r0 + PEER_HALF, :])
                for c in range(n_lane_tiles):
                    dst[h, c] = s[:, c * LANES:(c + 1) * LANES]

        per_trip = PEER_ROUTE_TILES
        trips_per_head = n_lane_tiles // per_trip

        def body(i, carry):
            h = i // trips_per_head
            cs = [(i % trips_per_head) * per_trip + k for k in range(per_trip)]
            outs = _route_tiles([(n1_scr[h, c], s2_scr[h, c], cand_scr.at[k], n1_scr.at[h, c], r2_scr.at[h, c])
                                 for k, c in enumerate(cs)])
            for c, (c1, e2) in zip(cs, outs):
                c1_scr[h, c] = c1
                e2_scr[h, c] = e2.astype(BF16)
            return carry

        lax.fori_loop(0, PEER_HEADS * trips_per_head, body, 0)
        acc_scr[...] = jnp.zeros_like(acc_scr)

    def gate(a_scr, w_scr, block):
        zero = jnp.zeros((), BF16)
        group = PEER_GATE_KEYS
        for k0 in range(0, keys_per_block, group):
            def lane_tile(c, carry, k0=k0):
                lanes = pl.ds(pl.multiple_of(c * LANES, LANES), LANES)
                gts = [None] * group
                for h in range(PEER_HEADS):
                    r2 = r2_scr[h, c].reshape(PEER_KEYS // 16, 16, LANES)
                    e2 = e2_scr[h, c].reshape(PEER_KEYS // 16, 16, LANES)
                    for ii in range(group):
                        row = pl.ds(block * keys_per_block + k0 + ii, 1)
                        n_b = jnp.broadcast_to(n1_scr[h, c, row, :], (16, LANES)).astype(BF16)[None]
                        c_b = jnp.broadcast_to(c1_scr[h, c, row, :], (16, LANES)).astype(BF16)[None]
                        term = jnp.where(r2 < n_b, e2, zero) * c_b
                        gts[ii] = term if h == 0 else gts[ii] + term
                for ii in range(group):
                    r0 = (k0 + ii) * PEER_KEYS
                    rows = slice(r0, r0 + PEER_KEYS)
                    w_scr[rows, lanes] = gts[ii].reshape(PEER_KEYS, LANES) * a_scr[rows, lanes]
                return carry

            lax.fori_loop(0, n_lane_tiles, lane_tile, 0)

    @pl.when(g < ng)
    def _scores():
        a_scr[...] = _gelu(_dot(u_ref[...], xt_scr[...]).astype(BF16))

    @pl.when(g > 0)
    def _v_product():
        acc_scr[...] += _dot(vt_ref[...], w_scr[...])

    @pl.when(g < ng)
    def _weights():
        gate(a_scr, w_scr, g)

    @pl.when(g == ng)
    def _fin():
        y_ref[...] = h_ref[...] + acc_scr[...].T


def _peer(xn, h, wqt, k1, k2, u_tab, vt_tab):
    n = xn.shape[0]
    tt = min(TOKEN_TILE, n)
    eb = PEER_EXPERT_BLOCK
    ng = PEER_EXPERTS // eb
    full = lambda shape: pl.BlockSpec(shape, lambda i, g: (0,) * len(shape), pipeline_mode=pl.Buffered(1))
    tok = pl.BlockSpec((tt, D_MODEL), lambda i, g: (i, 0))
    head_f32 = pltpu.VMEM((PEER_HEADS, tt // LANES, PEER_KEYS, LANES), F32)
    head_bf16 = pltpu.VMEM((PEER_HEADS, tt // LANES, PEER_KEYS, LANES), BF16)
    return pl.pallas_call(
        functools.partial(_peer_kernel, n_steps=ng),
        grid=(n // tt, ng + 1),
        in_specs=[tok, tok, full((2 * PEER_HEADS * PEER_HALF, D_MODEL)),
                  full((PEER_KEYS, PEER_HALF)), full((PEER_KEYS, PEER_HALF)),
                  pl.BlockSpec((eb, D_MODEL), lambda i, g: (jnp.minimum(g, ng - 1), 0)),
                  pl.BlockSpec((D_MODEL, eb), lambda i, g: (0, jnp.maximum(g - 1, 0)))],
        out_specs=tok,
        out_shape=jax.ShapeDtypeStruct((n, D_MODEL), F32),
        scratch_shapes=[head_f32, head_f32, head_f32, head_bf16, head_bf16,
                        pltpu.VMEM((D_MODEL, tt), F32),
                        pltpu.VMEM((eb, tt), BF16), pltpu.VMEM((eb, tt), BF16),
                        pltpu.VMEM((2 * PEER_HEADS * PEER_HALF, tt), BF16),
                        pltpu.VMEM((D_MODEL, tt), BF16),
                        pltpu.VMEM((PEER_ROUTE_TILES, _CAND_ROWS, LANES), F32)],
        compiler_params=_cparams(("parallel", "arbitrary")),
        name="peer",
    )(xn, h, wqt, k1, k2, u_tab, vt_tab)


def _rope_tables(pos):
    half = HEAD_DIM // 2
    inv = ROPE_THETA ** (-jnp.arange(half, dtype=F32) / half)
    ang = pos.astype(F32)[:, None] * inv[None, :]
    cos = jnp.cos(ang)
    sin = jnp.sin(ang)
    cos = jnp.tile(jnp.concatenate([cos, cos], axis=-1), (1, SWA_Q_HEADS))
    sin = jnp.tile(jnp.concatenate([-sin, sin], axis=-1), (1, SWA_Q_HEADS))
    return cos, sin


def _ssm_params(log_dt, a_re, a_im, b_re, b_im, c_re, c_im):
    dt = jnp.exp(log_dt)
    mag = jnp.exp(a_re * dt)
    lam_re = mag * jnp.cos(a_im * dt)
    lam_im = mag * jnp.sin(a_im * dt)
    den = a_re * a_re + a_im * a_im
    z_re = ((lam_re - 1.0) * a_re + lam_im * a_im) / den
    z_im = (lam_im * a_re - (lam_re - 1.0) * a_im) / den
    bb_re = z_re[..., None] * b_re - z_im[..., None] * b_im
    bb_im = z_re[..., None] * b_im + z_im[..., None] * b_re
    hg = SSM_HALF_GROUPS
    eye = jnp.eye(hg, dtype=F32)
    bb = jnp.stack([bb_re, bb_im]).reshape(2, 2, hg, SSM_STATE, SSM_GROUP)
    bmat = jnp.einsum('rjgnc,gh->jgcrhn', bb, eye).reshape(2, hg * SSM_GROUP, 2 * SSM_HALF_STATE)
    cc = jnp.stack([c_re, -c_im]).reshape(2, 2, hg, SSM_GROUP, SSM_STATE)
    cmat = jnp.einsum('rjgcn,gh->jrgnhc', cc, eye).reshape(2, 2 * SSM_HALF_STATE, hg * SSM_GROUP)
    lam = jnp.stack([lam_re.reshape(2, SSM_HALF_STATE), lam_im.reshape(2, SSM_HALF_STATE)], axis=1)
    return lam.reshape(1, SSM_COLS), bmat.astype(BF16), cmat.astype(BF16)


def _state_to_cols(s_re, s_im):
    b = s_re.shape[0]
    st = jnp.stack([s_re.reshape(b, 2, SSM_HALF_STATE), s_im.reshape(b, 2, SSM_HALF_STATE)], axis=2)
    return st.reshape(b, SSM_COLS)


def _cols_to_state(cols):
    b = cols.shape[0]
    st = cols.reshape(b, 2, 2, SSM_HALF_STATE)
    return (st[:, :, 0].reshape(b, SSM_GROUPS, SSM_STATE), st[:, :, 1].reshape(b, SSM_GROUPS, SSM_STATE))


def kernel(x_prompt, x_sample, state_ssm_re, state_ssm_im, cache_win_k, cache_win_v, cache_mem_k, cache_mem_v, mem_prompt, norm1_g, w_in, ssm_log_dt, ssm_a_re, ssm_a_im, ssm_b_re, ssm_b_im, ssm_c_re, ssm_c_im, ssm_d, w_glu, b_glu, swa_q_norm, swa_k_norm, swa_sinks, mem_norm_g, w_mem_kv, mem_q_norm, mem_k_norm, w_out, norm2_g, peer_wq, peer_k1, peer_k2, peer_u, peer_v):
    depth = norm1_g.shape[0]
    assert depth == 1
    l = 0
    B, T, _ = x_prompt.shape
    SB, ST, _ = x_sample.shape
    w = cache_win_k.shape[2]

    row = lambda a: a.reshape(1, -1)
    g1 = row(norm1_g[l])
    g2 = row(norm2_g[l])
    win = w_in[l].astype(BF16)
    wo = w_out[l].astype(BF16)
    wglu = w_glu[l].astype(BF16)
    bglu = row(b_glu[l])
    gq = row(jnp.tile(swa_q_norm[l], SWA_Q_HEADS))
    gk = row(jnp.tile(swa_k_norm[l], SWA_KV_HEADS))
    gm = row(jnp.tile(mem_q_norm[l], MEM_HEADS))
    gmk = row(jnp.tile(mem_k_norm[l], MEM_HEADS))
    gmem = row(mem_norm_g[l])
    wkv = w_mem_kv[l].astype(BF16)
    sinks = swa_sinks[l]
    head_id = np.arange(SWA_WIDTH) // HEAD_DIM
    ones = jnp.asarray(head_id[:, None] == head_id[None, :], dtype=BF16)
    lam, bmat, cmat = _ssm_params(ssm_log_dt[l], ssm_a_re[l], ssm_a_im[l], ssm_b_re[l], ssm_b_im[l],
                                  ssm_c_re[l], ssm_c_im[l])
    dskip = row(ssm_d[l])
    wqt = peer_wq[l].astype(BF16).T
    k1 = peer_k1[l].astype(BF16)
    k2 = peer_k2[l].astype(BF16)
    u_tab = peer_u[l].astype(BF16)
    vt_tab = peer_v[l].astype(BF16).T

    cos_p, sin_p = _rope_tables(jnp.arange(T, dtype=jnp.int32))
    u_p, q_p, k_p, v_p, qm_p = _in_proj(x_prompt, cos_p, sin_p, g1, win, ones, gq, gk, gm)
    zeros = jnp.zeros((B, SSM_COLS), F32)
    ossm_p, sfin_p = _s5(u_p.reshape(T * B, SSM_WIDTH), zeros, lam, bmat, cmat, dskip, wglu, bglu,
                         bt=B, tt=S5_TIME_TILE)
    mk, mv = _mem_kv(mem_prompt.reshape(B * MEM_TOKENS, D_MODEL), gmem, wkv, ones, gmk)
    mk = mk.reshape(B, MEM_TOKENS, MEM_WIDTH)
    mv = mv.reshape(B, MEM_TOKENS, MEM_WIDTH)
    osw_p, omem_p = _attn_prompt(sinks, q_p, k_p, v_p, qm_p, mk, mv)
    h_p, xn_p = _out_proj(x_prompt, ossm_p.reshape(T, B * SSM_WIDTH), osw_p, omem_p, wo, g2)
    y_p = _peer(xn_p.reshape(B * T, D_MODEL), h_p.reshape(B * T, D_MODEL), wqt, k1, k2, u_tab, vt_tab)
    y_p = y_p.reshape(B, T, D_MODEL)
    p_sr, p_si = _cols_to_state(sfin_p)
    p_wk = k_p[:, T - w:].reshape(B, w, SWA_KV_HEADS, HEAD_DIM)
    p_wv = v_p[:, T - w:].reshape(B, w, SWA_KV_HEADS, HEAD_DIM)
    p_mk = mk.reshape(B, MEM_TOKENS, MEM_HEADS, HEAD_DIM)
    p_mv = mv.reshape(B, MEM_TOKENS, MEM_HEADS, HEAD_DIM)

    n_s = SB * ST
    pos_s = PAST_LEN + jnp.tile(jnp.arange(ST, dtype=jnp.int32), SB)
    cos_s, sin_s = _rope_tables(pos_s)
    xs = x_sample.reshape(1, n_s, D_MODEL)
    u_s, q_s, k_s, v_s, qm_s = _in_proj(xs, cos_s, sin_s, g1, win, ones, gq, gk, gm)
    u_tm = u_s.reshape(SB, ST, SSM_WIDTH).transpose(1, 0, 2).reshape(n_s, SSM_WIDTH)
    ossm_tm, sfin_s = _s5(u_tm, _state_to_cols(state_ssm_re[l], state_ssm_im[l]), lam, bmat, cmat,
                          dskip, wglu, bglu, bt=SB, tt=ST)
    ossm_s = ossm_tm.reshape(ST, SB, SSM_WIDTH).transpose(1, 0, 2).reshape(n_s, SSM_WIDTH)
    ck = cache_win_k[l].reshape(SB, w, SWA_KV_WIDTH)
    cv = cache_win_v[l].reshape(SB, w, SWA_KV_WIDTH)
    q_s2 = q_s.reshape(n_s, SWA_WIDTH)
    k_s2 = k_s.reshape(n_s, SWA_KV_WIDTH)
    v_s2 = v_s.reshape(n_s, SWA_KV_WIDTH)
    osw_s, omem_s = _attn_sample(sinks, q_s2, k_s2, v_s2, ck, cv, qm_s.reshape(n_s, MEM_WIDTH),
                                 cache_mem_k[l].reshape(SB, MEM_TOKENS, MEM_WIDTH),
                                 cache_mem_v[l].reshape(SB, MEM_TOKENS, MEM_WIDTH),
                                 ts=ST, start=PAST_LEN)
    h_s, xn_s = _out_proj(xs, ossm_s, osw_s.reshape(1, n_s, SWA_WIDTH), omem_s.reshape(1, n_s, MEM_WIDTH),
                          wo, g2)
    y_s = _peer(xn_s.reshape(n_s, D_MODEL), h_s.reshape(n_s, D_MODEL), wqt, k1, k2, u_tab, vt_tab)
    y_s = y_s.reshape(SB, ST, D_MODEL)
    s_sr, s_si = _cols_to_state(sfin_s)
    s_wk = jnp.concatenate([ck, k_s2.reshape(SB, ST, SWA_KV_WIDTH)], axis=1)[:, -w:]
    s_wv = jnp.concatenate([cv, v_s2.reshape(SB, ST, SWA_KV_WIDTH)], axis=1)[:, -w:]
    s_wk = s_wk.reshape(SB, w, SWA_KV_HEADS, HEAD_DIM)
    s_wv = s_wv.reshape(SB, w, SWA_KV_HEADS, HEAD_DIM)

    st = lambda a: a[None]
    return (y_p, y_s, st(p_sr), st(p_si), st(p_wk), st(p_wv), st(p_mk), st(p_mv),
            st(s_sr), st(s_si), st(s_wk), st(s_wv))
```

```python
import functools
import math

import jax
import jax.numpy as jnp
import numpy as np
from jax import lax
from jax.experimental import pallas as pl
from jax.experimental.pallas import tpu as pltpu

F32 = jnp.float32
BF16 = jnp.bfloat16

D_MODEL = 1024
HEAD_DIM = 64
EPS = 1e-6
ROPE_THETA = 10000.0
PAST_LEN = 8192
SSM_WIDTH = 512
SSM_GROUP = 16
SSM_GROUPS = 32
SSM_STATE = 64
SSM_HALF_GROUPS = SSM_GROUPS // 2
SSM_HALF_STATE = SSM_HALF_GROUPS * SSM_STATE
SSM_COLS = 2 * 2 * SSM_HALF_STATE
SWA_Q_HEADS = 4
SWA_KV_HEADS = 2
SWA_REP = 2
SWA_WIDTH = 256
SWA_KV_WIDTH = 128
WINDOW = 128
MEM_TOKENS = 256
MEM_HEADS = 4
MEM_WIDTH = 256
IN_WIDTH = 1280
PEER_HEADS = 8
PEER_KEYS = 128
PEER_EXPERTS = PEER_KEYS * PEER_KEYS
PEER_TOPK = 16
PEER_HALF = 128

LANES = 128
VMEM_LIMIT = 60 * 1024 * 1024

TOKEN_TILE = 512
PEER_EXPERT_BLOCK = 2048
PEER_GATE_KEYS = 4
PEER_ROUTE_TILES = 2
S5_TIME_TILE = 64
ATTN_BLOCKS = 2

_CAND_COUNT = [PEER_TOPK // (a + 1) for a in range(PEER_TOPK)]
_CAND_ROW0 = [0, 16, 24, 32, 36, 40, 42, 44, 48, 49, 50, 51, 52, 53, 54, 55]
_CAND_ROWS = 56


def _cparams(sem):
    return pltpu.CompilerParams(dimension_semantics=sem, vmem_limit_bytes=VMEM_LIMIT)


def _rms(x, g):
    return x * lax.rsqrt(jnp.mean(x * x, axis=-1, keepdims=True) + EPS) * g


def _gelu(x):
    return 0.5 * x * (1.0 + lax.erf(x * math.sqrt(0.5)))


def _dot(a, b):
    return jnp.dot(a, b, preferred_element_type=F32)


def _dot_nt(a, b):
    return lax.dot_general(a, b, (((1,), (1,)), ((), ())), preferred_element_type=F32)


def _div(x, n):
    return x >> (n.bit_length() - 1) if n & (n - 1) == 0 else x // n


def _mod(x, n):
    return x & (n - 1) if n & (n - 1) == 0 else x % n


def _head_rms(x, ones_bd, g):
    sq = x * x
    hi = sq.astype(BF16)
    lo = (sq - hi.astype(F32)).astype(BF16)
    ms = (_dot(hi, ones_bd) + _dot(lo, ones_bd)) * np.float32(1.0 / HEAD_DIM)
    return x * lax.rsqrt(ms + EPS) * g


def _rope(x, cos, sin_signed):
    w = x.shape[-1]
    lane = lax.broadcasted_iota(jnp.int32, x.shape, 1)
    first_half = _mod(lane, HEAD_DIM) < (HEAD_DIM // 2)
    partner = jnp.where(first_half, pltpu.roll(x, w - HEAD_DIM // 2, 1),
                        pltpu.roll(x, HEAD_DIM // 2, 1))
    return x * cos + partner * sin_signed


def _in_proj_kernel(x_ref, g1_ref, win_ref, ones_ref, gq_ref, gk_ref, gm_ref, cos_ref, sin_ref,
                    u_ref, q_ref, k_ref, v_ref, qm_ref):
    x = x_ref[0]
    xn = _rms(x, g1_ref[...])
    proj = _dot(xn.astype(BF16), win_ref[...])
    u_ref[...] = proj[:, :SSM_WIDTH]
    q = proj[:, 512:768]
    k = proj[:, 768:896]
    v_ref[0] = proj[:, 896:1024]
    qm = proj[:, 1024:1280]
    ones = ones_ref[...]
    cos = cos_ref[...]
    sin = sin_ref[...]
    q_ref[0] = _rope(_head_rms(q, ones, gq_ref[...]), cos, sin)
    k_ref[0] = _rope(_head_rms(k, ones[:SWA_KV_WIDTH, :SWA_KV_WIDTH], gk_ref[...]),
                     cos[:, :SWA_KV_WIDTH], sin[:, :SWA_KV_WIDTH])
    qm_ref[0] = _head_rms(qm, ones, gm_ref[...])


def _in_proj(x, cos, sin, g1, win, ones, gq, gk, gm):
    B, T, _ = x.shape
    tt = min(TOKEN_TILE, T)
    grid = (B, T // tt)
    full = lambda shape: pl.BlockSpec(shape, lambda b, t: (0,) * len(shape))
    return pl.pallas_call(
        _in_proj_kernel,
        grid=grid,
        in_specs=[
            pl.BlockSpec((1, tt, D_MODEL), lambda b, t: (b, t, 0)),
            full((1, D_MODEL)), full((D_MODEL, IN_WIDTH)), full((SWA_WIDTH, SWA_WIDTH)),
            full((1, SWA_WIDTH)), full((1, SWA_KV_WIDTH)), full((1, MEM_WIDTH)),
            pl.BlockSpec((tt, SWA_WIDTH), lambda b, t: (t, 0)),
            pl.BlockSpec((tt, SWA_WIDTH), lambda b, t: (t, 0)),
        ],
        out_specs=[
            pl.BlockSpec((tt, SSM_WIDTH), lambda b, t: (t, b)),
            pl.BlockSpec((1, tt, SWA_WIDTH), lambda b, t: (b, t, 0)),
            pl.BlockSpec((1, tt, SWA_KV_WIDTH), lambda b, t: (b, t, 0)),
            pl.BlockSpec((1, tt, SWA_KV_WIDTH), lambda b, t: (b, t, 0)),
            pl.BlockSpec((1, tt, MEM_WIDTH), lambda b, t: (b, t, 0)),
        ],
        out_shape=[
            jax.ShapeDtypeStruct((T, B * SSM_WIDTH), F32),
            jax.ShapeDtypeStruct((B, T, SWA_WIDTH), F32),
            jax.ShapeDtypeStruct((B, T, SWA_KV_WIDTH), F32),
            jax.ShapeDtypeStruct((B, T, SWA_KV_WIDTH), F32),
            jax.ShapeDtypeStruct((B, T, MEM_WIDTH), F32),
        ],
        compiler_params=_cparams(("parallel", "parallel")),
        name="in_proj",
    )(x, g1, win, ones, gq, gk, gm, cos, sin)


def _s5_kernel(u_ref, s0_ref, lam_ref, bmat_ref, cmat_ref, d_ref, wglu_ref, bglu_ref,
               o_ref, sfin_ref, s_scr, carry_scr, *, bt, tt):
    @pl.when(pl.program_id(0) == 0)
    def _():
        carry_scr[...] = s0_ref[...]

    u = u_ref[...]
    ub = u.astype(BF16)
    hw = 2 * SSM_HALF_STATE
    for j in range(2):
        s_scr[:, j * hw:(j + 1) * hw] = _dot(ub[:, j * 256:(j + 1) * 256], bmat_ref[j])

    def step(t, carry):
        r0 = pl.multiple_of(t * bt, bt)
        for j in range(2):
            c_re = pl.ds(j * hw, SSM_HALF_STATE)
            c_im = pl.ds(j * hw + SSM_HALF_STATE, SSM_HALF_STATE)
            p_re = carry_scr[:, c_re]
            p_im = carry_scr[:, c_im]
            l_re = lam_ref[:, c_re]
            l_im = lam_ref[:, c_im]
            n_re = l_re * p_re - l_im * p_im + s_scr[pl.ds(r0, bt), c_re]
            n_im = l_re * p_im + l_im * p_re + s_scr[pl.ds(r0, bt), c_im]
            s_scr[pl.ds(r0, bt), c_re] = n_re
            s_scr[pl.ds(r0, bt), c_im] = n_im
            carry_scr[:, c_re] = n_re
            carry_scr[:, c_im] = n_im
        return carry

    lax.fori_loop(0, tt, step, 0)
    sfin_ref[...] = carry_scr[...]

    ys = [_dot(s_scr[:, j * hw:(j + 1) * hw].astype(BF16), cmat_ref[j]) for j in range(2)]
    y = jnp.concatenate(ys, axis=-1) + d_ref[...] * u
    y = _gelu(y)
    z = _dot(y.astype(BF16), wglu_ref[...]) + bglu_ref[...]
    o_ref[...] = y * jax.nn.sigmoid(z)


def _s5(u_tm, s0, lam, bmat, cmat, d, wglu, bglu, *, bt, tt):
    rows = u_tm.shape[0]
    nt = rows // (bt * tt)
    full = lambda shape: pl.BlockSpec(shape, lambda t: (0,) * len(shape))
    return pl.pallas_call(
        functools.partial(_s5_kernel, bt=bt, tt=tt),
        grid=(nt,),
        in_specs=[
            pl.BlockSpec((bt * tt, SSM_WIDTH), lambda t: (t, 0)),
            full((bt, SSM_COLS)), full((1, SSM_COLS)),
            full((2, 256, 2 * SSM_HALF_STATE)), full((2, 2 * SSM_HALF_STATE, 256)),
            full((1, SSM_WIDTH)), full((SSM_WIDTH, SSM_WIDTH)), full((1, SSM_WIDTH)),
        ],
        out_specs=[
            pl.BlockSpec((bt * tt, SSM_WIDTH), lambda t: (t, 0)),
            full((bt, SSM_COLS)),
        ],
        out_shape=[
            jax.ShapeDtypeStruct((rows, SSM_WIDTH), F32),
            jax.ShapeDtypeStruct((bt, SSM_COLS), F32),
        ],
        scratch_shapes=[pltpu.VMEM((bt * tt, SSM_COLS), F32), pltpu.VMEM((bt, SSM_COLS), F32)],
        compiler_params=_cparams(("arbitrary",)),
        name="s5_mixer",
    )(u_tm, s0, lam, bmat, cmat, d, wglu, bglu)


def _mem_kv_kernel(m_ref, g_ref, w_ref, ones_ref, gk_ref, k_ref, v_ref):
    xn = _rms(m_ref[...], g_ref[...])
    kv = _dot(xn.astype(BF16), w_ref[...])
    k_ref[...] = _head_rms(kv[:, :MEM_WIDTH], ones_ref[...], gk_ref[...])
    v_ref[...] = kv[:, MEM_WIDTH:]


def _mem_kv(mem_rows, g, w, ones, gk):
    rows = mem_rows.shape[0]
    tt = min(TOKEN_TILE, rows)
    full = lambda shape: pl.BlockSpec(shape, lambda t: (0,) * len(shape))
    return pl.pallas_call(
        _mem_kv_kernel,
        grid=(rows // tt,),
        in_specs=[pl.BlockSpec((tt, D_MODEL), lambda t: (t, 0)), full((1, D_MODEL)),
                  full((D_MODEL, 2 * MEM_WIDTH)), full((MEM_WIDTH, MEM_WIDTH)), full((1, MEM_WIDTH))],
        out_specs=[pl.BlockSpec((tt, MEM_WIDTH), lambda t: (t, 0))] * 2,
        out_shape=[jax.ShapeDtypeStruct((rows, MEM_WIDTH), F32)] * 2,
        compiler_params=_cparams(("parallel",)),
        name="mem_kv",
    )(mem_rows, g, w, ones, gk)


def _softmax_pv(s, v_b, sink=None):
    m = jnp.max(s, axis=-1, keepdims=True)
    if sink is not None:
        m = jnp.maximum(m, sink)
    p = jnp.exp(s - m)
    den = jnp.sum(p, axis=-1, keepdims=True)
    if sink is not None:
        den = den + jnp.exp(sink - m)
    return _dot((p / den).astype(BF16), v_b)


def _attn_prompt_kernel(sinks_ref, q_ref, kp_ref, kc_ref, vp_ref, vc_ref, qm_ref, mk_ref, mv_ref,
                        osw_ref, omem_ref):
    n = pl.program_id(1)
    scale = np.float32(HEAD_DIM ** -0.5)
    q = q_ref[0].astype(BF16)
    kk = jnp.concatenate([kp_ref[0], kc_ref[0]], axis=0).astype(BF16)
    vv = jnp.concatenate([vp_ref[0], vc_ref[0]], axis=0).astype(BF16)
    qi = lax.broadcasted_iota(jnp.int32, (WINDOW, 2 * WINDOW), 0)
    ki = lax.broadcasted_iota(jnp.int32, (WINDOW, 2 * WINDOW), 1) - WINDOW
    diff = qi - ki
    for j in range(ATTN_BLOCKS):
        valid = (diff >= 0) & (diff < WINDOW) & ((n * ATTN_BLOCKS + j) * WINDOW + ki >= 0)
        rows = slice(j * WINDOW, (j + 1) * WINDOW)
        keys = slice(j * WINDOW, (j + 2) * WINDOW)
        outs = []
        for hq in range(SWA_Q_HEADS):
            kv = hq // SWA_REP
            s = _dot_nt(q[rows, hq * HEAD_DIM:(hq + 1) * HEAD_DIM],
                        kk[keys, kv * HEAD_DIM:(kv + 1) * HEAD_DIM]) * scale
            s = jnp.where(valid, s, -jnp.inf)
            outs.append(_softmax_pv(s, vv[keys, kv * HEAD_DIM:(kv + 1) * HEAD_DIM], sinks_ref[hq]))
        osw_ref[0, rows, :] = jnp.concatenate(outs, axis=-1)

    qm = qm_ref[0].astype(BF16)
    mk = mk_ref[0].astype(BF16)
    mv = mv_ref[0].astype(BF16)
    outs = []
    for h in range(MEM_HEADS):
        sl = slice(h * HEAD_DIM, (h + 1) * HEAD_DIM)
        s = _dot_nt(qm[:, sl], mk[:, sl]) * scale
        outs.append(_softmax_pv(s, mv[:, sl]))
    omem_ref[0] = jnp.concatenate(outs, axis=-1)


def _attn_prompt(sinks, q, k, v, qm, mk, mv):
    B, T, _ = q.shape
    rows = ATTN_BLOCKS * WINDOW
    blk = lambda w: pl.BlockSpec((1, rows, w), lambda b, n: (b, n, 0))
    prev = lambda w: pl.BlockSpec((1, WINDOW, w), lambda b, n: (b, jnp.maximum(n * ATTN_BLOCKS - 1, 0), 0))
    memb = pl.BlockSpec((1, MEM_TOKENS, MEM_WIDTH), lambda b, n: (b, 0, 0))
    return pl.pallas_call(
        _attn_prompt_kernel,
        grid=(B, T // rows),
        in_specs=[pl.BlockSpec(memory_space=pltpu.SMEM),
                  blk(SWA_WIDTH), prev(SWA_KV_WIDTH), blk(SWA_KV_WIDTH), prev(SWA_KV_WIDTH),
                  blk(SWA_KV_WIDTH), blk(MEM_WIDTH), memb, memb],
        out_specs=[blk(SWA_WIDTH), blk(MEM_WIDTH)],
        out_shape=[jax.ShapeDtypeStruct((B, T, SWA_WIDTH), F32),
                   jax.ShapeDtypeStruct((B, T, MEM_WIDTH), F32)],
        compiler_params=_cparams(("parallel", "parallel")),
        name="attn_prompt",
    )(sinks, q, k, k, v, v, qm, mk, mv)


def _attn_sample_kernel(sinks_ref, q_ref, kn_ref, vn_ref, ck_ref, cv_ref, qm_ref, mk_ref, mv_ref,
                        osw_ref, omem_ref, *, bb, ts, start):
    scale = np.float32(HEAD_DIM ** -0.5)
    w = ck_ref.shape[1]
    nq = bb * ts
    q = q_ref[...].astype(BF16)
    kn = kn_ref[...].astype(BF16)
    vn = vn_ref[...].astype(BF16)
    ck = ck_ref[...].reshape(bb * w, SWA_KV_WIDTH).astype(BF16)
    cv = cv_ref[...].reshape(bb * w, SWA_KV_WIDTH).astype(BF16)

    rq = lax.broadcasted_iota(jnp.int32, (nq, bb * w), 0)
    cc = lax.broadcasted_iota(jnp.int32, (nq, bb * w), 1)
    qpos = start + _mod(rq, ts)
    kpos = start - w + _mod(cc, w)
    diff = qpos - kpos
    valid_c = (_div(rq, ts) == _div(cc, w)) & (diff >= 0) & (diff < WINDOW) & (kpos >= 0)
    rq = lax.broadcasted_iota(jnp.int32, (nq, nq), 0)
    cn = lax.broadcasted_iota(jnp.int32, (nq, nq), 1)
    diff = _mod(rq, ts) - _mod(cn, ts)
    valid_n = (_div(rq, ts) == _div(cn, ts)) & (diff >= 0) & (diff < WINDOW)

    outs = []
    for hq in range(SWA_Q_HEADS):
        kv = hq // SWA_REP
        qs = q[:, hq * HEAD_DIM:(hq + 1) * HEAD_DIM]
        ks = slice(kv * HEAD_DIM, (kv + 1) * HEAD_DIM)
        s_c = jnp.where(valid_c, _dot_nt(qs, ck[:, ks]) * scale, -jnp.inf)
        s_n = jnp.where(valid_n, _dot_nt(qs, kn[:, ks]) * scale, -jnp.inf)
        sink = sinks_ref[hq]
        m = jnp.maximum(jnp.maximum(jnp.max(s_c, axis=-1, keepdims=True),
                                    jnp.max(s_n, axis=-1, keepdims=True)), sink)
        p_c = jnp.exp(s_c - m)
        p_n = jnp.exp(s_n - m)
        den = (jnp.sum(p_c, axis=-1, keepdims=True) + jnp.sum(p_n, axis=-1, keepdims=True)
               + jnp.exp(sink - m))
        outs.append(_dot((p_c / den).astype(BF16), cv[:, ks]) + _dot((p_n / den).astype(BF16), vn[:, ks]))
    osw_ref[...] = jnp.concatenate(outs, axis=-1)

    qm = qm_ref[...].astype(BF16)
    mk = mk_ref[...].reshape(bb * MEM_TOKENS, MEM_WIDTH).astype(BF16)
    mv = mv_ref[...].reshape(bb * MEM_TOKENS, MEM_WIDTH).astype(BF16)
    rq = lax.broadcasted_iota(jnp.int32, (nq, bb * MEM_TOKENS), 0)
    cm = lax.broadcasted_iota(jnp.int32, (nq, bb * MEM_TOKENS), 1)
    valid_m = _div(rq, ts) == _div(cm, MEM_TOKENS)
    outs = []
    for h in range(MEM_HEADS):
        sl = slice(h * HEAD_DIM, (h + 1) * HEAD_DIM)
        s = jnp.where(valid_m, _dot_nt(qm[:, sl], mk[:, sl]) * scale, -jnp.inf)
        outs.append(_softmax_pv(s, mv[:, sl]))
    omem_ref[...] = jnp.concatenate(outs, axis=-1)


def _attn_sample(sinks, q, kn, vn, ck, cv, qm, cmk, cmv, *, ts, start):
    B, w, _ = ck.shape
    bb = 8
    rows = lambda wd: pl.BlockSpec((bb * ts, wd), lambda i: (i, 0))
    blk3 = lambda n, wd: pl.BlockSpec((bb, n, wd), lambda i: (i, 0, 0))
    return pl.pallas_call(
        functools.partial(_attn_sample_kernel, bb=bb, ts=ts, start=start),
        grid=(B // bb,),
        in_specs=[pl.BlockSpec(memory_space=pltpu.SMEM),
                  rows(SWA_WIDTH), rows(SWA_KV_WIDTH), rows(SWA_KV_WIDTH),
                  blk3(w, SWA_KV_WIDTH), blk3(w, SWA_KV_WIDTH), rows(MEM_WIDTH),
                  blk3(MEM_TOKENS, MEM_WIDTH), blk3(MEM_TOKENS, MEM_WIDTH)],
        out_specs=[rows(SWA_WIDTH), rows(MEM_WIDTH)],
        out_shape=[jax.ShapeDtypeStruct((B * ts, SWA_WIDTH), F32),
                   jax.ShapeDtypeStruct((B * ts, MEM_WIDTH), F32)],
        compiler_params=_cparams(("parallel",)),
        name="attn_sample",
    )(sinks, q, kn, vn, ck, cv, qm, cmk, cmv)


def _out_proj_kernel(x_ref, ossm_ref, osw_ref, omem_ref, wo_ref, g2_ref, h_ref, xn_ref):
    h = x_ref[0]
    h = h + _dot(ossm_ref[...].astype(BF16), wo_ref[0:512, :])
    h = h + _dot(osw_ref[0].astype(BF16), wo_ref[512:768, :])
    h = h + _dot(omem_ref[0].astype(BF16), wo_ref[768:1024, :])
    h_ref[0] = h
    xn_ref[0] = _rms(h, g2_ref[...]).astype(BF16)


def _out_proj(x, ossm, osw, omem, wo, g2):
    B, T, _ = x.shape
    tt = min(TOKEN_TILE, T)
    full = lambda shape: pl.BlockSpec(shape, lambda b, t: (0,) * len(shape))
    blk = lambda wd: pl.BlockSpec((1, tt, wd), lambda b, t: (b, t, 0))
    return pl.pallas_call(
        _out_proj_kernel,
        grid=(B, T // tt),
        in_specs=[blk(D_MODEL), pl.BlockSpec((tt, SSM_WIDTH), lambda b, t: (t, b)),
                  blk(SWA_WIDTH), blk(MEM_WIDTH), full((D_MODEL, D_MODEL)), full((1, D_MODEL))],
        out_specs=[blk(D_MODEL), blk(D_MODEL)],
        out_shape=[jax.ShapeDtypeStruct((B, T, D_MODEL), F32),
                   jax.ShapeDtypeStruct((B, T, D_MODEL), BF16)],
        compiler_params=_cparams(("parallel", "parallel")),
        name="out_proj",
    )(x, ossm, osw, omem, wo, g2)


def _top16(s, iota):
    work = s
    rank = jnp.full(s.shape, float(PEER_TOPK), F32)
    vals = []
    for a in range(PEER_TOPK):
        m = jnp.max(work, axis=0, keepdims=True)
        idx = jnp.min(jnp.where(work == m, iota, float(PEER_KEYS)), axis=0, keepdims=True)
        sel = iota == idx
        rank = jnp.where(sel, float(a), rank)
        work = jnp.where(sel, -jnp.inf, work)
        vals.append(m)
    return vals, rank


def _sort_pairs(n):
    pairs = []
    t = max(1, (n - 1).bit_length())
    p = 1 << (t - 1)
    while p > 0:
        q, r, d = 1 << (t - 1), 0, p
        while d > 0:
            pairs += [(i, i + d) for i in range(n - d) if (i & p) == r]
            d, q, r = q - p, q >> 1, p
        p >>= 1
    return pairs


_SORT16_PAIRS = _sort_pairs(PEER_TOPK)


def _cmp_exchange(xs, i, j):
    xs[i], xs[j] = jnp.maximum(xs[i], xs[j]), jnp.minimum(xs[i], xs[j])


def _top16_sorted(s):
    n = PEER_TOPK
    xs = [s[8 * k:8 * k + 8, :] for k in range(n)]
    for i, j in _SORT16_PAIRS:
        _cmp_exchange(xs, i, j)
    for shift in (4, 2, 1):
        other = [pltpu.roll(x, shift, 0) for x in xs]
        xs = [jnp.maximum(xs[k], other[n - 1 - k]) for k in range(n)]
        for d in (8, 4, 2, 1):
            for k in range(n):
                if k & d == 0:
                    _cmp_exchange(xs, k, k + d)
    return xs


def _candidates(v1, v2, cand_scr):
    L = v1[0].shape[1]
    cand_scr[...] = jnp.full((_CAND_ROWS, L), -jnp.inf, F32)
    for a in range(PEER_TOPK):
        for b in range(_CAND_COUNT[a]):
            cand_scr[pl.ds(_CAND_ROW0[a] + b, 1), :] = v1[a] + v2[b]
    return cand_scr[...], lax.broadcasted_iota(jnp.int32, (_CAND_ROWS, L), 0).astype(F32)


def _taken_by_index(cand, iota_c):
    work = cand
    taken = jnp.zeros(cand.shape, F32)
    for _ in range(PEER_TOPK):
        m = jnp.max(work, axis=0, keepdims=True)
        idx = jnp.min(jnp.where(work == m, iota_c, float(_CAND_ROWS)), axis=0, keepdims=True)
        sel = iota_c == idx
        taken = jnp.where(sel, 1.0, taken)
        work = jnp.where(sel, -jnp.inf, work)
    return taken


_SORT8_PAIRS = _sort_pairs(8)


def _candidates_sorted(cand):
    n = PEER_TOPK
    xs = [cand[8 * k:8 * k + 8, :] for k in range(_CAND_ROWS // 8)]
    xs.append(jnp.full_like(xs[0], -jnp.inf))
    for i, j in _SORT8_PAIRS:
        _cmp_exchange(xs, i, j)
    other = [pltpu.roll(x, 4, 0) for x in xs]
    xs = xs + other[::-1]
    for shift in (4, 2, 1):
        if shift != 4:
            other = [pltpu.roll(x, shift, 0) for x in xs]
            xs = [jnp.maximum(xs[k], other[n - 1 - k]) for k in range(n)]
        for d in (8, 4, 2, 1):
            for k in range(n):
                if k & d == 0:
                    _cmp_exchange(xs, k, k + d)
    return xs


def _group_counts(taken, iota_c):
    cnt = []
    for a in range(PEER_TOPK):
        lo, hi = _CAND_ROW0[a], _CAND_ROW0[a] + _CAND_COUNT[a]
        if hi - lo == 1:
            cnt.append(taken[lo:hi])
            continue
        t0, t1 = (lo // 8) * 8, -(-hi // 8) * 8
        part = taken[t0:t1]
        if (lo, hi) != (t0, t1):
            rows = iota_c[t0:t1]
            part = jnp.where((rows >= float(lo)) & (rows < float(hi)), part, 0.0)
        cnt.append(jnp.sum(part, axis=0, keepdims=True))
    return cnt


def _route_tiles(tiles):
    n = PEER_TOPK
    states = []
    bad_any = None
    for s1, s2, cand_scr, n1_out, r2_out in tiles:
        t1 = _top16_sorted(s1)
        t2 = _top16_sorted(s2)
        v1 = [t[0:1, :] for t in t1]
        v2 = [t[0:1, :] for t in t2]
        cand, iota_c = _candidates(v1, v2, cand_scr)
        tc = _candidates_sorted(cand)
        z = jnp.ones_like(v1[0])
        for k in range(1, n):
            z = z + jnp.exp(tc[k][0:1, :] - tc[0][0:1, :])
        taken = jnp.where(cand >= tc[n - 1][0:1, :], 1.0, 0.0)
        cnt = _group_counts(taken, iota_c)

        n_taken = jnp.sum(taken, axis=0, keepdims=True)
        bad = jnp.broadcast_to(jnp.where(n_taken != float(n), 1.0, 0.0), (8, s1.shape[1]))
        for s, t in ((s1, t1), (s2, t2)):
            gap = t[0] - t[1]
            for a in range(1, n - 1):
                gap = jnp.minimum(gap, t[a] - t[a + 1])
            at_least = jnp.zeros((8, s.shape[1]), F32)
            for k in range(n):
                at_least = at_least + jnp.where(s[8 * k:8 * k + 8, :] >= t[n - 1], 1.0, 0.0)
            at_least = jnp.sum(at_least, axis=0, keepdims=True)
            bad = jnp.where((gap <= 0.0) | (at_least != float(n)), 1.0, bad)
        bad_any = bad if bad_any is None else jnp.maximum(bad_any, bad)
        states.append((s1, s2, t1, t2, v1, v2, cand, iota_c, cnt, z, n1_out, r2_out))
    has_ties = jnp.max(bad_any) > 0.0

    @pl.when(jnp.logical_not(has_ties))
    def _by_value():
        for s1, s2, t1, t2, _, _, _, _, cnt, _, n1_out, r2_out in states:
            cnt_b = [jnp.broadcast_to(c, (8, c.shape[1])) for c in cnt]
            ranks = []
            for k in range(n):
                x1 = s1[8 * k:8 * k + 8, :]
                x2 = s2[8 * k:8 * k + 8, :]
                n1 = cnt_b[0]
                r2 = jnp.zeros_like(x2)
                for a in range(n):
                    n1 = jnp.where(t1[a] > x1, cnt_b[a + 1] if a + 1 < n else 0.0, n1)
                    r2 = jnp.where(t2[a] > x2, float(a + 1), r2)
                n1_out[8 * k:8 * k + 8, :] = n1
                ranks.append(r2)
            r2_out[...] = jnp.concatenate(ranks, axis=0).astype(r2_out.dtype)

    @pl.when(has_ties)
    def _by_index():
        for s1, s2, _, _, _, _, cand, iota_c, _, _, n1_out, r2_out in states:
            iota = lax.broadcasted_iota(jnp.int32, s1.shape, 0).astype(F32)
            _, r1 = _top16(s1, iota)
            _, r2 = _top16(s2, iota)
            cnt_exact = _group_counts(_taken_by_index(cand, iota_c), iota_c)
            n1 = jnp.zeros(s1.shape, F32)
            for a in range(n):
                n1 = jnp.where(r1 == float(a), cnt_exact[a], n1)
            n1_out[...] = n1
            r2_out[...] = r2.astype(r2_out.dtype)

    return [(jnp.exp(st[0] - st[4][0]) / st[9], jnp.exp(st[1] - st[5][0])) for st in states]


def _peer_kernel(xn_ref, h_ref, wqt_ref, k1_ref, k2_ref, u_ref, vt_ref, y_ref,
                 n1_scr, c1_scr, s2_scr, r2_scr, e2_scr, acc_scr, a_scr, w_scr,
                 q_scr, xt_scr, cand_scr, nb_scr, cb_scr, *, n_steps):
    g = pl.program_id(1)
    ng = n_steps
    tt = xn_ref.shape[0]
    eb = u_ref.shape[0]
    keys_per_block = eb // PEER_KEYS
    n_lane_tiles = tt // LANES

    @pl.when(g == 0)
    def _route():
        xt_scr[...] = xn_ref[...].T
        xt = xt_scr[...]
        q_scr[...] = _dot(wqt_ref[...], xt).astype(BF16)
        for h in range(PEER_HEADS):
            for side, dst in enumerate((n1_scr, s2_scr)):
                k_ref = (k1_ref, k2_ref)[side]
                r0 = h * 2 * PEER_HALF + side * PEER_HALF
                s = _dot(k_ref[...], q_scr[r0:r0 + PEER_HALF, :])
                for c in range(n_lane_tiles):
                    dst[h, c] = s[:, c * LANES:(c + 1) * LANES]

        per_trip = PEER_ROUTE_TILES
        trips_per_head = n_lane_tiles // per_trip

        def body(i, carry):
            h = i // trips_per_head
            cs = [(i % trips_per_head) * per_trip + k for k in range(per_trip)]
            outs = _route_tiles([(n1_scr[h, c], s2_scr[h, c], cand_scr.at[k], n1_scr.at[h, c], r2_scr.at[h, c])
                                 for k, c in enumerate(cs)])
            for c, (c1, e2) in zip(cs, outs):
                c1_scr[h, c] = c1
                e2_scr[h, c] = e2.astype(BF16)
            return carry

        lax.fori_loop(0, PEER_HEADS * trips_per_head, body, 0)
        acc_scr[...] = jnp.zeros_like(acc_scr)

    def gate(a_scr, w_scr, block):
        zero = jnp.zeros((), BF16)
        group = PEER_GATE_KEYS
        for k0 in range(0, keys_per_block, group):
            def lane_tile(c, carry, k0=k0):
                lanes = pl.ds(pl.multiple_of(c * LANES, LANES), LANES)
                gts = [None] * group
                for h in range(PEER_HEADS):
                    r2 = r2_scr[h, c].reshape(PEER_KEYS // 16, 16, LANES)
                    e2 = e2_scr[h, c].reshape(PEER_KEYS // 16, 16, LANES)
                    for ii in range(group):
                        row = pl.ds(block * keys_per_block + k0 + ii, 1)
                        n_b = nb_scr[h, c, k0 + ii][None]
                        c_b = cb_scr[h, c, k0 + ii][None]
                        term = jnp.where(r2 < n_b, e2, zero) * c_b
                        gts[ii] = term if h == 0 else gts[ii] + term
                for ii in range(group):
                    r0 = (k0 + ii) * PEER_KEYS
                    rows = slice(r0, r0 + PEER_KEYS)
                    w_scr[rows, lanes] = gts[ii].reshape(PEER_KEYS, LANES) * a_scr[rows, lanes]
                return carry

            lax.fori_loop(0, n_lane_tiles, lane_tile, 0)

    @pl.when(g < ng)
    def _scores():
        a_scr[...] = _gelu(_dot(u_ref[...], xt_scr[...]).astype(BF16))
        for h in range(PEER_HEADS):
            for c in range(n_lane_tiles):
                for k in range(keys_per_block):
                    row = pl.ds(g * keys_per_block + k, 1)
                    nb_scr[h, c, k] = jnp.broadcast_to(n1_scr[h, c, row, :], (16, LANES)).astype(BF16)
                    cb_scr[h, c, k] = jnp.broadcast_to(c1_scr[h, c, row, :], (16, LANES)).astype(BF16)

    @pl.when(g > 0)
    def _v_product():
        acc_scr[...] += _dot(vt_ref[...], w_scr[...])

    @pl.when(g < ng)
    def _weights():
        gate(a_scr, w_scr, g)

    @pl.when(g == ng)
    def _fin():
        y_ref[...] = h_ref[...] + acc_scr[...].T


def _peer(xn, h, wqt, k1, k2, u_tab, vt_tab):
    n = xn.shape[0]
    tt = min(TOKEN_TILE, n)
    eb = PEER_EXPERT_BLOCK
    ng = PEER_EXPERTS // eb
    full = lambda shape: pl.BlockSpec(shape, lambda i, g: (0,) * len(shape), pipeline_mode=pl.Buffered(1))
    tok = pl.BlockSpec((tt, D_MODEL), lambda i, g: (i, 0))
    head_f32 = pltpu.VMEM((PEER_HEADS, tt // LANES, PEER_KEYS, LANES), F32)
    head_bf16 = pltpu.VMEM((PEER_HEADS, tt // LANES, PEER_KEYS, LANES), BF16)
    return pl.pallas_call(
        functools.partial(_peer_kernel, n_steps=ng),
        grid=(n // tt, ng + 1),
        in_specs=[tok, tok, full((2 * PEER_HEADS * PEER_HALF, D_MODEL)),
                  full((PEER_KEYS, PEER_HALF)), full((PEER_KEYS, PEER_HALF)),
                  pl.BlockSpec((eb, D_MODEL), lambda i, g: (jnp.minimum(g, ng - 1), 0)),
                  pl.BlockSpec((D_MODEL, eb), lambda i, g: (0, jnp.maximum(g - 1, 0)))],
        out_specs=tok,
        out_shape=jax.ShapeDtypeStruct((n, D_MODEL), F32),
        scratch_shapes=[head_f32, head_f32, head_f32, head_bf16, head_bf16,
                        pltpu.VMEM((D_MODEL, tt), F32),
                        pltpu.VMEM((eb, tt), BF16), pltpu.VMEM((eb, tt), BF16),
                        pltpu.VMEM((2 * PEER_HEADS * PEER_HALF, tt), BF16),
                        pltpu.VMEM((D_MODEL, tt), BF16),
                        pltpu.VMEM((PEER_ROUTE_TILES, _CAND_ROWS, LANES), F32),
                        pltpu.VMEM((PEER_HEADS, tt // LANES, eb // PEER_KEYS, 16, LANES), BF16),
                        pltpu.VMEM((PEER_HEADS, tt // LANES, eb // PEER_KEYS, 16, LANES), BF16)],
        compiler_params=_cparams(("parallel", "arbitrary")),
        name="peer",
    )(xn, h, wqt, k1, k2, u_tab, vt_tab)


def _rope_tables(pos):
    half = HEAD_DIM // 2
    inv = ROPE_THETA ** (-jnp.arange(half, dtype=F32) / half)
    ang = pos.astype(F32)[:, None] * inv[None, :]
    cos = jnp.cos(ang)
    sin = jnp.sin(ang)
    cos = jnp.tile(jnp.concatenate([cos, cos], axis=-1), (1, SWA_Q_HEADS))
    sin = jnp.tile(jnp.concatenate([-sin, sin], axis=-1), (1, SWA_Q_HEADS))
    return cos, sin


def _ssm_params(log_dt, a_re, a_im, b_re, b_im, c_re, c_im):
    dt = jnp.exp(log_dt)
    mag = jnp.exp(a_re * dt)
    lam_re = mag * jnp.cos(a_im * dt)
    lam_im = mag * jnp.sin(a_im * dt)
    den = a_re * a_re + a_im * a_im
    z_re = ((lam_re - 1.0) * a_re + lam_im * a_im) / den
    z_im = (lam_im * a_re - (lam_re - 1.0) * a_im) / den
    bb_re = z_re[..., None] * b_re - z_im[..., None] * b_im
    bb_im = z_re[..., None] * b_im + z_im[..., None] * b_re
    hg = SSM_HALF_GROUPS
    eye = jnp.eye(hg, dtype=F32)
    bb = jnp.stack([bb_re, bb_im]).reshape(2, 2, hg, SSM_STATE, SSM_GROUP)
    bmat = jnp.einsum('rjgnc,gh->jgcrhn', bb, eye).reshape(2, hg * SSM_GROUP, 2 * SSM_HALF_STATE)
    cc = jnp.stack([c_re, -c_im]).reshape(2, 2, hg, SSM_GROUP, SSM_STATE)
    cmat = jnp.einsum('rjgcn,gh->jrgnhc', cc, eye).reshape(2, 2 * SSM_HALF_STATE, hg * SSM_GROUP)
    lam = jnp.stack([lam_re.reshape(2, SSM_HALF_STATE), lam_im.reshape(2, SSM_HALF_STATE)], axis=1)
    return lam.reshape(1, SSM_COLS), bmat.astype(BF16), cmat.astype(BF16)


def _state_to_cols(s_re, s_im):
    b = s_re.shape[0]
    st = jnp.stack([s_re.reshape(b, 2, SSM_HALF_STATE), s_im.reshape(b, 2, SSM_HALF_STATE)], axis=2)
    return st.reshape(b, SSM_COLS)


def _cols_to_state(cols):
    b = cols.shape[0]
    st = cols.reshape(b, 2, 2, SSM_HALF_STATE)
    return (st[:, :, 0].reshape(b, SSM_GROUPS, SSM_STATE), st[:, :, 1].reshape(b, SSM_GROUPS, SSM_STATE))


def kernel(x_prompt, x_sample, state_ssm_re, state_ssm_im, cache_win_k, cache_win_v, cache_mem_k, cache_mem_v, mem_prompt, norm1_g, w_in, ssm_log_dt, ssm_a_re, ssm_a_im, ssm_b_re, ssm_b_im, ssm_c_re, ssm_c_im, ssm_d, w_glu, b_glu, swa_q_norm, swa_k_norm, swa_sinks, mem_norm_g, w_mem_kv, mem_q_norm, mem_k_norm, w_out, norm2_g, peer_wq, peer_k1, peer_k2, peer_u, peer_v):
    depth = norm1_g.shape[0]
    assert depth == 1
    l = 0
    B, T, _ = x_prompt.shape
    SB, ST, _ = x_sample.shape
    w = cache_win_k.shape[2]

    row = lambda a: a.reshape(1, -1)
    g1 = row(norm1_g[l])
    g2 = row(norm2_g[l])
    win = w_in[l].astype(BF16)
    wo = w_out[l].astype(BF16)
    wglu = w_glu[l].astype(BF16)
    bglu = row(b_glu[l])
    gq = row(jnp.tile(swa_q_norm[l], SWA_Q_HEADS))
    gk = row(jnp.tile(swa_k_norm[l], SWA_KV_HEADS))
    gm = row(jnp.tile(mem_q_norm[l], MEM_HEADS))
    gmk = row(jnp.tile(mem_k_norm[l], MEM_HEADS))
    gmem = row(mem_norm_g[l])
    wkv = w_mem_kv[l].astype(BF16)
    sinks = swa_sinks[l]
    head_id = np.arange(SWA_WIDTH) // HEAD_DIM
    ones = jnp.asarray(head_id[:, None] == head_id[None, :], dtype=BF16)
    lam, bmat, cmat = _ssm_params(ssm_log_dt[l], ssm_a_re[l], ssm_a_im[l], ssm_b_re[l], ssm_b_im[l],
                                  ssm_c_re[l], ssm_c_im[l])
    dskip = row(ssm_d[l])
    wqt = peer_wq[l].astype(BF16).T
    k1 = peer_k1[l].astype(BF16)
    k2 = peer_k2[l].astype(BF16)
    u_tab = peer_u[l].astype(BF16)
    vt_tab = peer_v[l].astype(BF16).T

    cos_p, sin_p = _rope_tables(jnp.arange(T, dtype=jnp.int32))
    u_p, q_p, k_p, v_p, qm_p = _in_proj(x_prompt, cos_p, sin_p, g1, win, ones, gq, gk, gm)
    zeros = jnp.zeros((B, SSM_COLS), F32)
    ossm_p, sfin_p = _s5(u_p.reshape(T * B, SSM_WIDTH), zeros, lam, bmat, cmat, dskip, wglu, bglu,
                         bt=B, tt=S5_TIME_TILE)
    mk, mv = _mem_kv(mem_prompt.reshape(B * MEM_TOKENS, D_MODEL), gmem, wkv, ones, gmk)
    mk = mk.reshape(B, MEM_TOKENS, MEM_WIDTH)
    mv = mv.reshape(B, MEM_TOKENS, MEM_WIDTH)
    osw_p, omem_p = _attn_prompt(sinks, q_p, k_p, v_p, qm_p, mk, mv)
    h_p, xn_p = _out_proj(x_prompt, ossm_p.reshape(T, B * SSM_WIDTH), osw_p, omem_p, wo, g2)
    y_p = _peer(xn_p.reshape(B * T, D_MODEL), h_p.reshape(B * T, D_MODEL), wqt, k1, k2, u_tab, vt_tab)
    y_p = y_p.reshape(B, T, D_MODEL)
    p_sr, p_si = _cols_to_state(sfin_p)
    p_wk = k_p[:, T - w:].reshape(B, w, SWA_KV_HEADS, HEAD_DIM)
    p_wv = v_p[:, T - w:].reshape(B, w, SWA_KV_HEADS, HEAD_DIM)
    p_mk = mk.reshape(B, MEM_TOKENS, MEM_HEADS, HEAD_DIM)
    p_mv = mv.reshape(B, MEM_TOKENS, MEM_HEADS, HEAD_DIM)

    n_s = SB * ST
    pos_s = PAST_LEN + jnp.tile(jnp.arange(ST, dtype=jnp.int32), SB)
    cos_s, sin_s = _rope_tables(pos_s)
    xs = x_sample.reshape(1, n_s, D_MODEL)
    u_s, q_s, k_s, v_s, qm_s = _in_proj(xs, cos_s, sin_s, g1, win, ones, gq, gk, gm)
    u_tm = u_s.reshape(SB, ST, SSM_WIDTH).transpose(1, 0, 2).reshape(n_s, SSM_WIDTH)
    ossm_tm, sfin_s = _s5(u_tm, _state_to_cols(state_ssm_re[l], state_ssm_im[l]), lam, bmat, cmat,
                          dskip, wglu, bglu, bt=SB, tt=ST)
    ossm_s = ossm_tm.reshape(ST, SB, SSM_WIDTH).transpose(1, 0, 2).reshape(n_s, SSM_WIDTH)
    ck = cache_win_k[l].reshape(SB, w, SWA_KV_WIDTH)
    cv = cache_win_v[l].reshape(SB, w, SWA_KV_WIDTH)
    q_s2 = q_s.reshape(n_s, SWA_WIDTH)
    k_s2 = k_s.reshape(n_s, SWA_KV_WIDTH)
    v_s2 = v_s.reshape(n_s, SWA_KV_WIDTH)
    osw_s, omem_s = _attn_sample(sinks, q_s2, k_s2, v_s2, ck, cv, qm_s.reshape(n_s, MEM_WIDTH),
                                 cache_mem_k[l].reshape(SB, MEM_TOKENS, MEM_WIDTH),
                                 cache_mem_v[l].reshape(SB, MEM_TOKENS, MEM_WIDTH),
                                 ts=ST, start=PAST_LEN)
    h_s, xn_s = _out_proj(xs, ossm_s, osw_s.reshape(1, n_s, SWA_WIDTH), omem_s.reshape(1, n_s, MEM_WIDTH),
                          wo, g2)
    y_s = _peer(xn_s.reshape(n_s, D_MODEL), h_s.reshape(n_s, D_MODEL), wqt, k1, k2, u_tab, vt_tab)
    y_s = y_s.reshape(SB, ST, D_MODEL)
    s_sr, s_si = _cols_to_state(sfin_s)
    s_wk = jnp.concatenate([ck, k_s2.reshape(SB, ST, SWA_KV_WIDTH)], axis=1)[:, -w:]
    s_wv = jnp.concatenate([cv, v_s2.reshape(SB, ST, SWA_KV_WIDTH)], axis=1)[:, -w:]
    s_wk = s_wk.reshape(SB, w, SWA_KV_HEADS, HEAD_DIM)
    s_wv = s_wv.reshape(SB, w, SWA_KV_HEADS, HEAD_DIM)

    st = lambda a: a[None]
    return (y_p, y_s, st(p_sr), st(p_si), st(p_wk), st(p_wv), st(p_mk), st(p_mv),
            st(s_sr), st(s_si), st(s_wk), st(s_wv))
```
